```python
import jax, jax.numpy as jnp
from jax import lax
import numpy as np

D_MODEL = 1024
BATCH = 4
SEQ = 4096
DEPTH = 1

HEAD_DIM = 64
HEADS_PER_GROUP = 4
DILATED_GROUPS = ((128, 1), (512, 4), (2048, 16))
N_GROUPS = len(DILATED_GROUPS)
N_ATTN_HEADS = N_GROUPS * HEADS_PER_GROUP
ATTN_WIDTH = N_ATTN_HEADS * HEAD_DIM
ATTN_OUT_WIDTH = HEADS_PER_GROUP * HEAD_DIM
ROT_DIM = HEAD_DIM // 4
ROPE_THETA = 500000.0
BLOCK = 128
POOL_WINDOWS = (2, 4, 8, 16)
POOL_GROUP_WIDTH = 128
POOL_WIDTH = len(POOL_WINDOWS) * POOL_GROUP_WIDTH
N_BRANCHES = 2
IN_WIDTH = 3 * ATTN_WIDTH + POOL_WIDTH + N_BRANCHES * D_MODEL
D_FF = 2816
RMS_EPS = 1e-6

kernel_name = "hybrid_dilated_attn_pool_macaron_block"


def rms_norm(x, g):
    xf = x.astype(jnp.float32)
    y = xf * lax.rsqrt(jnp.mean(xf * xf, axis=-1, keepdims=True) + RMS_EPS)
    return (y * g.astype(jnp.float32)).astype(x.dtype)


def swiglu(x, w1, w3, w2):
    return (jax.nn.silu(x @ w1) * (x @ w3)) @ w2


def rope_tables(seq):
    pos = jnp.arange(seq, dtype=jnp.float32)
    inv = ROPE_THETA ** (-jnp.arange(0, ROT_DIM, 2, dtype=jnp.float32) / ROT_DIM)
    ang = pos[:, None] * inv[None, :]
    return jnp.cos(ang), jnp.sin(ang)


def apply_partial_rope(x, cos, sin):
    xf = x.astype(jnp.float32)
    half = ROT_DIM // 2
    x1, x2 = xf[..., :half], xf[..., half:ROT_DIM]
    c, s = cos[None, :, None, :], sin[None, :, None, :]
    out = jnp.concatenate([x1 * c - x2 * s, x2 * c + x1 * s, xf[..., ROT_DIM:]], axis=-1)
    return out.astype(x.dtype)


def dilated_window_attention(q, k, v, window, dilation):
    B, S, H, Dh = q.shape
    r = dilation
    L = S // r
    n_back = window // r
    nb = -(-L // BLOCK)
    Lp = nb * BLOCK

    def to_blocks(t):
        t = t.reshape(B, L, r, H, Dh).transpose(0, 2, 1, 3, 4)
        t = jnp.pad(t, ((0, 0), (0, 0), (0, Lp - L), (0, 0), (0, 0)))
        return t.reshape(B, r, nb, BLOCK, H, Dh)

    def band(t):
        prev = jnp.pad(t[:, :, :-1], ((0, 0), (0, 0), (1, 0), (0, 0), (0, 0), (0, 0)))
        return jnp.concatenate([prev, t], axis=3)

    qb = to_blocks(q)
    kk = band(to_blocks(k))
    vv = band(to_blocks(v))
    s = jnp.einsum('brnqhd,brnkhd->brnhqk', qb, kk).astype(jnp.float32) * (Dh ** -0.5)
    qi = jnp.arange(BLOCK)[:, None]
    kj = jnp.arange(2 * BLOCK)[None, :]
    dist = qi + BLOCK - kj
    key_pos = (jnp.arange(nb)[:, None, None] - 1) * BLOCK + kj[None]
    allowed = (dist >= 0)[None] & (dist <= n_back)[None] & (key_pos >= 0)
    s = jnp.where(allowed[None, None, :, None], s, -jnp.inf)
    m = jnp.max(s, axis=-1, keepdims=True)
    p = jnp.exp(s - m)
    l = jnp.sum(p, axis=-1, keepdims=True)
    o = jnp.einsum('brnhqk,brnkhd->brnqhd', p, vv.astype(jnp.float32)) / jnp.swapaxes(l, 3, 4)
    lse = (m + jnp.log(l))[..., 0]
    o = o.reshape(B, r, Lp, H, Dh)[:, :, :L].transpose(0, 2, 1, 3, 4).reshape(B, S, H, Dh)
    lse = lse.transpose(0, 1, 2, 4, 3).reshape(B, r, Lp, H)[:, :, :L].transpose(0, 2, 1, 3).reshape(B, S, H)
    return o, lse


def dilated_mixture_attention(q, k, v):
    B, S = q.shape[:2]
    cos, sin = rope_tables(S)
    q = apply_partial_rope(q, cos, sin)
    k = apply_partial_rope(k, cos, sin)
    outs, lses = [], []
    for g, (window, dilation) in enumerate(DILATED_GROUPS):
        hs = slice(g * HEADS_PER_GROUP, (g + 1) * HEADS_PER_GROUP)
        o, lse = dilated_window_attention(q[:, :, hs], k[:, :, hs], v[:, :, hs], window, dilation)
        outs.append(o)
        lses.append(lse)
    alpha = jax.nn.softmax(jnp.stack(lses, axis=0), axis=0)
    o = jnp.sum(alpha[..., None] * jnp.stack(outs, axis=0), axis=0)
    return o.reshape(B, S, ATTN_OUT_WIDTH).astype(v.dtype)


def multiscale_pool(p, w_pool, pool_scale):
    B, S, C = p.shape
    pf = p.astype(jnp.float32)
    cs0 = jnp.concatenate([jnp.zeros((B, 1, C), jnp.float32), jnp.cumsum(pf, axis=1)], axis=1)
    outs = []
    for gi, win in enumerate(POOL_WINDOWS):
        cs = slice(gi * POOL_GROUP_WIDTH, (gi + 1) * POOL_GROUP_WIDTH)
        c = cs0[..., cs]
        lagged = jnp.pad(c[:, :S - win + 1], ((0, 0), (win - 1, 0), (0, 0)))
        cnt = jnp.minimum(jnp.arange(1, S + 1), win).astype(jnp.float32)[None, :, None]
        outs.append((c[:, 1:] - lagged) / cnt - pf[..., cs])
    d = jnp.stack(outs, axis=2).astype(p.dtype)
    y = jnp.einsum('bsgc,gcd->bsgd', d, w_pool).reshape(B, S, C)
    return y * pool_scale


def setup_inputs(seed: int = 0) -> dict:
    key = jax.random.key(seed)
    ks = jax.random.split(key, 20)
    f32 = jnp.float32

    def nrm(k, shape, fan_in):
        return jax.random.normal(k, shape, f32) * (fan_in ** -0.5)

    def gain(k, shape):
        return 1.0 + 0.02 * jax.random.normal(k, shape, f32)

    L = DEPTH
    return {
        "x": jax.random.normal(ks[0], (BATCH, SEQ, D_MODEL), f32),
        "ffn1_norm": gain(ks[1], (L, D_MODEL)),
        "ffn1_w1": nrm(ks[2], (L, D_MODEL, D_FF), D_MODEL),
        "ffn1_w3": nrm(ks[3], (L, D_MODEL, D_FF), D_MODEL),
        "ffn1_w2": nrm(ks[4], (L, D_FF, D_MODEL), D_FF),
        "mix_norm": gain(ks[5], (L, D_MODEL)),
        "w_in": nrm(ks[6], (L, D_MODEL, IN_WIDTH), D_MODEL),
        "w_branch_attn": nrm(ks[7], (L, ATTN_OUT_WIDTH, D_MODEL), ATTN_OUT_WIDTH),
        "w_branch_pool": nrm(ks[8], (L, POOL_WIDTH, D_MODEL), POOL_WIDTH),
        "pool_w": nrm(ks[9], (L, len(POOL_WINDOWS), POOL_GROUP_WIDTH, POOL_GROUP_WIDTH), POOL_GROUP_WIDTH),
        "pool_scale": gain(ks[10], (L, POOL_WIDTH)),
        "w_out": nrm(ks[11], (L, D_MODEL, D_MODEL), D_MODEL),
        "ffn2_norm": gain(ks[12], (L, D_MODEL)),
        "ffn2_w1": nrm(ks[13], (L, D_MODEL, D_FF), D_MODEL),
        "ffn2_w3": nrm(ks[14], (L, D_MODEL, D_FF), D_MODEL),
        "ffn2_w2": nrm(ks[15], (L, D_FF, D_MODEL), D_FF),
        "final_norm": gain(ks[16], (D_MODEL,)),
    }


def reference(x, ffn1_norm, ffn1_w1, ffn1_w3, ffn1_w2, mix_norm, w_in, w_branch_attn, w_branch_pool,
              pool_w, pool_scale, w_out, ffn2_norm, ffn2_w1, ffn2_w3, ffn2_w2, final_norm):
    B, S, D = x.shape
    h = x
    for l in range(DEPTH):
        h = h + 0.5 * swiglu(rms_norm(h, ffn1_norm[l]), ffn1_w1[l], ffn1_w3[l], ffn1_w2[l])
        u = rms_norm(h, mix_norm[l])
        proj = u @ w_in[l]
        o = 0
        q = proj[..., o:o + ATTN_WIDTH].reshape(B, S, N_ATTN_HEADS, HEAD_DIM); o += ATTN_WIDTH
        k = proj[..., o:o + ATTN_WIDTH].reshape(B, S, N_ATTN_HEADS, HEAD_DIM); o += ATTN_WIDTH
        v = proj[..., o:o + ATTN_WIDTH].reshape(B, S, N_ATTN_HEADS, HEAD_DIM); o += ATTN_WIDTH
        pz = proj[..., o:o + POOL_WIDTH]; o += POOL_WIDTH
        gates = jax.nn.sigmoid(proj[..., o:o + N_BRANCHES * D_MODEL].reshape(B, S, N_BRANCHES, D_MODEL))
        y_attn = dilated_mixture_attention(q, k, v) @ w_branch_attn[l]
        y_pool = multiscale_pool(pz, pool_w[l], pool_scale[l]) @ w_branch_pool[l]
        merged = gates[:, :, 0] * y_attn + gates[:, :, 1] * y_pool
        h = h + merged @ w_out[l]
        h = h + 0.5 * swiglu(rms_norm(h, ffn2_norm[l]), ffn2_w1[l], ffn2_w3[l], ffn2_w2[l])
    return rms_norm(h, final_norm)
```

```python
import functools

import jax
import jax.numpy as jnp
from jax import lax
from jax.experimental import pallas as pl
from jax.experimental.pallas import tpu as pltpu

F32 = jnp.float32
BF16 = jnp.bfloat16

HEAD_DIM = 64
HEADS_PER_GROUP = 4
GROUP_WIDTH = HEADS_PER_GROUP * HEAD_DIM
DILATED_GROUPS = ((128, 1), (512, 4), (2048, 16))
N_GROUPS = len(DILATED_GROUPS)
ATTN_BLOCK = 128
ROT_DIM = HEAD_DIM // 4
ROPE_THETA = 500000.0
POOL_WINDOWS = (2, 4, 8, 16)
POOL_GROUP_WIDTH = 128
POOL_WIDTH = len(POOL_WINDOWS) * POOL_GROUP_WIDTH
POOL_HALO = max(POOL_WINDOWS)
RMS_EPS = 1e-6

LANES = 128
V7X_VMEM_BYTES = 64 * 1024 * 1024

TOKEN_TILE = 512
FF_CHUNK = 256
ATTN_ROWS = 1024
VMEM_LIMIT = 56 * 1024 * 1024


def _rms(x, gain):
    return x * lax.rsqrt(jnp.mean(x * x, axis=-1, keepdims=True) + RMS_EPS) * gain


def _swiglu_into(u, w1_ref, w3_ref, w2_ref, acc_ref):
    d_ff = w1_ref.shape[1]
    for f in range(d_ff // FF_CHUNK):
        cols = slice(f * FF_CHUNK, (f + 1) * FF_CHUNK)
        a = jnp.dot(u, w1_ref[:, cols], preferred_element_type=F32)
        b = jnp.dot(u, w3_ref[:, cols], preferred_element_type=F32)
        hid = (a * jax.nn.sigmoid(a) * b).astype(BF16)
        part = jnp.dot(hid, w2_ref[cols, :], preferred_element_type=F32)
        if f == 0:
            acc_ref[...] = part
        else:
            acc_ref[...] += part


def _rope(x, cos, sin):
    j = lax.broadcasted_iota(jnp.int32, x.shape, 1) % HEAD_DIM
    half = ROT_DIM // 2
    partner = jnp.where(j < half, pltpu.roll(x, LANES - half, axis=1), pltpu.roll(x, half, axis=1))
    return jnp.where(j < ROT_DIM, x * cos + partner * sin, x)


def _ffn1_proj_kernel(x_ref, g1_ref, w1_ref, w3_ref, w2_ref, gm_ref, win_ref, cos_ref, sin_ref,
                      h1_ref, q0_ref, q1_ref, q2_ref, k0_ref, k1_ref, k2_ref, v0_ref, v1_ref, v2_ref,
                      pz_ref, gate_ref, acc_ref, slab_ref):
    tm = x_ref.shape[0]
    x = x_ref[...]
    u = _rms(x, g1_ref[...]).astype(BF16)
    _swiglu_into(u, w1_ref, w3_ref, w2_ref, acc_ref)
    h1 = x + 0.5 * acc_ref[...]
    h1_ref[...] = h1
    um = _rms(h1, gm_ref[...]).astype(BF16)
    cos = cos_ref[...]
    sin = sin_ref[...]

    def project(col):
        return jnp.dot(um, win_ref[:, col:col + GROUP_WIDTH], preferred_element_type=F32)

    def rope(p):
        return jnp.concatenate([_rope(p[:, :LANES], cos, sin), _rope(p[:, LANES:], cos, sin)], axis=1)

    col = 0
    scratch_slot = 0
    for kind, refs in (("q", (q0_ref, q1_ref, q2_ref)), ("k", (k0_ref, k1_ref, k2_ref)),
                       ("v", (v0_ref, v1_ref, v2_ref))):
        for (_, r), ref in zip(DILATED_GROUPS, refs):
            p = project(col)
            col += GROUP_WIDTH
            if kind == "q":
                p = rope(p) * (HEAD_DIM ** -0.5)
            elif kind == "k":
                p = rope(p)
            if r == 1:
                ref[0, 0] = p.astype(BF16)
            else:
                for half in range(GROUP_WIDTH // LANES):
                    lanes = slice(half * LANES, (half + 1) * LANES)
                    slab_ref[scratch_slot] = p[:, lanes]
                    for c in range(r):
                        ref[0, c, :, lanes] = slab_ref[scratch_slot, pl.ds(c, tm // r, stride=r), :].astype(BF16)
                    scratch_slot += 1
    for s in range(POOL_WIDTH // GROUP_WIDTH):
        pz_ref[:, s * GROUP_WIDTH:(s + 1) * GROUP_WIDTH] = project(col).astype(BF16)
        col += GROUP_WIDTH
    for s in range(gate_ref.shape[1] // GROUP_WIDTH):
        gate_ref[:, s * GROUP_WIDTH:(s + 1) * GROUP_WIDTH] = jax.nn.sigmoid(project(col)).astype(BF16)
        col += GROUP_WIDTH


def _attn_kernel(q_ref, k_ref, v_ref, kh_ref, vh_ref, o_ref, lse_ref, *, blocks_per_residue, n_back):
    rows = q_ref.shape[1]
    nblk = rows // ATTN_BLOCK
    step = pl.program_id(1)
    qi = lax.broadcasted_iota(jnp.int32, (ATTN_BLOCK, 2 * ATTN_BLOCK), 0)
    kj = lax.broadcasted_iota(jnp.int32, (ATTN_BLOCK, 2 * ATTN_BLOCK), 1)
    dist = qi + ATTN_BLOCK - kj
    band = (dist >= 0) & (dist <= n_back)
    in_current = kj >= ATTN_BLOCK
    head_of_lane = lax.broadcasted_iota(jnp.int32, (1, GROUP_WIDTH), 1) // HEAD_DIM
    for j in range(nblk):
        has_prev = ((step * nblk + j) % blocks_per_residue) != 0
        allowed = band & (in_current | has_prev)
        cur = slice(j * ATTN_BLOCK, (j + 1) * ATTN_BLOCK)
        qb = q_ref[0, cur, :]
        if j == 0:
            kk = jnp.concatenate([kh_ref[0], k_ref[0, cur, :]], axis=0)
            vv = jnp.concatenate([vh_ref[0], v_ref[0, cur, :]], axis=0)
        else:
            both = slice((j - 1) * ATTN_BLOCK, (j + 1) * ATTN_BLOCK)
            kk = k_ref[0, both, :]
            vv = v_ref[0, both, :]
        o_acc = jnp.zeros((ATTN_BLOCK, GROUP_WIDTH), F32)
        lse_acc = jnp.zeros((ATTN_BLOCK, GROUP_WIDTH), F32)
        for h in range(HEADS_PER_GROUP):
            mine = head_of_lane == h
            qh = jnp.where(mine, qb, jnp.zeros_like(qb))
            s = lax.dot_general(qh, kk, (((1,), (1,)), ((), ())), preferred_element_type=F32)
            s = jnp.where(allowed, s, -jnp.inf)
            m = jnp.max(s, axis=-1, keepdims=True)
            p = jnp.exp(s - m)
            l = jnp.sum(p, axis=-1, keepdims=True)
            pv = jnp.dot(p.astype(BF16), vv, preferred_element_type=F32)
            o_acc = jnp.where(mine, pv / l, o_acc)
            lse_acc = jnp.where(mine, m + jnp.log(l), lse_acc)
        o_ref[0, cur, :] = o_acc.astype(BF16)
        lse_ref[0, cur, :] = lse_acc


def _merge_ffn2_kernel(h1_ref, o0_ref, o1_ref, o2_ref, l0_ref, l1_ref, l2_ref, pz_ref, pzh_ref, gate_ref,
                       wa_ref, poolw_ref, pscale_ref, wb_ref, wo_ref, g2_ref, w1_ref, w3_ref, w2_ref, gf_ref,
                       out_ref, acc_ref, nat_ref, pext_ref, *, tiles_per_seq):
    tm = h1_ref.shape[0]
    t = pl.program_id(0) % tiles_per_seq

    def natural_order(ref, r, slot):
        if r == 1:
            return ref[0, 0].astype(F32)
        halves = []
        for half in range(GROUP_WIDTH // LANES):
            lanes = slice(half * LANES, (half + 1) * LANES)
            for c in range(r):
                nat_ref[slot + half, pl.ds(c, tm // r, stride=r), :] = ref[0, c, :, lanes].astype(F32)
            halves.append(nat_ref[slot + half])
        return jnp.concatenate(halves, axis=1)

    outs, lses = [], []
    slot = 0
    halves_per_slab = GROUP_WIDTH // LANES
    for (_, r), o_ref, l_ref in zip(DILATED_GROUPS, (o0_ref, o1_ref, o2_ref), (l0_ref, l1_ref, l2_ref)):
        outs.append(natural_order(o_ref, r, slot))
        lses.append(natural_order(l_ref, r, slot + halves_per_slab))
        if r != 1:
            slot += 2 * halves_per_slab
    top = functools.reduce(jnp.maximum, lses)
    weights = [jnp.exp(l - top) for l in lses]
    o = sum(w * og for w, og in zip(weights, outs)) / sum(weights)
    y_attn = jnp.dot(o.astype(BF16), wa_ref[...], preferred_element_type=F32)

    halo = pzh_ref[...].astype(F32)
    pext_ref[0:POOL_HALO, :] = jnp.where(t > 0, halo, jnp.zeros_like(halo))
    pext_ref[POOL_HALO:, :] = pz_ref[...].astype(F32)
    pos = t * tm + lax.broadcasted_iota(jnp.int32, (tm, 1), 0)
    pooled = []
    for gi, win in enumerate(POOL_WINDOWS):
        lanes = slice(gi * POOL_GROUP_WIDTH, (gi + 1) * POOL_GROUP_WIDTH)
        own = pext_ref[POOL_HALO:POOL_HALO + tm, lanes]
        total = own
        for back in range(1, win):
            total = total + pext_ref[POOL_HALO - back:POOL_HALO - back + tm, lanes]
        count = jnp.minimum(pos + 1, win).astype(F32)
        d = total / count - own
        y = jnp.dot(d.astype(BF16), poolw_ref[gi], preferred_element_type=F32)
        pooled.append(y * pscale_ref[:, lanes])
    y_pool = jnp.dot(jnp.concatenate(pooled, axis=1).astype(BF16), wb_ref[...], preferred_element_type=F32)

    d_model = h1_ref.shape[1]
    merged = (gate_ref[:, :d_model].astype(F32) * y_attn + gate_ref[:, d_model:].astype(F32) * y_pool)
    h2 = h1_ref[...] + jnp.dot(merged.astype(BF16), wo_ref[...], preferred_element_type=F32)
    u2 = _rms(h2, g2_ref[...]).astype(BF16)
    _swiglu_into(u2, w1_ref, w3_ref, w2_ref, acc_ref)
    h3 = h2 + 0.5 * acc_ref[...]
    out_ref[...] = _rms(h3, gf_ref[...])


def _resident(shape):
    return pl.BlockSpec(shape, lambda *_: (0,) * len(shape), pipeline_mode=pl.Buffered(1))


def _rope_tables(seq):
    pos = jnp.arange(seq, dtype=F32)
    inv = ROPE_THETA ** (-jnp.arange(0, ROT_DIM, 2, dtype=F32) / ROT_DIM)
    ang = pos[:, None] * inv[None, :]
    cos, sin = jnp.cos(ang), jnp.sin(ang)
    rest = HEAD_DIM - ROT_DIM
    cos_h = jnp.concatenate([cos, cos, jnp.ones((seq, rest), F32)], axis=1)
    sin_h = jnp.concatenate([-sin, sin, jnp.zeros((seq, rest), F32)], axis=1)
    reps = LANES // HEAD_DIM
    return jnp.tile(cos_h, (1, reps)), jnp.tile(sin_h, (1, reps))


def _layer(h, B, S, ffn1_norm, ffn1_w1, ffn1_w3, ffn1_w2, mix_norm, w_in, w_branch_attn, w_branch_pool,
           pool_w, pool_scale, w_out, ffn2_norm, ffn2_w1, ffn2_w3, ffn2_w2, final_gain, cos_t, sin_t):
    T, D = h.shape
    d_ff = ffn1_w1.shape[1]
    in_width = w_in.shape[1]
    gate_width = in_width - 3 * N_GROUPS * GROUP_WIDTH - POOL_WIDTH
    tm = TOKEN_TILE
    tiles_per_seq = S // tm
    n_tiles = T // tm
    bf = lambda w: w.astype(BF16)
    row = lambda g: g.reshape(1, -1).astype(F32)
    params = pltpu.CompilerParams(dimension_semantics=("arbitrary",), vmem_limit_bytes=VMEM_LIMIT)

    tok = lambda width: pl.BlockSpec((tm, width), lambda i: (i, 0))

    def residue_major(width, r):
        return pl.BlockSpec((1, r, tm // r, width), lambda i: (i // tiles_per_seq, 0, i % tiles_per_seq, 0))

    qkv_shapes = [jax.ShapeDtypeStruct((B, r, S // r, GROUP_WIDTH), BF16) for _, r in DILATED_GROUPS]
    qkv_specs = [residue_major(GROUP_WIDTH, r) for _, r in DILATED_GROUPS]
    outs = pl.pallas_call(
        _ffn1_proj_kernel,
        name="ffn1_proj",
        grid=(n_tiles,),
        in_specs=[tok(D), _resident((1, D)), _resident((D, d_ff)), _resident((D, d_ff)), _resident((d_ff, D)),
                  _resident((1, D)), _resident((D, in_width)),
                  pl.BlockSpec((tm, LANES), lambda i: (i % tiles_per_seq, 0)),
                  pl.BlockSpec((tm, LANES), lambda i: (i % tiles_per_seq, 0))],
        out_specs=[tok(D)] + qkv_specs * 3 + [tok(POOL_WIDTH), tok(gate_width)],
        out_shape=[jax.ShapeDtypeStruct((T, D), F32)] + qkv_shapes * 3
        + [jax.ShapeDtypeStruct((T, POOL_WIDTH), BF16), jax.ShapeDtypeStruct((T, gate_width), BF16)],
        scratch_shapes=[pltpu.VMEM((tm, D), F32),
                        pltpu.VMEM((3 * (N_GROUPS - 1) * GROUP_WIDTH // LANES, tm, LANES), F32)],
        compiler_params=params,
    )(h, row(ffn1_norm), bf(ffn1_w1), bf(ffn1_w3), bf(ffn1_w2), row(mix_norm), bf(w_in), cos_t, sin_t)
    h1 = outs[0]
    qs, ks, vs = outs[1:4], outs[4:7], outs[7:10]
    pz, gates = outs[10], outs[11]

    attn_o, attn_lse = [], []
    rows = ATTN_ROWS
    blocks_per_step = rows // ATTN_BLOCK
    for g, (window, r) in enumerate(DILATED_GROUPS):
        L = S // r
        flat = lambda a: a.reshape(B, S, GROUP_WIDTH)
        cur = pl.BlockSpec((1, rows, GROUP_WIDTH), lambda b, s: (b, s, 0))
        prev = pl.BlockSpec((1, ATTN_BLOCK, GROUP_WIDTH),
                            lambda b, s: (b, jnp.maximum(s * blocks_per_step - 1, 0), 0))
        o_g, lse_g = pl.pallas_call(
            functools.partial(_attn_kernel, blocks_per_residue=L // ATTN_BLOCK, n_back=window // r),
            name=f"dilated_attn_g{g}",
            grid=(B, S // rows),
            in_specs=[cur, cur, cur, prev, prev],
            out_specs=[cur, cur],
            out_shape=[jax.ShapeDtypeStruct((B, S, GROUP_WIDTH), BF16),
                       jax.ShapeDtypeStruct((B, S, GROUP_WIDTH), F32)],
            compiler_params=pltpu.CompilerParams(dimension_semantics=("arbitrary", "arbitrary")),
        )(flat(qs[g]), flat(ks[g]), flat(vs[g]), flat(ks[g]), flat(vs[g]))
        attn_o.append(o_g.reshape(B, r, L, GROUP_WIDTH))
        attn_lse.append(lse_g.reshape(B, r, L, GROUP_WIDTH))

    o_specs = [residue_major(GROUP_WIDTH, r) for _, r in DILATED_GROUPS]
    return pl.pallas_call(
        functools.partial(_merge_ffn2_kernel, tiles_per_seq=tiles_per_seq),
        name="merge_ffn2",
        grid=(n_tiles,),
        in_specs=[tok(D)] + o_specs + o_specs
        + [tok(POOL_WIDTH),
           pl.BlockSpec((POOL_HALO, POOL_WIDTH), lambda i: (jnp.maximum(i * (tm // POOL_HALO) - 1, 0), 0)),
           tok(gate_width),
           _resident(w_branch_attn.shape), _resident(pool_w.shape), _resident((1, POOL_WIDTH)),
           _resident(w_branch_pool.shape), _resident(w_out.shape), _resident((1, D)),
           _resident((D, d_ff)), _resident((D, d_ff)), _resident((d_ff, D)), _resident((1, D))],
        out_specs=tok(D),
        out_shape=jax.ShapeDtypeStruct((T, D), F32),
        scratch_shapes=[pltpu.VMEM((tm, D), F32),
                        pltpu.VMEM((2 * (N_GROUPS - 1) * GROUP_WIDTH // LANES, tm, LANES), F32),
                        pltpu.VMEM((POOL_HALO + tm, POOL_WIDTH), F32)],
        compiler_params=params,
    )(h1, *attn_o, *attn_lse, pz, pz, gates, bf(w_branch_attn), bf(pool_w), row(pool_scale),
      bf(w_branch_pool), bf(w_out), row(ffn2_norm), bf(ffn2_w1), bf(ffn2_w3), bf(ffn2_w2), row(final_gain))


def kernel(x, ffn1_norm, ffn1_w1, ffn1_w3, ffn1_w2, mix_norm, w_in, w_branch_attn, w_branch_pool, pool_w,
           pool_scale, w_out, ffn2_norm, ffn2_w1, ffn2_w3, ffn2_w2, final_norm):
    B, S, D = x.shape
    depth = ffn1_norm.shape[0]
    assert S % TOKEN_TILE == 0 and S % ATTN_ROWS == 0
    assert all(window // r == ATTN_BLOCK and (S // r) % ATTN_BLOCK == 0 for window, r in DILATED_GROUPS)
    assert all(TOKEN_TILE % (16 * r) == 0 for _, r in DILATED_GROUPS)
    cos_t, sin_t = _rope_tables(S)
    h = x.reshape(B * S, D)
    for l in range(depth):
        assert l == depth - 1, "only the last layer's output norm is implemented"
        h = _layer(h, B, S, ffn1_norm[l], ffn1_w1[l], ffn1_w3[l], ffn1_w2[l], mix_norm[l], w_in[l],
                   w_branch_attn[l], w_branch_pool[l], pool_w[l], pool_scale[l], w_out[l], ffn2_norm[l],
                   ffn2_w1[l], ffn2_w3[l], ffn2_w2[l], final_norm, cos_t, sin_t)
    return h.reshape(B, S, D)
```

```python
import functools

import jax
import jax.numpy as jnp
from jax import lax
from jax.experimental import pallas as pl
from jax.experimental.pallas import tpu as pltpu

F32 = jnp.float32
BF16 = jnp.bfloat16

HEAD_DIM = 64
HEADS_PER_GROUP = 4
GROUP_WIDTH = HEADS_PER_GROUP * HEAD_DIM
DILATED_GROUPS = ((128, 1), (512, 4), (2048, 16))
N_GROUPS = len(DILATED_GROUPS)
ATTN_BLOCK = 128
ROT_DIM = HEAD_DIM // 4
ROPE_THETA = 500000.0
POOL_WINDOWS = (2, 4, 8, 16)
POOL_GROUP_WIDTH = 128
POOL_WIDTH = len(POOL_WINDOWS) * POOL_GROUP_WIDTH
POOL_HALO = max(POOL_WINDOWS)
RMS_EPS = 1e-6

LANES = 128
V7X_VMEM_BYTES = 64 * 1024 * 1024

TOKEN_TILE = 512
FF_CHUNK = 256
ATTN_ROWS = 1024
VMEM_LIMIT = 56 * 1024 * 1024


def _rms(x, gain):
    return x * lax.rsqrt(jnp.mean(x * x, axis=-1, keepdims=True) + RMS_EPS) * gain


def _swiglu_into(u, w1_ref, w3_ref, w2_ref, acc_ref):
    d_ff = w1_ref.shape[1]
    for f in range(d_ff // FF_CHUNK):
        cols = slice(f * FF_CHUNK, (f + 1) * FF_CHUNK)
        a = jnp.dot(u, w1_ref[:, cols], preferred_element_type=F32)
        b = jnp.dot(u, w3_ref[:, cols], preferred_element_type=F32)
        hid = (a * jax.nn.sigmoid(a) * b).astype(BF16)
        part = jnp.dot(hid, w2_ref[cols, :], preferred_element_type=F32)
        if f == 0:
            acc_ref[...] = part
        else:
            acc_ref[...] += part


def _rope(x, cos, sin):
    j = lax.broadcasted_iota(jnp.int32, x.shape, 1) % HEAD_DIM
    half = ROT_DIM // 2
    partner = jnp.where(j < half, pltpu.roll(x, LANES - half, axis=1), pltpu.roll(x, half, axis=1))
    return jnp.where(j < ROT_DIM, x * cos + partner * sin, x)


def _ffn1_proj_kernel(x_ref, g1_ref, w1_ref, w3_ref, w2_ref, gm_ref, win_ref, cos_ref, sin_ref,
                      h1_ref, q0_ref, q1_ref, q2_ref, k0_ref, k1_ref, k2_ref, v0_ref, v1_ref, v2_ref,
                      pz_ref, gate_ref, acc_ref, slab_ref):
    tm = x_ref.shape[0]
    x = x_ref[...]
    u = _rms(x, g1_ref[...]).astype(BF16)
    _swiglu_into(u, w1_ref, w3_ref, w2_ref, acc_ref)
    h1 = x + 0.5 * acc_ref[...]
    h1_ref[...] = h1
    um = _rms(h1, gm_ref[...]).astype(BF16)
    cos = cos_ref[...]
    sin = sin_ref[...]

    def project(col):
        return jnp.dot(um, win_ref[:, col:col + GROUP_WIDTH], preferred_element_type=F32)

    def rope(p):
        return jnp.concatenate([_rope(p[:, :LANES], cos, sin), _rope(p[:, LANES:], cos, sin)], axis=1)

    col = 0
    scratch_slot = 0
    for kind, refs in (("q", (q0_ref, q1_ref, q2_ref)), ("k", (k0_ref, k1_ref, k2_ref)),
                       ("v", (v0_ref, v1_ref, v2_ref))):
        for (_, r), ref in zip(DILATED_GROUPS, refs):
            p = project(col)
            col += GROUP_WIDTH
            if kind == "q":
                p = rope(p) * (HEAD_DIM ** -0.5)
            elif kind == "k":
                p = rope(p)
            if r == 1:
                ref[0, 0] = p.astype(BF16)
            else:
                for half in range(GROUP_WIDTH // LANES):
                    lanes = slice(half * LANES, (half + 1) * LANES)
                    slab_ref[scratch_slot] = p[:, lanes]
                    for c in range(r):
                        ref[0, c, :, lanes] = slab_ref[scratch_slot, pl.ds(c, tm // r, stride=r), :].astype(BF16)
                    scratch_slot += 1
    for s in range(POOL_WIDTH // GROUP_WIDTH):
        pz_ref[:, s * GROUP_WIDTH:(s + 1) * GROUP_WIDTH] = project(col).astype(BF16)
        col += GROUP_WIDTH
    for s in range(gate_ref.shape[1] // GROUP_WIDTH):
        gate_ref[:, s * GROUP_WIDTH:(s + 1) * GROUP_WIDTH] = jax.nn.sigmoid(project(col)).astype(BF16)
        col += GROUP_WIDTH


def _attn_kernel(q_ref, k_ref, v_ref, kh_ref, vh_ref, o_ref, lse_ref, *, blocks_per_residue, n_back):
    rows = q_ref.shape[1]
    nblk = rows // ATTN_BLOCK
    step = pl.program_id(1)
    stacked = HEADS_PER_GROUP * ATTN_BLOCK
    qi = lax.broadcasted_iota(jnp.int32, (stacked, 2 * ATTN_BLOCK), 0) % ATTN_BLOCK
    kj = lax.broadcasted_iota(jnp.int32, (stacked, 2 * ATTN_BLOCK), 1)
    dist = qi + ATTN_BLOCK - kj
    band = (dist >= 0) & (dist <= n_back)
    in_current = kj >= ATTN_BLOCK
    head_of_lane = lax.broadcasted_iota(jnp.int32, (1, GROUP_WIDTH), 1) // HEAD_DIM
    heads = range(HEADS_PER_GROUP)

    def by_head(rows_of_head):
        out = rows_of_head[0:ATTN_BLOCK]
        for h in heads[1:]:
            out = jnp.where(head_of_lane == h, rows_of_head[h * ATTN_BLOCK:(h + 1) * ATTN_BLOCK], out)
        return out

    for j in range(nblk):
        has_prev = ((step * nblk + j) % blocks_per_residue) != 0
        allowed = band & (in_current | has_prev)
        cur = slice(j * ATTN_BLOCK, (j + 1) * ATTN_BLOCK)
        qb = q_ref[0, cur, :]
        if j == 0:
            kk = jnp.concatenate([kh_ref[0], k_ref[0, cur, :]], axis=0)
            vv = jnp.concatenate([vh_ref[0], v_ref[0, cur, :]], axis=0)
        else:
            both = slice((j - 1) * ATTN_BLOCK, (j + 1) * ATTN_BLOCK)
            kk = k_ref[0, both, :]
            vv = v_ref[0, both, :]
        qs = jnp.concatenate([jnp.where(head_of_lane == h, qb, jnp.zeros_like(qb)) for h in heads], axis=0)
        s = lax.dot_general(qs, kk, (((1,), (1,)), ((), ())), preferred_element_type=F32)
        s = jnp.where(allowed, s, -jnp.inf)
        m = jnp.max(s, axis=-1, keepdims=True)
        p = jnp.exp(s - m)
        l = jnp.sum(p, axis=-1, keepdims=True)
        pv = jnp.dot(p.astype(BF16), vv, preferred_element_type=F32)
        o_ref[0, cur, :] = by_head(pv * (1.0 / l)).astype(BF16)
        lse_ref[0, cur, :] = by_head(jnp.broadcast_to(m + jnp.log(l), pv.shape))


def _merge_ffn2_kernel(h1_ref, o0_ref, o1_ref, o2_ref, l0_ref, l1_ref, l2_ref, pz_ref, pzh_ref, gate_ref,
                       wa_ref, poolw_ref, pscale_ref, wb_ref, wo_ref, g2_ref, w1_ref, w3_ref, w2_ref, gf_ref,
                       out_ref, acc_ref, nat_ref, pext_ref, *, tiles_per_seq):
    tm = h1_ref.shape[0]
    t = pl.program_id(0) % tiles_per_seq

    def natural_order(ref, r, slot):
        if r == 1:
            return ref[0, 0].astype(F32)
        halves = []
        for half in range(GROUP_WIDTH // LANES):
            lanes = slice(half * LANES, (half + 1) * LANES)
            for c in range(r):
                nat_ref[slot + half, pl.ds(c, tm // r, stride=r), :] = ref[0, c, :, lanes].astype(F32)
            halves.append(nat_ref[slot + half])
        return jnp.concatenate(halves, axis=1)

    outs, lses = [], []
    slot = 0
    halves_per_slab = GROUP_WIDTH // LANES
    for (_, r), o_ref, l_ref in zip(DILATED_GROUPS, (o0_ref, o1_ref, o2_ref), (l0_ref, l1_ref, l2_ref)):
        outs.append(natural_order(o_ref, r, slot))
        lses.append(natural_order(l_ref, r, slot + halves_per_slab))
        if r != 1:
            slot += 2 * halves_per_slab
    top = functools.reduce(jnp.maximum, lses)
    weights = [jnp.exp(l - top) for l in lses]
    o = sum(w * og for w, og in zip(weights, outs)) / sum(weights)
    y_attn = jnp.dot(o.astype(BF16), wa_ref[...], preferred_element_type=F32)

    halo = pzh_ref[...].astype(F32)
    pext_ref[0:POOL_HALO, :] = jnp.where(t > 0, halo, jnp.zeros_like(halo))
    pext_ref[POOL_HALO:, :] = pz_ref[...].astype(F32)
    pos = t * tm + lax.broadcasted_iota(jnp.int32, (tm, 1), 0)
    pooled = []
    for gi, win in enumerate(POOL_WINDOWS):
        lanes = slice(gi * POOL_GROUP_WIDTH, (gi + 1) * POOL_GROUP_WIDTH)
        own = pext_ref[POOL_HALO:POOL_HALO + tm, lanes]
        total = own
        for back in range(1, win):
            total = total + pext_ref[POOL_HALO - back:POOL_HALO - back + tm, lanes]
        count = jnp.minimum(pos + 1, win).astype(F32)
        d = total / count - own
        y = jnp.dot(d.astype(BF16), poolw_ref[gi], preferred_element_type=F32)
        pooled.append(y * pscale_ref[:, lanes])
    y_pool = jnp.dot(jnp.concatenate(pooled, axis=1).astype(BF16), wb_ref[...], preferred_element_type=F32)

    d_model = h1_ref.shape[1]
    merged = (gate_ref[:, :d_model].astype(F32) * y_attn + gate_ref[:, d_model:].astype(F32) * y_pool)
    h2 = h1_ref[...] + jnp.dot(merged.astype(BF16), wo_ref[...], preferred_element_type=F32)
    u2 = _rms(h2, g2_ref[...]).astype(BF16)
    _swiglu_into(u2, w1_ref, w3_ref, w2_ref, acc_ref)
    h3 = h2 + 0.5 * acc_ref[...]
    out_ref[...] = _rms(h3, gf_ref[...])


def _resident(shape):
    return pl.BlockSpec(shape, lambda *_: (0,) * len(shape), pipeline_mode=pl.Buffered(1))


def _rope_tables(seq):
    pos = jnp.arange(seq, dtype=F32)
    inv = ROPE_THETA ** (-jnp.arange(0, ROT_DIM, 2, dtype=F32) / ROT_DIM)
    ang = pos[:, None] * inv[None, :]
    cos, sin = jnp.cos(ang), jnp.sin(ang)
    rest = HEAD_DIM - ROT_DIM
    cos_h = jnp.concatenate([cos, cos, jnp.ones((seq, rest), F32)], axis=1)
    sin_h = jnp.concatenate([-sin, sin, jnp.zeros((seq, rest), F32)], axis=1)
    reps = LANES // HEAD_DIM
    return jnp.tile(cos_h, (1, reps)), jnp.tile(sin_h, (1, reps))


def _layer(h, B, S, ffn1_norm, ffn1_w1, ffn1_w3, ffn1_w2, mix_norm, w_in, w_branch_attn, w_branch_pool,
           pool_w, pool_scale, w_out, ffn2_norm, ffn2_w1, ffn2_w3, ffn2_w2, final_gain, cos_t, sin_t):
    T, D = h.shape
    d_ff = ffn1_w1.shape[1]
    in_width = w_in.shape[1]
    gate_width = in_width - 3 * N_GROUPS * GROUP_WIDTH - POOL_WIDTH
    tm = TOKEN_TILE
    tiles_per_seq = S // tm
    n_tiles = T // tm
    bf = lambda w: w.astype(BF16)
    row = lambda g: g.reshape(1, -1).astype(F32)
    params = pltpu.CompilerParams(dimension_semantics=("arbitrary",), vmem_limit_bytes=VMEM_LIMIT)

    tok = lambda width: pl.BlockSpec((tm, width), lambda i: (i, 0))

    def residue_major(width, r):
        return pl.BlockSpec((1, r, tm // r, width), lambda i: (i // tiles_per_seq, 0, i % tiles_per_seq, 0))

    qkv_shapes = [jax.ShapeDtypeStruct((B, r, S // r, GROUP_WIDTH), BF16) for _, r in DILATED_GROUPS]
    qkv_specs = [residue_major(GROUP_WIDTH, r) for _, r in DILATED_GROUPS]
    outs = pl.pallas_call(
        _ffn1_proj_kernel,
        name="ffn1_proj",
        grid=(n_tiles,),
        in_specs=[tok(D), _resident((1, D)), _resident((D, d_ff)), _resident((D, d_ff)), _resident((d_ff, D)),
                  _resident((1, D)), _resident((D, in_width)),
                  pl.BlockSpec((tm, LANES), lambda i: (i % tiles_per_seq, 0)),
                  pl.BlockSpec((tm, LANES), lambda i: (i % tiles_per_seq, 0))],
        out_specs=[tok(D)] + qkv_specs * 3 + [tok(POOL_WIDTH), tok(gate_width)],
        out_shape=[jax.ShapeDtypeStruct((T, D), F32)] + qkv_shapes * 3
        + [jax.ShapeDtypeStruct((T, POOL_WIDTH), BF16), jax.ShapeDtypeStruct((T, gate_width), BF16)],
        scratch_shapes=[pltpu.VMEM((tm, D), F32),
                        pltpu.VMEM((3 * (N_GROUPS - 1) * GROUP_WIDTH // LANES, tm, LANES), F32)],
        compiler_params=params,
    )(h, row(ffn1_norm), bf(ffn1_w1), bf(ffn1_w3), bf(ffn1_w2), row(mix_norm), bf(w_in), cos_t, sin_t)
    h1 = outs[0]
    qs, ks, vs = outs[1:4], outs[4:7], outs[7:10]
    pz, gates = outs[10], outs[11]

    attn_o, attn_lse = [], []
    rows = ATTN_ROWS
    blocks_per_step = rows // ATTN_BLOCK
    for g, (window, r) in enumerate(DILATED_GROUPS):
        L = S // r
        flat = lambda a: a.reshape(B, S, GROUP_WIDTH)
        cur = pl.BlockSpec((1, rows, GROUP_WIDTH), lambda b, s: (b, s, 0))
        prev = pl.BlockSpec((1, ATTN_BLOCK, GROUP_WIDTH),
                            lambda b, s: (b, jnp.maximum(s * blocks_per_step - 1, 0), 0))
        o_g, lse_g = pl.pallas_call(
            functools.partial(_attn_kernel, blocks_per_residue=L // ATTN_BLOCK, n_back=window // r),
            name=f"dilated_attn_g{g}",
            grid=(B, S // rows),
            in_specs=[cur, cur, cur, prev, prev],
            out_specs=[cur, cur],
            out_shape=[jax.ShapeDtypeStruct((B, S, GROUP_WIDTH), BF16),
                       jax.ShapeDtypeStruct((B, S, GROUP_WIDTH), F32)],
            compiler_params=pltpu.CompilerParams(dimension_semantics=("arbitrary", "arbitrary")),
        )(flat(qs[g]), flat(ks[g]), flat(vs[g]), flat(ks[g]), flat(vs[g]))
        attn_o.append(o_g.reshape(B, r, L, GROUP_WIDTH))
        attn_lse.append(lse_g.reshape(B, r, L, GROUP_WIDTH))

    o_specs = [residue_major(GROUP_WIDTH, r) for _, r in DILATED_GROUPS]
    return pl.pallas_call(
        functools.partial(_merge_ffn2_kernel, tiles_per_seq=tiles_per_seq),
        name="merge_ffn2",
        grid=(n_tiles,),
        in_specs=[tok(D)] + o_specs + o_specs
        + [tok(POOL_WIDTH),
           pl.BlockSpec((POOL_HALO, POOL_WIDTH), lambda i: (jnp.maximum(i * (tm // POOL_HALO) - 1, 0), 0)),
           tok(gate_width),
           _resident(w_branch_attn.shape), _resident(pool_w.shape), _resident((1, POOL_WIDTH)),
           _resident(w_branch_pool.shape), _resident(w_out.shape), _resident((1, D)),
           _resident((D, d_ff)), _resident((D, d_ff)), _resident((d_ff, D)), _resident((1, D))],
        out_specs=tok(D),
        out_shape=jax.ShapeDtypeStruct((T, D), F32),
        scratch_shapes=[pltpu.VMEM((tm, D), F32),
                        pltpu.VMEM((2 * (N_GROUPS - 1) * GROUP_WIDTH // LANES, tm, LANES), F32),
                        pltpu.VMEM((POOL_HALO + tm, POOL_WIDTH), F32)],
        compiler_params=params,
    )(h1, *attn_o, *attn_lse, pz, pz, gates, bf(w_branch_attn), bf(pool_w), row(pool_scale),
      bf(w_branch_pool), bf(w_out), row(ffn2_norm), bf(ffn2_w1), bf(ffn2_w3), bf(ffn2_w2), row(final_gain))


def kernel(x, ffn1_norm, ffn1_w1, ffn1_w3, ffn1_w2, mix_norm, w_in, w_branch_attn, w_branch_pool, pool_w,
           pool_scale, w_out, ffn2_norm, ffn2_w1, ffn2_w3, ffn2_w2, final_norm):
    B, S, D = x.shape
    depth = ffn1_norm.shape[0]
    assert S % TOKEN_TILE == 0 and S % ATTN_ROWS == 0
    assert all(window // r == ATTN_BLOCK and (S // r) % ATTN_BLOCK == 0 for window, r in DILATED_GROUPS)
    assert all(TOKEN_TILE % (16 * r) == 0 for _, r in DILATED_GROUPS)
    cos_t, sin_t = _rope_tables(S)
    h = x.reshape(B * S, D)
    for l in range(depth):
        assert l == depth - 1, "only the last layer's output norm is implemented"
        h = _layer(h, B, S, ffn1_norm[l], ffn1_w1[l], ffn1_w3[l], ffn1_w2[l], mix_norm[l], w_in[l],
                   w_branch_attn[l], w_branch_pool[l], pool_w[l], pool_scale[l], w_out[l], ffn2_norm[l],
                   ffn2_w1[l], ffn2_w3[l], ffn2_w2[l], final_norm, cos_t, sin_t)
    return h.reshape(B, S, D)
```

```python
import functools

import jax
import jax.numpy as jnp
from jax import lax
from jax.experimental import pallas as pl
from jax.experimental.pallas import tpu as pltpu

F32 = jnp.float32
BF16 = jnp.bfloat16

HEAD_DIM = 64
HEADS_PER_GROUP = 4
GROUP_WIDTH = HEADS_PER_GROUP * HEAD_DIM
DILATED_GROUPS = ((128, 1), (512, 4), (2048, 16))
N_GROUPS = len(DILATED_GROUPS)
ATTN_BLOCK = 128
ROT_DIM = HEAD_DIM // 4
ROPE_THETA = 500000.0
POOL_WINDOWS = (2, 4, 8, 16)
POOL_GROUP_WIDTH = 128
POOL_WIDTH = len(POOL_WINDOWS) * POOL_GROUP_WIDTH
POOL_HALO = max(POOL_WINDOWS)
RMS_EPS = 1e-6

LANES = 128
V7X_VMEM_BYTES = 64 * 1024 * 1024

TOKEN_TILE = 512
FF_CHUNK = 256
ATTN_ROWS = 1024
VMEM_LIMIT = 56 * 1024 * 1024


def _rms(x, gain):
    return x * lax.rsqrt(jnp.mean(x * x, axis=-1, keepdims=True) + RMS_EPS) * gain


def _swiglu(u, w1_ref, w3_ref, w2_ref, hid_ref):
    d_ff = w1_ref.shape[1]
    for f in range(d_ff // FF_CHUNK):
        cols = slice(f * FF_CHUNK, (f + 1) * FF_CHUNK)
        a = jnp.dot(u, w1_ref[:, cols], preferred_element_type=F32)
        b = jnp.dot(u, w3_ref[:, cols], preferred_element_type=F32)
        hid_ref[:, cols] = (a * jax.nn.sigmoid(a) * b).astype(BF16)
    return jnp.dot(hid_ref[...], w2_ref[...], preferred_element_type=F32)


def _rope(x, cos, sin):
    j = lax.broadcasted_iota(jnp.int32, x.shape, 1) % HEAD_DIM
    half = ROT_DIM // 2
    partner = jnp.where(j < half, pltpu.roll(x, LANES - half, axis=1), pltpu.roll(x, half, axis=1))
    return jnp.where(j < ROT_DIM, x * cos + partner * sin, x)


def _pool_deltas(pext_ref, first_pos):
    tm = pext_ref.shape[0] - POOL_HALO
    pos = first_pos + lax.broadcasted_iota(jnp.int32, (tm, 1), 0)
    deltas = []
    for gi, win in enumerate(POOL_WINDOWS):
        lanes = slice(gi * POOL_GROUP_WIDTH, (gi + 1) * POOL_GROUP_WIDTH)
        own = pext_ref[POOL_HALO:POOL_HALO + tm, lanes]
        total = own
        for back in range(1, win):
            total = total + pext_ref[POOL_HALO - back:POOL_HALO - back + tm, lanes]
        count = jnp.minimum(pos + 1, win).astype(F32)
        deltas.append((total / count - own).astype(BF16))
    return deltas


def _pool_project(deltas, poolw_ref, pscale_ref, wb_ref):
    pooled = []
    for gi, d in enumerate(deltas):
        lanes = slice(gi * POOL_GROUP_WIDTH, (gi + 1) * POOL_GROUP_WIDTH)
        pooled.append(jnp.dot(d, poolw_ref[gi], preferred_element_type=F32) * pscale_ref[:, lanes])
    return jnp.dot(jnp.concatenate(pooled, axis=1).astype(BF16), wb_ref[...], preferred_element_type=F32)


def _ffn1_proj_kernel(x_ref, g1_ref, w1_ref, w3_ref, w2_ref, gm_ref, win_ref, cos_ref, sin_ref,
                      poolw_ref, pscale_ref, wb_ref,
                      h1_ref, q0_ref, q1_ref, q2_ref, k0_ref, k1_ref, k2_ref, v0_ref, v1_ref, v2_ref,
                      ga_ref, gpool_ref, hid_ref, slab_ref, pext_ref, *, tiles_per_seq):
    tm = x_ref.shape[0]
    d_model = x_ref.shape[1]
    t = pl.program_id(0) % tiles_per_seq

    @pl.when(t == 0)
    def _():
        pext_ref[0:POOL_HALO, :] = jnp.zeros((POOL_HALO, POOL_WIDTH), F32)

    x = x_ref[...]
    u = _rms(x, g1_ref[...]).astype(BF16)
    h1 = x + 0.5 * _swiglu(u, w1_ref, w3_ref, w2_ref, hid_ref)
    h1_ref[...] = h1
    um = _rms(h1, gm_ref[...]).astype(BF16)
    cos = cos_ref[...]
    sin = sin_ref[...]

    def project(col):
        return jnp.dot(um, win_ref[:, col:col + GROUP_WIDTH], preferred_element_type=F32)

    def rope(p):
        return jnp.concatenate([_rope(p[:, :LANES], cos, sin), _rope(p[:, LANES:], cos, sin)], axis=1)

    qkv_width = 3 * N_GROUPS * GROUP_WIDTH
    gate_col = qkv_width + POOL_WIDTH

    def qkv_slab(kind, g, ref, scratch_slot):
        r = DILATED_GROUPS[g][1]
        p = project(("q", "k", "v").index(kind) * N_GROUPS * GROUP_WIDTH + g * GROUP_WIDTH)
        if kind == "q":
            p = rope(p) * (HEAD_DIM ** -0.5)
        elif kind == "k":
            p = rope(p)
        if r == 1:
            ref[0, 0] = p.astype(BF16)
            return
        for half in range(GROUP_WIDTH // LANES):
            lanes = slice(half * LANES, (half + 1) * LANES)
            slab_ref[scratch_slot + half] = p[:, lanes]
            for c in range(r):
                ref[0, c, :, lanes] = slab_ref[scratch_slot + half, pl.ds(c, tm // r, stride=r), :].astype(BF16)

    for s in range(POOL_WIDTH // GROUP_WIDTH):
        lanes = slice(s * GROUP_WIDTH, (s + 1) * GROUP_WIDTH)
        pext_ref[POOL_HALO:, lanes] = project(qkv_width + s * GROUP_WIDTH)
    deltas = _pool_deltas(pext_ref, t * tm)
    pext_ref[0:POOL_HALO, :] = pext_ref[tm:tm + POOL_HALO, :]
    for s in range(d_model // GROUP_WIDTH):
        lanes = slice(s * GROUP_WIDTH, (s + 1) * GROUP_WIDTH)
        ga_ref[:, lanes] = jax.nn.sigmoid(project(gate_col + s * GROUP_WIDTH)).astype(BF16)
    halves = GROUP_WIDTH // LANES
    slot = 0
    for g, ref in enumerate((q0_ref, q1_ref, q2_ref)):
        qkv_slab("q", g, ref, slot)
        slot += halves if DILATED_GROUPS[g][1] != 1 else 0
    y_pool = _pool_project(deltas, poolw_ref, pscale_ref, wb_ref)
    for s in range(d_model // GROUP_WIDTH):
        lanes = slice(s * GROUP_WIDTH, (s + 1) * GROUP_WIDTH)
        gate_pool = jax.nn.sigmoid(project(gate_col + d_model + s * GROUP_WIDTH))
        gpool_ref[:, lanes] = (gate_pool * y_pool[:, lanes]).astype(BF16)
    for kind, refs in (("k", (k0_ref, k1_ref, k2_ref)), ("v", (v0_ref, v1_ref, v2_ref))):
        for g, ref in enumerate(refs):
            qkv_slab(kind, g, ref, slot)
            slot += halves if DILATED_GROUPS[g][1] != 1 else 0


def _attn_kernel(q_ref, k_ref, v_ref, kh_ref, vh_ref, o_ref, lse_ref, *, blocks_per_residue, n_back):
    rows = q_ref.shape[1]
    nblk = rows // ATTN_BLOCK
    step = pl.program_id(1)
    stacked = HEADS_PER_GROUP * ATTN_BLOCK
    qi = lax.broadcasted_iota(jnp.int32, (stacked, 2 * ATTN_BLOCK), 0) % ATTN_BLOCK
    kj = lax.broadcasted_iota(jnp.int32, (stacked, 2 * ATTN_BLOCK), 1)
    dist = qi + ATTN_BLOCK - kj
    band = (dist >= 0) & (dist <= n_back)
    in_current = kj >= ATTN_BLOCK
    head_of_lane = lax.broadcasted_iota(jnp.int32, (1, GROUP_WIDTH), 1) // HEAD_DIM
    heads = range(HEADS_PER_GROUP)

    def by_head(rows_of_head):
        out = rows_of_head[0:ATTN_BLOCK]
        for h in heads[1:]:
            out = jnp.where(head_of_lane == h, rows_of_head[h * ATTN_BLOCK:(h + 1) * ATTN_BLOCK], out)
        return out

    for j in range(nblk):
        has_prev = ((step * nblk + j) % blocks_per_residue) != 0
        allowed = band & (in_current | has_prev)
        cur = slice(j * ATTN_BLOCK, (j + 1) * ATTN_BLOCK)
        qb = q_ref[0, cur, :]
        if j == 0:
            kk = jnp.concatenate([kh_ref[0], k_ref[0, cur, :]], axis=0)
            vv = jnp.concatenate([vh_ref[0], v_ref[0, cur, :]], axis=0)
        else:
            both = slice((j - 1) * ATTN_BLOCK, (j + 1) * ATTN_BLOCK)
            kk = k_ref[0, both, :]
            vv = v_ref[0, both, :]
        qs = jnp.concatenate([jnp.where(head_of_lane == h, qb, jnp.zeros_like(qb)) for h in heads], axis=0)
        s = lax.dot_general(qs, kk, (((1,), (1,)), ((), ())), preferred_element_type=F32)
        s = jnp.where(allowed, s, -jnp.inf)
        m = jnp.max(s, axis=-1, keepdims=True)
        p = jnp.exp(s - m)
        l = jnp.sum(p, axis=-1, keepdims=True)
        pv = jnp.dot(p.astype(BF16), vv, preferred_element_type=F32)
        o_ref[0, cur, :] = by_head(pv * (1.0 / l)).astype(BF16)
        lse_ref[0, cur, :] = by_head(jnp.broadcast_to(m + jnp.log(l), pv.shape))


def _merge_ffn2_kernel(h1_ref, o0_ref, o1_ref, o2_ref, l0_ref, l1_ref, l2_ref, ga_ref, gpool_ref,
                       wa_ref, wo_ref, g2_ref, w1_ref, w3_ref, w2_ref, gf_ref,
                       out_ref, hid_ref, nat_ref):
    tm = h1_ref.shape[0]

    def natural_order(ref, r, slot):
        if r == 1:
            return ref[0, 0].astype(F32)
        halves = []
        for half in range(GROUP_WIDTH // LANES):
            lanes = slice(half * LANES, (half + 1) * LANES)
            for c in range(r):
                nat_ref[slot + half, pl.ds(c, tm // r, stride=r), :] = ref[0, c, :, lanes].astype(F32)
            halves.append(nat_ref[slot + half])
        return jnp.concatenate(halves, axis=1)

    outs, lses = [], []
    slot = 0
    halves_per_slab = GROUP_WIDTH // LANES
    for (_, r), o_ref, l_ref in zip(DILATED_GROUPS, (o0_ref, o1_ref, o2_ref), (l0_ref, l1_ref, l2_ref)):
        outs.append(natural_order(o_ref, r, slot))
        lses.append(natural_order(l_ref, r, slot + halves_per_slab))
        if r != 1:
            slot += 2 * halves_per_slab
    top = functools.reduce(jnp.maximum, lses)
    weights = [jnp.exp(l - top) for l in lses]
    o = sum(w * og for w, og in zip(weights, outs)) / sum(weights)
    y_attn = jnp.dot(o.astype(BF16), wa_ref[...], preferred_element_type=F32)

    merged = ga_ref[...].astype(F32) * y_attn + gpool_ref[...].astype(F32)
    h2 = h1_ref[...] + jnp.dot(merged.astype(BF16), wo_ref[...], preferred_element_type=F32)
    u2 = _rms(h2, g2_ref[...]).astype(BF16)
    h3 = h2 + 0.5 * _swiglu(u2, w1_ref, w3_ref, w2_ref, hid_ref)
    out_ref[...] = _rms(h3, gf_ref[...])


def _resident(shape):
    return pl.BlockSpec(shape, lambda *_: (0,) * len(shape), pipeline_mode=pl.Buffered(1))


def _rope_tables(seq):
    pos = jnp.arange(seq, dtype=F32)
    inv = ROPE_THETA ** (-jnp.arange(0, ROT_DIM, 2, dtype=F32) / ROT_DIM)
    ang = pos[:, None] * inv[None, :]
    cos, sin = jnp.cos(ang), jnp.sin(ang)
    rest = HEAD_DIM - ROT_DIM
    cos_h = jnp.concatenate([cos, cos, jnp.ones((seq, rest), F32)], axis=1)
    sin_h = jnp.concatenate([-sin, sin, jnp.zeros((seq, rest), F32)], axis=1)
    reps = LANES // HEAD_DIM
    return jnp.tile(cos_h, (1, reps)), jnp.tile(sin_h, (1, reps))


def _layer(h, B, S, ffn1_norm, ffn1_w1, ffn1_w3, ffn1_w2, mix_norm, w_in, w_branch_attn, w_branch_pool,
           pool_w, pool_scale, w_out, ffn2_norm, ffn2_w1, ffn2_w3, ffn2_w2, final_gain, cos_t, sin_t):
    T, D = h.shape
    d_ff = ffn1_w1.shape[1]
    in_width = w_in.shape[1]
    assert in_width == 3 * N_GROUPS * GROUP_WIDTH + POOL_WIDTH + 2 * D
    tm = TOKEN_TILE
    tiles_per_seq = S // tm
    n_tiles = T // tm
    bf = lambda w: w.astype(BF16)
    row = lambda g: g.reshape(1, -1).astype(F32)
    params = pltpu.CompilerParams(dimension_semantics=("arbitrary",), vmem_limit_bytes=VMEM_LIMIT)

    tok = lambda width: pl.BlockSpec((tm, width), lambda i: (i, 0))

    def residue_major(width, r):
        return pl.BlockSpec((1, r, tm // r, width), lambda i: (i // tiles_per_seq, 0, i % tiles_per_seq, 0))

    qkv_shapes = [jax.ShapeDtypeStruct((B, r, S // r, GROUP_WIDTH), BF16) for _, r in DILATED_GROUPS]
    qkv_specs = [residue_major(GROUP_WIDTH, r) for _, r in DILATED_GROUPS]
    outs = pl.pallas_call(
        functools.partial(_ffn1_proj_kernel, tiles_per_seq=tiles_per_seq),
        name="ffn1_proj",
        grid=(n_tiles,),
        in_specs=[tok(D), _resident((1, D)), _resident((D, d_ff)), _resident((D, d_ff)), _resident((d_ff, D)),
                  _resident((1, D)), _resident((D, in_width)),
                  pl.BlockSpec((tm, LANES), lambda i: (i % tiles_per_seq, 0)),
                  pl.BlockSpec((tm, LANES), lambda i: (i % tiles_per_seq, 0)),
                  _resident(pool_w.shape), _resident((1, POOL_WIDTH)), _resident(w_branch_pool.shape)],
        out_specs=[tok(D)] + qkv_specs * 3 + [tok(D), tok(D)],
        out_shape=[jax.ShapeDtypeStruct((T, D), F32)] + qkv_shapes * 3
        + [jax.ShapeDtypeStruct((T, D), BF16), jax.ShapeDtypeStruct((T, D), BF16)],
        scratch_shapes=[pltpu.VMEM((tm, d_ff), BF16),
                        pltpu.VMEM((3 * (N_GROUPS - 1) * GROUP_WIDTH // LANES, tm, LANES), F32),
                        pltpu.VMEM((POOL_HALO + tm, POOL_WIDTH), F32)],
        compiler_params=params,
    )(h, row(ffn1_norm), bf(ffn1_w1), bf(ffn1_w3), bf(ffn1_w2), row(mix_norm), bf(w_in), cos_t, sin_t,
      bf(pool_w), row(pool_scale), bf(w_branch_pool))
    h1 = outs[0]
    qs, ks, vs = outs[1:4], outs[4:7], outs[7:10]
    gate_attn, gated_pool = outs[10], outs[11]

    attn_o, attn_lse = [], []
    rows = ATTN_ROWS
    blocks_per_step = rows // ATTN_BLOCK
    for g, (window, r) in enumerate(DILATED_GROUPS):
        L = S // r
        flat = lambda a: a.reshape(B, S, GROUP_WIDTH)
        cur = pl.BlockSpec((1, rows, GROUP_WIDTH), lambda b, s: (b, s, 0))
        prev = pl.BlockSpec((1, ATTN_BLOCK, GROUP_WIDTH),
                            lambda b, s: (b, jnp.maximum(s * blocks_per_step - 1, 0), 0))
        o_g, lse_g = pl.pallas_call(
            functools.partial(_attn_kernel, blocks_per_residue=L // ATTN_BLOCK, n_back=window // r),
            name=f"dilated_attn_g{g}",
            grid=(B, S // rows),
            in_specs=[cur, cur, cur, prev, prev],
            out_specs=[cur, cur],
            out_shape=[jax.ShapeDtypeStruct((B, S, GROUP_WIDTH), BF16),
                       jax.ShapeDtypeStruct((B, S, GROUP_WIDTH), F32)],
            compiler_params=pltpu.CompilerParams(dimension_semantics=("arbitrary", "arbitrary")),
        )(flat(qs[g]), flat(ks[g]), flat(vs[g]), flat(ks[g]), flat(vs[g]))
        attn_o.append(o_g.reshape(B, r, L, GROUP_WIDTH))
        attn_lse.append(lse_g.reshape(B, r, L, GROUP_WIDTH))

    o_specs = [residue_major(GROUP_WIDTH, r) for _, r in DILATED_GROUPS]
    return pl.pallas_call(
        _merge_ffn2_kernel,
        name="merge_ffn2",
        grid=(n_tiles,),
        in_specs=[tok(D)] + o_specs + o_specs + [tok(D), tok(D)]
        + [_resident(w_branch_attn.shape), _resident(w_out.shape), _resident((1, D)),
           _resident((D, d_ff)), _resident((D, d_ff)), _resident((d_ff, D)), _resident((1, D))],
        out_specs=tok(D),
        out_shape=jax.ShapeDtypeStruct((T, D), F32),
        scratch_shapes=[pltpu.VMEM((tm, d_ff), BF16),
                        pltpu.VMEM((2 * (N_GROUPS - 1) * GROUP_WIDTH // LANES, tm, LANES), F32)],
        compiler_params=params,
    )(h1, *attn_o, *attn_lse, gate_attn, gated_pool, bf(w_branch_attn), bf(w_out), row(ffn2_norm),
      bf(ffn2_w1), bf(ffn2_w3), bf(ffn2_w2), row(final_gain))


def kernel(x, ffn1_norm, ffn1_w1, ffn1_w3, ffn1_w2, mix_norm, w_in, w_branch_attn, w_branch_pool, pool_w,
           pool_scale, w_out, ffn2_norm, ffn2_w1, ffn2_w3, ffn2_w2, final_norm):
    B, S, D = x.shape
    depth = ffn1_norm.shape[0]
    assert S % TOKEN_TILE == 0 and S % ATTN_ROWS == 0
    assert all(window // r == ATTN_BLOCK and (S // r) % ATTN_BLOCK == 0 for window, r in DILATED_GROUPS)
    assert all(TOKEN_TILE % (16 * r) == 0 for _, r in DILATED_GROUPS)
    cos_t, sin_t = _rope_tables(S)
    h = x.reshape(B * S, D)
    for l in range(depth):
        assert l == depth - 1, "only the last layer's output norm is implemented"
        h = _layer(h, B, S, ffn1_norm[l], ffn1_w1[l], ffn1_w3[l], ffn1_w2[l], mix_norm[l], w_in[l],
                   w_branch_attn[l], w_branch_pool[l], pool_w[l], pool_scale[l], w_out[l], ffn2_norm[l],
                   ffn2_w1[l], ffn2_w3[l], ffn2_w2[l], final_norm, cos_t, sin_t)
    return h.reshape(B, S, D)
```

```python
import functools

import jax
import jax.numpy as jnp
from jax import lax
from jax.experimental import pallas as pl
from jax.experimental.pallas import tpu as pltpu

F32 = jnp.float32
BF16 = jnp.bfloat16

HEAD_DIM = 64
HEADS_PER_GROUP = 4
GROUP_WIDTH = HEADS_PER_GROUP * HEAD_DIM
DILATED_GROUPS = ((128, 1), (512, 4), (2048, 16))
N_GROUPS = len(DILATED_GROUPS)
ATTN_BLOCK = 128
ROT_DIM = HEAD_DIM // 4
ROPE_THETA = 500000.0
POOL_WINDOWS = (2, 4, 8, 16)
POOL_GROUP_WIDTH = 128
POOL_WIDTH = len(POOL_WINDOWS) * POOL_GROUP_WIDTH
POOL_HALO = max(POOL_WINDOWS)
RMS_EPS = 1e-6

LANES = 128
V7X_VMEM_BYTES = 64 * 1024 * 1024

TOKEN_TILE = 512
FF_CHUNK = 256
ATTN_ROWS = 1024
VMEM_LIMIT = 56 * 1024 * 1024


def _rms(x, gain):
    return x * lax.rsqrt(jnp.mean(x * x, axis=-1, keepdims=True) + RMS_EPS) * gain


def _swiglu(u, w1_ref, w3_ref, w2_ref, hid_ref):
    d_ff = w1_ref.shape[1]
    for f in range(d_ff // FF_CHUNK):
        cols = slice(f * FF_CHUNK, (f + 1) * FF_CHUNK)
        a = jnp.dot(u, w1_ref[:, cols], preferred_element_type=F32)
        b = jnp.dot(u, w3_ref[:, cols], preferred_element_type=F32)
        hid_ref[:, cols] = (a * jax.nn.sigmoid(a) * b).astype(BF16)
    return jnp.dot(hid_ref[...], w2_ref[...], preferred_element_type=F32)


def _rope(x, cos, sin):
    j = lax.broadcasted_iota(jnp.int32, x.shape, 1) % HEAD_DIM
    half = ROT_DIM // 2
    partner = jnp.where(j < half, pltpu.roll(x, LANES - half, axis=1), pltpu.roll(x, half, axis=1))
    return jnp.where(j < ROT_DIM, x * cos + partner * sin, x)


def _pool_deltas(pext_ref, first_pos):
    tm = pext_ref.shape[0] - POOL_HALO
    pos = first_pos + lax.broadcasted_iota(jnp.int32, (tm, 1), 0)
    deltas = []
    for gi, win in enumerate(POOL_WINDOWS):
        lanes = slice(gi * POOL_GROUP_WIDTH, (gi + 1) * POOL_GROUP_WIDTH)
        own = pext_ref[POOL_HALO:POOL_HALO + tm, lanes]
        total = own
        for back in range(1, win):
            total = total + pext_ref[POOL_HALO - back:POOL_HALO - back + tm, lanes]
        count = jnp.minimum(pos + 1, win).astype(F32)
        deltas.append((total / count - own).astype(BF16))
    return deltas


def _pool_project(deltas, poolw_ref, pscale_ref, wb_ref):
    pooled = []
    for gi, d in enumerate(deltas):
        lanes = slice(gi * POOL_GROUP_WIDTH, (gi + 1) * POOL_GROUP_WIDTH)
        pooled.append(jnp.dot(d, poolw_ref[gi], preferred_element_type=F32) * pscale_ref[:, lanes])
    return jnp.dot(jnp.concatenate(pooled, axis=1).astype(BF16), wb_ref[...], preferred_element_type=F32)


def _ffn1_proj_kernel(x_ref, g1_ref, w1_ref, w3_ref, w2_ref, gm_ref, win_ref, cos_ref, sin_ref,
                      poolw_ref, pscale_ref, wb_ref,
                      h1_ref, q0_ref, q1_ref, q2_ref, k0_ref, k1_ref, k2_ref, v0_ref, v1_ref, v2_ref,
                      ga_ref, gpool_ref, hid_ref, slab_ref, pext_ref, *, tiles_per_seq):
    tm = x_ref.shape[0]
    d_model = x_ref.shape[1]
    t = pl.program_id(0) % tiles_per_seq

    @pl.when(t == 0)
    def _():
        pext_ref[0:POOL_HALO, :] = jnp.zeros((POOL_HALO, POOL_WIDTH), F32)

    x = x_ref[...]
    u = _rms(x, g1_ref[...]).astype(BF16)
    h1 = x + 0.5 * _swiglu(u, w1_ref, w3_ref, w2_ref, hid_ref)
    h1_ref[...] = h1
    um = _rms(h1, gm_ref[...]).astype(BF16)
    cos = cos_ref[...]
    sin = sin_ref[...]

    def project(col):
        return jnp.dot(um, win_ref[:, col:col + GROUP_WIDTH], preferred_element_type=F32)

    def rope(p):
        return jnp.concatenate([_rope(p[:, :LANES], cos, sin), _rope(p[:, LANES:], cos, sin)], axis=1)

    qkv_width = 3 * N_GROUPS * GROUP_WIDTH
    gate_col = qkv_width + POOL_WIDTH

    def qkv_slab(kind, g, ref, scratch_slot):
        r = DILATED_GROUPS[g][1]
        p = project(("q", "k", "v").index(kind) * N_GROUPS * GROUP_WIDTH + g * GROUP_WIDTH)
        if kind == "q":
            p = rope(p) * (HEAD_DIM ** -0.5)
        elif kind == "k":
            p = rope(p)
        if r == 1:
            ref[0, 0] = p.astype(BF16)
            return
        for half in range(GROUP_WIDTH // LANES):
            lanes = slice(half * LANES, (half + 1) * LANES)
            slab_ref[scratch_slot + half] = p[:, lanes]
            for c in range(r):
                ref[0, c, :, lanes] = slab_ref[scratch_slot + half, pl.ds(c, tm // r, stride=r), :].astype(BF16)

    for s in range(POOL_WIDTH // GROUP_WIDTH):
        lanes = slice(s * GROUP_WIDTH, (s + 1) * GROUP_WIDTH)
        pext_ref[POOL_HALO:, lanes] = project(qkv_width + s * GROUP_WIDTH)
    deltas = _pool_deltas(pext_ref, t * tm)
    pext_ref[0:POOL_HALO, :] = pext_ref[tm:tm + POOL_HALO, :]
    for s in range(d_model // GROUP_WIDTH):
        lanes = slice(s * GROUP_WIDTH, (s + 1) * GROUP_WIDTH)
        ga_ref[:, lanes] = jax.nn.sigmoid(project(gate_col + s * GROUP_WIDTH)).astype(BF16)
    halves = GROUP_WIDTH // LANES
    slot = 0
    for g, ref in enumerate((q0_ref, q1_ref, q2_ref)):
        qkv_slab("q", g, ref, slot)
        slot += halves if DILATED_GROUPS[g][1] != 1 else 0
    y_pool = _pool_project(deltas, poolw_ref, pscale_ref, wb_ref)
    for s in range(d_model // GROUP_WIDTH):
        lanes = slice(s * GROUP_WIDTH, (s + 1) * GROUP_WIDTH)
        gate_pool = jax.nn.sigmoid(project(gate_col + d_model + s * GROUP_WIDTH))
        gpool_ref[:, lanes] = (gate_pool * y_pool[:, lanes]).astype(BF16)
    for kind, refs in (("k", (k0_ref, k1_ref, k2_ref)), ("v", (v0_ref, v1_ref, v2_ref))):
        for g, ref in enumerate(refs):
            qkv_slab(kind, g, ref, slot)
            slot += halves if DILATED_GROUPS[g][1] != 1 else 0


def _attn_kernel(q_ref, k_ref, v_ref, kh_ref, vh_ref, o_ref, lse_ref, s_ref, p_ref, *,
                 blocks_per_residue, n_back):
    rows = q_ref.shape[1]
    nblk = rows // ATTN_BLOCK
    step = pl.program_id(1)
    qi = lax.broadcasted_iota(jnp.int32, (ATTN_BLOCK, 2 * ATTN_BLOCK), 0)
    kj = lax.broadcasted_iota(jnp.int32, (ATTN_BLOCK, 2 * ATTN_BLOCK), 1)
    dist = qi + ATTN_BLOCK - kj
    band = (dist >= 0) & (dist <= n_back)
    in_current = kj >= ATTN_BLOCK
    head_of_lane = lax.broadcasted_iota(jnp.int32, (1, GROUP_WIDTH), 1) // HEAD_DIM
    heads = range(HEADS_PER_GROUP)
    heads_per_vreg = LANES // HEAD_DIM

    def head_rows(h):
        return slice(h * ATTN_BLOCK, (h + 1) * ATTN_BLOCK)

    def keys_values(j, ref, halo_ref):
        if j == 0:
            return jnp.concatenate([halo_ref[0], ref[0, 0:ATTN_BLOCK, :]], axis=0)
        return ref[0, (j - 1) * ATTN_BLOCK:(j + 1) * ATTN_BLOCK, :]

    def scores(j):
        qb = q_ref[0, j * ATTN_BLOCK:(j + 1) * ATTN_BLOCK, :]
        qs = jnp.concatenate([jnp.where(head_of_lane == h, qb, jnp.zeros_like(qb)) for h in heads], axis=0)
        return lax.dot_general(qs, keys_values(j, k_ref, kh_ref), (((1,), (1,)), ((), ())),
                               preferred_element_type=F32)

    def lanes_by_head(per_head):
        first_in_vreg = (head_of_lane[:, :LANES] % heads_per_vreg) == 0
        vregs = [jnp.where(first_in_vreg, per_head[v * heads_per_vreg], per_head[v * heads_per_vreg + 1])
                 for v in range(GROUP_WIDTH // LANES)]
        return jnp.concatenate(vregs, axis=1)

    s_ref[0] = scores(0)
    for j in range(nblk):
        if j + 1 < nblk:
            s_ref[(j + 1) % 2] = scores(j + 1)
        has_prev = ((step * nblk + j) % blocks_per_residue) != 0
        allowed = band & (in_current | has_prev)
        inv_l, lse = [], []
        for h in heads:
            s = jnp.where(allowed, s_ref[j % 2, head_rows(h), :], -jnp.inf)
            m = jnp.max(s, axis=-1, keepdims=True)
            p = jnp.exp(s - m)
            l = jnp.sum(p, axis=-1, keepdims=True)
            p_ref[head_rows(h), :] = p.astype(BF16)
            inv_l.append(1.0 / l)
            lse.append(m + jnp.log(l))
        pv = jnp.dot(p_ref[...], keys_values(j, v_ref, vh_ref), preferred_element_type=F32)
        o_heads = [pv[head_rows(h), (h // heads_per_vreg) * LANES:(h // heads_per_vreg + 1) * LANES] * inv_l[h]
                   for h in heads]
        cur = slice(j * ATTN_BLOCK, (j + 1) * ATTN_BLOCK)
        o_ref[0, cur, :] = lanes_by_head(o_heads).astype(BF16)
        lse_ref[0, cur, :] = lanes_by_head([jnp.broadcast_to(x, (ATTN_BLOCK, LANES)) for x in lse])


def _merge_ffn2_kernel(h1_ref, o0_ref, o1_ref, o2_ref, l0_ref, l1_ref, l2_ref, ga_ref, gpool_ref,
                       wa_ref, wo_ref, g2_ref, w1_ref, w3_ref, w2_ref, gf_ref,
                       out_ref, hid_ref, nat_ref):
    tm = h1_ref.shape[0]

    def natural_order(ref, r, slot):
        if r == 1:
            return ref[0, 0].astype(F32)
        halves = []
        for half in range(GROUP_WIDTH // LANES):
            lanes = slice(half * LANES, (half + 1) * LANES)
            for c in range(r):
                nat_ref[slot + half, pl.ds(c, tm // r, stride=r), :] = ref[0, c, :, lanes].astype(F32)
            halves.append(nat_ref[slot + half])
        return jnp.concatenate(halves, axis=1)

    outs, lses = [], []
    slot = 0
    halves_per_slab = GROUP_WIDTH // LANES
    for (_, r), o_ref, l_ref in zip(DILATED_GROUPS, (o0_ref, o1_ref, o2_ref), (l0_ref, l1_ref, l2_ref)):
        outs.append(natural_order(o_ref, r, slot))
        lses.append(natural_order(l_ref, r, slot + halves_per_slab))
        if r != 1:
            slot += 2 * halves_per_slab
    top = functools.reduce(jnp.maximum, lses)
    weights = [jnp.exp(l - top) for l in lses]
    o = sum(w * og for w, og in zip(weights, outs)) / sum(weights)
    y_attn = jnp.dot(o.astype(BF16), wa_ref[...], preferred_element_type=F32)

    merged = ga_ref[...].astype(F32) * y_attn + gpool_ref[...].astype(F32)
    h2 = h1_ref[...] + jnp.dot(merged.astype(BF16), wo_ref[...], preferred_element_type=F32)
    u2 = _rms(h2, g2_ref[...]).astype(BF16)
    h3 = h2 + 0.5 * _swiglu(u2, w1_ref, w3_ref, w2_ref, hid_ref)
    out_ref[...] = _rms(h3, gf_ref[...])


def _resident(shape):
    return pl.BlockSpec(shape, lambda *_: (0,) * len(shape), pipeline_mode=pl.Buffered(1))


def _rope_tables(seq):
    pos = jnp.arange(seq, dtype=F32)
    inv = ROPE_THETA ** (-jnp.arange(0, ROT_DIM, 2, dtype=F32) / ROT_DIM)
    ang = pos[:, None] * inv[None, :]
    cos, sin = jnp.cos(ang), jnp.sin(ang)
    rest = HEAD_DIM - ROT_DIM
    cos_h = jnp.concatenate([cos, cos, jnp.ones((seq, rest), F32)], axis=1)
    sin_h = jnp.concatenate([-sin, sin, jnp.zeros((seq, rest), F32)], axis=1)
    reps = LANES // HEAD_DIM
    return jnp.tile(cos_h, (1, reps)), jnp.tile(sin_h, (1, reps))


def _layer(h, B, S, ffn1_norm, ffn1_w1, ffn1_w3, ffn1_w2, mix_norm, w_in, w_branch_attn, w_branch_pool,
           pool_w, pool_scale, w_out, ffn2_norm, ffn2_w1, ffn2_w3, ffn2_w2, final_gain, cos_t, sin_t):
    T, D = h.shape
    d_ff = ffn1_w1.shape[1]
    in_width = w_in.shape[1]
    assert in_width == 3 * N_GROUPS * GROUP_WIDTH + POOL_WIDTH + 2 * D
    tm = TOKEN_TILE
    tiles_per_seq = S // tm
    n_tiles = T // tm
    bf = lambda w: w.astype(BF16)
    row = lambda g: g.reshape(1, -1).astype(F32)
    params = pltpu.CompilerParams(dimension_semantics=("arbitrary",), vmem_limit_bytes=VMEM_LIMIT)

    tok = lambda width: pl.BlockSpec((tm, width), lambda i: (i, 0))

    def residue_major(width, r):
        return pl.BlockSpec((1, r, tm // r, width), lambda i: (i // tiles_per_seq, 0, i % tiles_per_seq, 0))

    qkv_shapes = [jax.ShapeDtypeStruct((B, r, S // r, GROUP_WIDTH), BF16) for _, r in DILATED_GROUPS]
    qkv_specs = [residue_major(GROUP_WIDTH, r) for _, r in DILATED_GROUPS]
    outs = pl.pallas_call(
        functools.partial(_ffn1_proj_kernel, tiles_per_seq=tiles_per_seq),
        name="ffn1_proj",
        grid=(n_tiles,),
        in_specs=[tok(D), _resident((1, D)), _resident((D, d_ff)), _resident((D, d_ff)), _resident((d_ff, D)),
                  _resident((1, D)), _resident((D, in_width)),
                  pl.BlockSpec((tm, LANES), lambda i: (i % tiles_per_seq, 0)),
                  pl.BlockSpec((tm, LANES), lambda i: (i % tiles_per_seq, 0)),
                  _resident(pool_w.shape), _resident((1, POOL_WIDTH)), _resident(w_branch_pool.shape)],
        out_specs=[tok(D)] + qkv_specs * 3 + [tok(D), tok(D)],
        out_shape=[jax.ShapeDtypeStruct((T, D), F32)] + qkv_shapes * 3
        + [jax.ShapeDtypeStruct((T, D), BF16), jax.ShapeDtypeStruct((T, D), BF16)],
        scratch_shapes=[pltpu.VMEM((tm, d_ff), BF16),
                        pltpu.VMEM((3 * (N_GROUPS - 1) * GROUP_WIDTH // LANES, tm, LANES), F32),
                        pltpu.VMEM((POOL_HALO + tm, POOL_WIDTH), F32)],
        compiler_params=params,
    )(h, row(ffn1_norm), bf(ffn1_w1), bf(ffn1_w3), bf(ffn1_w2), row(mix_norm), bf(w_in), cos_t, sin_t,
      bf(pool_w), row(pool_scale), bf(w_branch_pool))
    h1 = outs[0]
    qs, ks, vs = outs[1:4], outs[4:7], outs[7:10]
    gate_attn, gated_pool = outs[10], outs[11]

    attn_o, attn_lse = [], []
    rows = ATTN_ROWS
    blocks_per_step = rows // ATTN_BLOCK
    for g, (window, r) in enumerate(DILATED_GROUPS):
        L = S // r
        flat = lambda a: a.reshape(B, S, GROUP_WIDTH)
        cur = pl.BlockSpec((1, rows, GROUP_WIDTH), lambda b, s: (b, s, 0))
        prev = pl.BlockSpec((1, ATTN_BLOCK, GROUP_WIDTH),
                            lambda b, s: (b, jnp.maximum(s * blocks_per_step - 1, 0), 0))
        o_g, lse_g = pl.pallas_call(
            functools.partial(_attn_kernel, blocks_per_residue=L // ATTN_BLOCK, n_back=window // r),
            name=f"dilated_attn_g{g}",
            grid=(B, S // rows),
            in_specs=[cur, cur, cur, prev, prev],
            out_specs=[cur, cur],
            out_shape=[jax.ShapeDtypeStruct((B, S, GROUP_WIDTH), BF16),
                       jax.ShapeDtypeStruct((B, S, GROUP_WIDTH), F32)],
            scratch_shapes=[pltpu.VMEM((2, HEADS_PER_GROUP * ATTN_BLOCK, 2 * ATTN_BLOCK), F32),
                            pltpu.VMEM((HEADS_PER_GROUP * ATTN_BLOCK, 2 * ATTN_BLOCK), BF16)],
            compiler_params=pltpu.CompilerParams(dimension_semantics=("arbitrary", "arbitrary")),
        )(flat(qs[g]), flat(ks[g]), flat(vs[g]), flat(ks[g]), flat(vs[g]))
        attn_o.append(o_g.reshape(B, r, L, GROUP_WIDTH))
        attn_lse.append(lse_g.reshape(B, r, L, GROUP_WIDTH))

    o_specs = [residue_major(GROUP_WIDTH, r) for _, r in DILATED_GROUPS]
    return pl.pallas_call(
        _merge_ffn2_kernel,
        name="merge_ffn2",
        grid=(n_tiles,),
        in_specs=[tok(D)] + o_specs + o_specs + [tok(D), tok(D)]
        + [_resident(w_branch_attn.shape), _resident(w_out.shape), _resident((1, D)),
           _resident((D, d_ff)), _resident((D, d_ff)), _resident((d_ff, D)), _resident((1, D))],
        out_specs=tok(D),
        out_shape=jax.ShapeDtypeStruct((T, D), F32),
        scratch_shapes=[pltpu.VMEM((tm, d_ff), BF16),
                        pltpu.VMEM((2 * (N_GROUPS - 1) * GROUP_WIDTH // LANES, tm, LANES), F32)],
        compiler_params=params,
    )(h1, *attn_o, *attn_lse, gate_attn, gated_pool, bf(w_branch_attn), bf(w_out), row(ffn2_norm),
      bf(ffn2_w1), bf(ffn2_w3), bf(ffn2_w2), row(final_gain))


def kernel(x, ffn1_norm, ffn1_w1, ffn1_w3, ffn1_w2, mix_norm, w_in, w_branch_attn, w_branch_pool, pool_w,
           pool_scale, w_out, ffn2_norm, ffn2_w1, ffn2_w3, ffn2_w2, final_norm):
    B, S, D = x.shape
    depth = ffn1_norm.shape[0]
    assert S % TOKEN_TILE == 0 and S % ATTN_ROWS == 0
    assert all(window // r == ATTN_BLOCK and (S // r) % ATTN_BLOCK == 0 for window, r in DILATED_GROUPS)
    assert all(TOKEN_TILE % (16 * r) == 0 for _, r in DILATED_GROUPS)
    cos_t, sin_t = _rope_tables(S)
    h = x.reshape(B * S, D)
    for l in range(depth):
        assert l == depth - 1, "only the last layer's output norm is implemented"
        h = _layer(h, B, S, ffn1_norm[l], ffn1_w1[l], ffn1_w3[l], ffn1_w2[l], mix_norm[l], w_in[l],
                   w_branch_attn[l], w_branch_pool[l], pool_w[l], pool_scale[l], w_out[l], ffn2_norm[l],
                   ffn2_w1[l], ffn2_w3[l], ffn2_w2[l], final_norm, cos_t, sin_t)
    return h.reshape(B, S, D)
```

```python
import functools

import jax
import jax.numpy as jnp
from jax import lax
from jax.experimental import pallas as pl
from jax.experimental.pallas import tpu as pltpu

F32 = jnp.float32
BF16 = jnp.bfloat16

HEAD_DIM = 64
HEADS_PER_GROUP = 4
GROUP_WIDTH = HEADS_PER_GROUP * HEAD_DIM
DILATED_GROUPS = ((128, 1), (512, 4), (2048, 16))
N_GROUPS = len(DILATED_GROUPS)
ATTN_BLOCK = 128
ROT_DIM = HEAD_DIM // 4
ROPE_THETA = 500000.0
POOL_WINDOWS = (2, 4, 8, 16)
POOL_GROUP_WIDTH = 128
POOL_WIDTH = len(POOL_WINDOWS) * POOL_GROUP_WIDTH
POOL_HALO = max(POOL_WINDOWS)
RMS_EPS = 1e-6
SCORE_SCALE_LOG2 = HEAD_DIM ** -0.5 * 1.4426950408889634

LANES = 128
V7X_VMEM_BYTES = 64 * 1024 * 1024

TOKEN_TILE = 512
FF_CHUNK = 256
ATTN_ROWS = 1024
VMEM_LIMIT = 56 * 1024 * 1024


def _rms(x, gain):
    return x * lax.rsqrt(jnp.mean(x * x, axis=-1, keepdims=True) + RMS_EPS) * gain


def _swiglu(u, w1_ref, w3_ref, w2_ref, hid_ref):
    d_ff = w1_ref.shape[1]
    for f in range(d_ff // FF_CHUNK):
        cols = slice(f * FF_CHUNK, (f + 1) * FF_CHUNK)
        a = jnp.dot(u, w1_ref[:, cols], preferred_element_type=F32)
        b = jnp.dot(u, w3_ref[:, cols], preferred_element_type=F32)
        hid_ref[:, cols] = (a * jax.nn.sigmoid(a) * b).astype(BF16)
    return jnp.dot(hid_ref[...], w2_ref[...], preferred_element_type=F32)


def _rope(x, cos, sin):
    j = lax.broadcasted_iota(jnp.int32, x.shape, 1) % HEAD_DIM
    half = ROT_DIM // 2
    partner = jnp.where(j < half, pltpu.roll(x, LANES - half, axis=1), pltpu.roll(x, half, axis=1))
    return jnp.where(j < ROT_DIM, x * cos + partner * sin, x)


def _pool_deltas(pext_ref, first_pos):
    tm = pext_ref.shape[0] - POOL_HALO
    pos = first_pos + lax.broadcasted_iota(jnp.int32, (tm, 1), 0)
    deltas = []
    for gi, win in enumerate(POOL_WINDOWS):
        lanes = slice(gi * POOL_GROUP_WIDTH, (gi + 1) * POOL_GROUP_WIDTH)
        own = pext_ref[POOL_HALO:POOL_HALO + tm, lanes]
        total = own
        for back in range(1, win):
            total = total + pext_ref[POOL_HALO - back:POOL_HALO - back + tm, lanes]
        count = jnp.minimum(pos + 1, win).astype(F32)
        deltas.append((total / count - own).astype(BF16))
    return deltas


def _pool_project(deltas, poolw_ref, pscale_ref, wb_ref):
    pooled = []
    for gi, d in enumerate(deltas):
        lanes = slice(gi * POOL_GROUP_WIDTH, (gi + 1) * POOL_GROUP_WIDTH)
        pooled.append(jnp.dot(d, poolw_ref[gi], preferred_element_type=F32) * pscale_ref[:, lanes])
    return jnp.dot(jnp.concatenate(pooled, axis=1).astype(BF16), wb_ref[...], preferred_element_type=F32)


def _ffn1_proj_kernel(x_ref, g1_ref, w1_ref, w3_ref, w2_ref, gm_ref, win_ref, cos_ref, sin_ref,
                      poolw_ref, pscale_ref, wb_ref,
                      h1_ref, q0_ref, q1_ref, q2_ref, k0_ref, k1_ref, k2_ref, v0_ref, v1_ref, v2_ref,
                      ga_ref, gpool_ref, hid_ref, slab_ref, pext_ref, *, tiles_per_seq):
    tm = x_ref.shape[0]
    d_model = x_ref.shape[1]
    t = pl.program_id(0) % tiles_per_seq

    @pl.when(t == 0)
    def _():
        pext_ref[0:POOL_HALO, :] = jnp.zeros((POOL_HALO, POOL_WIDTH), F32)

    x = x_ref[...]
    u = _rms(x, g1_ref[...]).astype(BF16)
    h1 = x + 0.5 * _swiglu(u, w1_ref, w3_ref, w2_ref, hid_ref)
    h1_ref[...] = h1
    um = _rms(h1, gm_ref[...]).astype(BF16)
    cos = cos_ref[...]
    sin = sin_ref[...]

    def project(col):
        return jnp.dot(um, win_ref[:, col:col + GROUP_WIDTH], preferred_element_type=F32)

    def rope(p):
        return jnp.concatenate([_rope(p[:, :LANES], cos, sin), _rope(p[:, LANES:], cos, sin)], axis=1)

    qkv_width = 3 * N_GROUPS * GROUP_WIDTH
    gate_col = qkv_width + POOL_WIDTH

    def qkv_slab(kind, g, ref, scratch_slot):
        r = DILATED_GROUPS[g][1]
        p = project(("q", "k", "v").index(kind) * N_GROUPS * GROUP_WIDTH + g * GROUP_WIDTH)
        if kind == "q":
            p = rope(p) * SCORE_SCALE_LOG2
        elif kind == "k":
            p = rope(p)
        if r == 1:
            ref[0, 0] = p.astype(BF16)
            return
        for half in range(GROUP_WIDTH // LANES):
            lanes = slice(half * LANES, (half + 1) * LANES)
            slab_ref[scratch_slot + half] = p[:, lanes]
            for c in range(r):
                ref[0, c, :, lanes] = slab_ref[scratch_slot + half, pl.ds(c, tm // r, stride=r), :].astype(BF16)

    for s in range(POOL_WIDTH // GROUP_WIDTH):
        lanes = slice(s * GROUP_WIDTH, (s + 1) * GROUP_WIDTH)
        pext_ref[POOL_HALO:, lanes] = project(qkv_width + s * GROUP_WIDTH)
    deltas = _pool_deltas(pext_ref, t * tm)
    pext_ref[0:POOL_HALO, :] = pext_ref[tm:tm + POOL_HALO, :]
    for s in range(d_model // GROUP_WIDTH):
        lanes = slice(s * GROUP_WIDTH, (s + 1) * GROUP_WIDTH)
        ga_ref[:, lanes] = jax.nn.sigmoid(project(gate_col + s * GROUP_WIDTH)).astype(BF16)
    halves = GROUP_WIDTH // LANES
    slot = 0
    for g, ref in enumerate((q0_ref, q1_ref, q2_ref)):
        qkv_slab("q", g, ref, slot)
        slot += halves if DILATED_GROUPS[g][1] != 1 else 0
    y_pool = _pool_project(deltas, poolw_ref, pscale_ref, wb_ref)
    for s in range(d_model // GROUP_WIDTH):
        lanes = slice(s * GROUP_WIDTH, (s + 1) * GROUP_WIDTH)
        gate_pool = jax.nn.sigmoid(project(gate_col + d_model + s * GROUP_WIDTH))
        gpool_ref[:, lanes] = (gate_pool * y_pool[:, lanes]).astype(BF16)
    for kind, refs in (("k", (k0_ref, k1_ref, k2_ref)), ("v", (v0_ref, v1_ref, v2_ref))):
        for g in reversed(range(N_GROUPS)):
            qkv_slab(kind, g, refs[g], slot)
            slot += halves if DILATED_GROUPS[g][1] != 1 else 0


def _attn_kernel(q_ref, k_ref, v_ref, kh_ref, vh_ref, *rest, blocks_per_residue, n_back, n_cast):
    cast_in, (o_ref, lse_ref), cast_out = rest[:n_cast], rest[n_cast:n_cast + 2], rest[n_cast + 2:2 * n_cast + 2]
    s_ref, p_ref = rest[2 * n_cast + 2:]
    for w_ref, w_bf_ref in zip(cast_in, cast_out):
        w_bf_ref[...] = w_ref[...].astype(BF16)
    rows = q_ref.shape[1]
    nblk = rows // ATTN_BLOCK
    step = pl.program_id(1)
    qi = lax.broadcasted_iota(jnp.int32, (ATTN_BLOCK, 2 * ATTN_BLOCK), 0)
    kj = lax.broadcasted_iota(jnp.int32, (ATTN_BLOCK, 2 * ATTN_BLOCK), 1)
    dist = qi + ATTN_BLOCK - kj
    band = (dist >= 0) & (dist <= n_back)
    in_current = kj >= ATTN_BLOCK
    head_of_lane = lax.broadcasted_iota(jnp.int32, (1, GROUP_WIDTH), 1) // HEAD_DIM
    heads = range(HEADS_PER_GROUP)
    heads_per_vreg = LANES // HEAD_DIM

    def head_rows(h):
        return slice(h * ATTN_BLOCK, (h + 1) * ATTN_BLOCK)

    def keys_values(j, ref, halo_ref):
        if j == 0:
            return jnp.concatenate([halo_ref[0], ref[0, 0:ATTN_BLOCK, :]], axis=0)
        return ref[0, (j - 1) * ATTN_BLOCK:(j + 1) * ATTN_BLOCK, :]

    def scores(j):
        qb = q_ref[0, j * ATTN_BLOCK:(j + 1) * ATTN_BLOCK, :]
        qs = jnp.concatenate([jnp.where(head_of_lane == h, qb, jnp.zeros_like(qb)) for h in heads], axis=0)
        return lax.dot_general(qs, keys_values(j, k_ref, kh_ref), (((1,), (1,)), ((), ())),
                               preferred_element_type=F32)

    def lanes_by_head(per_head):
        first_in_vreg = (head_of_lane[:, :LANES] % heads_per_vreg) == 0
        vregs = [jnp.where(first_in_vreg, per_head[v * heads_per_vreg], per_head[v * heads_per_vreg + 1])
                 for v in range(GROUP_WIDTH // LANES)]
        return jnp.concatenate(vregs, axis=1)

    s_ref[0] = scores(0)
    for j in range(nblk):
        if j + 1 < nblk:
            s_ref[(j + 1) % 2] = scores(j + 1)
        has_prev = ((step * nblk + j) % blocks_per_residue) != 0
        allowed = band & (in_current | has_prev)
        inv_l, lse = [], []
        for h in heads:
            s = jnp.where(allowed, s_ref[j % 2, head_rows(h), :], -jnp.inf)
            m = jnp.max(s, axis=-1, keepdims=True)
            p = jnp.exp2(s - m)
            l = jnp.sum(p, axis=-1, keepdims=True)
            p_ref[head_rows(h), :] = p.astype(BF16)
            inv_l.append(1.0 / l)
            lse.append(m + jnp.log2(l))
        pv = jnp.dot(p_ref[...], keys_values(j, v_ref, vh_ref), preferred_element_type=F32)
        o_heads = [pv[head_rows(h), (h // heads_per_vreg) * LANES:(h // heads_per_vreg + 1) * LANES] * inv_l[h]
                   for h in heads]
        cur = slice(j * ATTN_BLOCK, (j + 1) * ATTN_BLOCK)
        o_ref[0, cur, :] = lanes_by_head(o_heads).astype(BF16)
        lse_ref[0, cur, :] = lanes_by_head([jnp.broadcast_to(x, (ATTN_BLOCK, LANES)) for x in lse])


def _merge_ffn2_kernel(h1_ref, o0_ref, o1_ref, o2_ref, l0_ref, l1_ref, l2_ref, ga_ref, gpool_ref,
                       wa_ref, wo_ref, g2_ref, w1_ref, w3_ref, w2_ref, gf_ref,
                       out_ref, hid_ref, nat_ref):
    tm = h1_ref.shape[0]

    def natural_order(ref, r, slot):
        if r == 1:
            return ref[0, 0].astype(F32)
        halves = []
        for half in range(GROUP_WIDTH // LANES):
            lanes = slice(half * LANES, (half + 1) * LANES)
            for c in range(r):
                nat_ref[slot + half, pl.ds(c, tm // r, stride=r), :] = ref[0, c, :, lanes].astype(F32)
            halves.append(nat_ref[slot + half])
        return jnp.concatenate(halves, axis=1)

    outs, lses = [], []
    slot = 0
    halves_per_slab = GROUP_WIDTH // LANES
    for (_, r), o_ref, l_ref in zip(DILATED_GROUPS, (o0_ref, o1_ref, o2_ref), (l0_ref, l1_ref, l2_ref)):
        outs.append(natural_order(o_ref, r, slot))
        lses.append(natural_order(l_ref, r, slot + halves_per_slab))
        if r != 1:
            slot += 2 * halves_per_slab
    top = functools.reduce(jnp.maximum, lses)
    weights = [jnp.exp2(l - top) for l in lses]
    o = sum(w * og for w, og in zip(weights, outs)) / sum(weights)
    y_attn = jnp.dot(o.astype(BF16), wa_ref[...], preferred_element_type=F32)

    merged = ga_ref[...].astype(F32) * y_attn + gpool_ref[...].astype(F32)
    h2 = h1_ref[...] + jnp.dot(merged.astype(BF16), wo_ref[...], preferred_element_type=F32)
    u2 = _rms(h2, g2_ref[...]).astype(BF16)
    h3 = h2 + 0.5 * _swiglu(u2, w1_ref, w3_ref, w2_ref, hid_ref)
    out_ref[...] = _rms(h3, gf_ref[...])


def _resident(shape):
    return pl.BlockSpec(shape, lambda *_: (0,) * len(shape), pipeline_mode=pl.Buffered(1))


def _rope_tables(seq):
    pos = jnp.arange(seq, dtype=F32)
    inv = ROPE_THETA ** (-jnp.arange(0, ROT_DIM, 2, dtype=F32) / ROT_DIM)
    ang = pos[:, None] * inv[None, :]
    cos, sin = jnp.cos(ang), jnp.sin(ang)
    rest = HEAD_DIM - ROT_DIM
    cos_h = jnp.concatenate([cos, cos, jnp.ones((seq, rest), F32)], axis=1)
    sin_h = jnp.concatenate([-sin, sin, jnp.zeros((seq, rest), F32)], axis=1)
    reps = LANES // HEAD_DIM
    return jnp.tile(cos_h, (1, reps)), jnp.tile(sin_h, (1, reps))


def _layer(h, B, S, ffn1_norm, ffn1_w1, ffn1_w3, ffn1_w2, mix_norm, w_in, w_branch_attn, w_branch_pool,
           pool_w, pool_scale, w_out, ffn2_norm, ffn2_w1, ffn2_w3, ffn2_w2, final_gain, cos_t, sin_t):
    T, D = h.shape
    d_ff = ffn1_w1.shape[1]
    in_width = w_in.shape[1]
    assert in_width == 3 * N_GROUPS * GROUP_WIDTH + POOL_WIDTH + 2 * D
    tm = TOKEN_TILE
    tiles_per_seq = S // tm
    n_tiles = T // tm
    bf = lambda w: w.astype(BF16)
    row = lambda g: g.reshape(1, -1).astype(F32)
    params = pltpu.CompilerParams(dimension_semantics=("arbitrary",), vmem_limit_bytes=VMEM_LIMIT)

    tok = lambda width: pl.BlockSpec((tm, width), lambda i: (i, 0))

    def residue_major(width, r):
        return pl.BlockSpec((1, r, tm // r, width), lambda i: (i // tiles_per_seq, 0, i % tiles_per_seq, 0))

    qkv_shapes = [jax.ShapeDtypeStruct((B, r, S // r, GROUP_WIDTH), BF16) for _, r in DILATED_GROUPS]
    qkv_specs = [residue_major(GROUP_WIDTH, r) for _, r in DILATED_GROUPS]
    outs = pl.pallas_call(
        functools.partial(_ffn1_proj_kernel, tiles_per_seq=tiles_per_seq),
        name="ffn1_proj",
        grid=(n_tiles,),
        in_specs=[tok(D), _resident((1, D)), _resident((D, d_ff)), _resident((D, d_ff)), _resident((d_ff, D)),
                  _resident((1, D)), _resident((D, in_width)),
                  pl.BlockSpec((tm, LANES), lambda i: (i % tiles_per_seq, 0)),
                  pl.BlockSpec((tm, LANES), lambda i: (i % tiles_per_seq, 0)),
                  _resident(pool_w.shape), _resident((1, POOL_WIDTH)), _resident(w_branch_pool.shape)],
        out_specs=[tok(D)] + qkv_specs * 3 + [tok(D), tok(D)],
        out_shape=[jax.ShapeDtypeStruct((T, D), F32)] + qkv_shapes * 3
        + [jax.ShapeDtypeStruct((T, D), BF16), jax.ShapeDtypeStruct((T, D), BF16)],
        scratch_shapes=[pltpu.VMEM((tm, d_ff), BF16),
                        pltpu.VMEM((3 * (N_GROUPS - 1) * GROUP_WIDTH // LANES, tm, LANES), F32),
                        pltpu.VMEM((POOL_HALO + tm, POOL_WIDTH), F32)],
        compiler_params=params,
    )(h, row(ffn1_norm), bf(ffn1_w1), bf(ffn1_w3), bf(ffn1_w2), row(mix_norm), bf(w_in), cos_t, sin_t,
      bf(pool_w), row(pool_scale), bf(w_branch_pool))
    h1 = outs[0]
    qs, ks, vs = outs[1:4], outs[4:7], outs[7:10]
    gate_attn, gated_pool = outs[10], outs[11]

    attn_o, attn_lse = [], []
    rows = ATTN_ROWS
    blocks_per_step = rows // ATTN_BLOCK
    steps_per_batch = S // rows
    attn_steps = B * steps_per_batch
    cast_jobs = ((ffn2_w1, w_out), (ffn2_w3, w_branch_attn), (ffn2_w2,))
    cast_done = []
    for g, (window, r) in enumerate(DILATED_GROUPS):
        L = S // r
        flat = lambda a: a.reshape(B, S, GROUP_WIDTH)
        cur = pl.BlockSpec((1, rows, GROUP_WIDTH), lambda b, s: (b, s, 0))
        prev = pl.BlockSpec((1, ATTN_BLOCK, GROUP_WIDTH),
                            lambda b, s: (b, jnp.maximum(s * blocks_per_step - 1, 0), 0))
        weights = cast_jobs[g]
        assert all(w.shape[0] % (16 * attn_steps) == 0 for w in weights)
        w_specs = [pl.BlockSpec((w.shape[0] // attn_steps, w.shape[1]), lambda b, s: (b * steps_per_batch + s, 0))
                   for w in weights]
        o_g, lse_g, *w_bf = pl.pallas_call(
            functools.partial(_attn_kernel, blocks_per_residue=L // ATTN_BLOCK, n_back=window // r,
                              n_cast=len(weights)),
            name=f"dilated_attn_g{g}",
            grid=(B, steps_per_batch),
            in_specs=[cur, cur, cur, prev, prev] + w_specs,
            out_specs=[cur, cur] + w_specs,
            out_shape=[jax.ShapeDtypeStruct((B, S, GROUP_WIDTH), BF16),
                       jax.ShapeDtypeStruct((B, S, GROUP_WIDTH), F32)]
            + [jax.ShapeDtypeStruct(w.shape, BF16) for w in weights],
            scratch_shapes=[pltpu.VMEM((2, HEADS_PER_GROUP * ATTN_BLOCK, 2 * ATTN_BLOCK), F32),
                            pltpu.VMEM((HEADS_PER_GROUP * ATTN_BLOCK, 2 * ATTN_BLOCK), BF16)],
            compiler_params=pltpu.CompilerParams(dimension_semantics=("arbitrary", "arbitrary")),
        )(flat(qs[g]), flat(ks[g]), flat(vs[g]), flat(ks[g]), flat(vs[g]), *weights)
        attn_o.append(o_g.reshape(B, r, L, GROUP_WIDTH))
        attn_lse.append(lse_g.reshape(B, r, L, GROUP_WIDTH))
        cast_done.extend(w_bf)
    ffn2_w1_bf, w_out_bf, ffn2_w3_bf, w_branch_attn_bf, ffn2_w2_bf = cast_done

    o_specs = [residue_major(GROUP_WIDTH, r) for _, r in DILATED_GROUPS]
    return pl.pallas_call(
        _merge_ffn2_kernel,
        name="merge_ffn2",
        grid=(n_tiles,),
        in_specs=[tok(D)] + o_specs + o_specs + [tok(D), tok(D)]
        + [_resident(w_branch_attn.shape), _resident(w_out.shape), _resident((1, D)),
           _resident((D, d_ff)), _resident((D, d_ff)), _resident((d_ff, D)), _resident((1, D))],
        out_specs=tok(D),
        out_shape=jax.ShapeDtypeStruct((T, D), F32),
        scratch_shapes=[pltpu.VMEM((tm, d_ff), BF16),
                        pltpu.VMEM((2 * (N_GROUPS - 1) * GROUP_WIDTH // LANES, tm, LANES), F32)],
        compiler_params=params,
    )(h1, *attn_o, *attn_lse, gate_attn, gated_pool, w_branch_attn_bf, w_out_bf, row(ffn2_norm),
      ffn2_w1_bf, ffn2_w3_bf, ffn2_w2_bf, row(final_gain))


def kernel(x, ffn1_norm, ffn1_w1, ffn1_w3, ffn1_w2, mix_norm, w_in, w_branch_attn, w_branch_pool, pool_w,
           pool_scale, w_out, ffn2_norm, ffn2_w1, ffn2_w3, ffn2_w2, final_norm):
    B, S, D = x.shape
    depth = ffn1_norm.shape[0]
    assert S % TOKEN_TILE == 0 and S % ATTN_ROWS == 0
    assert all(window // r == ATTN_BLOCK and (S // r) % ATTN_BLOCK == 0 for window, r in DILATED_GROUPS)
    assert all(TOKEN_TILE % (16 * r) == 0 for _, r in DILATED_GROUPS)
    cos_t, sin_t = _rope_tables(S)
    h = x.reshape(B * S, D)
    for l in range(depth):
        assert l == depth - 1, "only the last layer's output norm is implemented"
        h = _layer(h, B, S, ffn1_norm[l], ffn1_w1[l], ffn1_w3[l], ffn1_w2[l], mix_norm[l], w_in[l],
                   w_branch_attn[l], w_branch_pool[l], pool_w[l], pool_scale[l], w_out[l], ffn2_norm[l],
                   ffn2_w1[l], ffn2_w3[l], ffn2_w2[l], final_norm, cos_t, sin_t)
    return h.reshape(B, S, D)
```

```python
import functools

import jax
import jax.numpy as jnp
from jax import lax
from jax.experimental import pallas as pl
from jax.experimental.pallas import tpu as pltpu

F32 = jnp.float32
BF16 = jnp.bfloat16

HEAD_DIM = 64
HEADS_PER_GROUP = 4
GROUP_WIDTH = HEADS_PER_GROUP * HEAD_DIM
DILATED_GROUPS = ((128, 1), (512, 4), (2048, 16))
N_GROUPS = len(DILATED_GROUPS)
ATTN_BLOCK = 128
ROT_DIM = HEAD_DIM // 4
ROPE_THETA = 500000.0
POOL_WINDOWS = (2, 4, 8, 16)
POOL_GROUP_WIDTH = 128
POOL_WIDTH = len(POOL_WINDOWS) * POOL_GROUP_WIDTH
POOL_HALO = max(POOL_WINDOWS)
RMS_EPS = 1e-6
SCORE_SCALE_LOG2 = HEAD_DIM ** -0.5 * 1.4426950408889634

LANES = 128
V7X_VMEM_BYTES = 64 * 1024 * 1024

TOKEN_TILE = 512
FF_CHUNK = 256
ATTN_ROWS = 1024
CAST_CHUNK_ROWS_WIDE = 32
CAST_CHUNK_ROWS_TALL = 128
VMEM_LIMIT = 56 * 1024 * 1024


def _rms(x, gain):
    return x * lax.rsqrt(jnp.mean(x * x, axis=-1, keepdims=True) + RMS_EPS) * gain


def _swiglu(u, w1_ref, w3_ref, w2_ref, hid_ref):
    d_ff = w1_ref.shape[1]
    for f in range(d_ff // FF_CHUNK):
        cols = slice(f * FF_CHUNK, (f + 1) * FF_CHUNK)
        a = jnp.dot(u, w1_ref[:, cols], preferred_element_type=F32)
        b = jnp.dot(u, w3_ref[:, cols], preferred_element_type=F32)
        hid_ref[:, cols] = (a * jax.nn.sigmoid(a) * b).astype(BF16)
    return jnp.dot(hid_ref[...], w2_ref[...], preferred_element_type=F32)


def _rope(x, cos, sin):
    j = lax.broadcasted_iota(jnp.int32, x.shape, 1) % HEAD_DIM
    half = ROT_DIM // 2
    partner = jnp.where(j < half, pltpu.roll(x, LANES - half, axis=1), pltpu.roll(x, half, axis=1))
    return jnp.where(j < ROT_DIM, x * cos + partner * sin, x)


def _pool_deltas(pext_ref, first_pos):
    tm = pext_ref.shape[0] - POOL_HALO
    pos = first_pos + lax.broadcasted_iota(jnp.int32, (tm, 1), 0)
    deltas = []
    for gi, win in enumerate(POOL_WINDOWS):
        lanes = slice(gi * POOL_GROUP_WIDTH, (gi + 1) * POOL_GROUP_WIDTH)
        own = pext_ref[POOL_HALO:POOL_HALO + tm, lanes]
        total = own
        for back in range(1, win):
            total = total + pext_ref[POOL_HALO - back:POOL_HALO - back + tm, lanes]
        count = jnp.minimum(pos + 1, win).astype(F32)
        deltas.append((total / count - own).astype(BF16))
    return deltas


def _pool_project(deltas, poolw_ref, pscale_ref, wb_ref):
    pooled = []
    for gi, d in enumerate(deltas):
        lanes = slice(gi * POOL_GROUP_WIDTH, (gi + 1) * POOL_GROUP_WIDTH)
        pooled.append(jnp.dot(d, poolw_ref[gi], preferred_element_type=F32) * pscale_ref[:, lanes])
    return jnp.dot(jnp.concatenate(pooled, axis=1).astype(BF16), wb_ref[...], preferred_element_type=F32)


def _cast_weight_from_hbm(src_hbm, dst_ref, stage_ref, sem_ref):
    rows, cols = src_hbm.shape
    chunk = stage_ref.shape[1]
    n_chunks = rows // chunk

    def copy(i, slot):
        return pltpu.make_async_copy(src_hbm.at[pl.ds(i * chunk, chunk), :],
                                     stage_ref.at[slot, :, pl.ds(0, cols)], sem_ref.at[slot])

    copy(0, 0).start()

    def body(i, carry):
        slot = i % 2

        @pl.when(i + 1 < n_chunks)
        def _():
            copy(i + 1, 1 - slot).start()

        copy(i, slot).wait()
        dst_ref[pl.ds(pl.multiple_of(i * chunk, chunk), chunk), :] = stage_ref[slot, :, 0:cols].astype(BF16)
        return carry

    lax.fori_loop(0, n_chunks, body, 0)


def _ffn1_proj_kernel(x_ref, g1_ref, w1_hbm, w3_hbm, w2_hbm, gm_ref, win_hbm, cos_ref, sin_ref,
                      poolw_ref, pscale_ref, wb_ref,
                      h1_ref, q0_ref, q1_ref, q2_ref, k0_ref, k1_ref, k2_ref, v0_ref, v1_ref, v2_ref,
                      ga_ref, gpool_ref, hid_ref, slab_ref, pext_ref,
                      w1_ref, w3_ref, w2_ref, win_ref, wide_stage_ref, tall_stage_ref, sem_ref, *, tiles_per_seq):
    tm = x_ref.shape[0]
    d_model = x_ref.shape[1]
    t = pl.program_id(0) % tiles_per_seq

    @pl.when(pl.program_id(0) == 0)
    def _():
        _cast_weight_from_hbm(w1_hbm, w1_ref, wide_stage_ref, sem_ref)
        _cast_weight_from_hbm(w3_hbm, w3_ref, wide_stage_ref, sem_ref)
        _cast_weight_from_hbm(w2_hbm, w2_ref, tall_stage_ref, sem_ref)
        _cast_weight_from_hbm(win_hbm, win_ref, wide_stage_ref, sem_ref)

    @pl.when(t == 0)
    def _():
        pext_ref[0:POOL_HALO, :] = jnp.zeros((POOL_HALO, POOL_WIDTH), F32)

    x = x_ref[...]
    u = _rms(x, g1_ref[...]).astype(BF16)
    h1 = x + 0.5 * _swiglu(u, w1_ref, w3_ref, w2_ref, hid_ref)
    h1_ref[...] = h1
    um = _rms(h1, gm_ref[...]).astype(BF16)
    cos = cos_ref[...]
    sin = sin_ref[...]

    def project(col):
        return jnp.dot(um, win_ref[:, col:col + GROUP_WIDTH], preferred_element_type=F32)

    def rope(p):
        return jnp.concatenate([_rope(p[:, :LANES], cos, sin), _rope(p[:, LANES:], cos, sin)], axis=1)

    qkv_width = 3 * N_GROUPS * GROUP_WIDTH
    gate_col = qkv_width + POOL_WIDTH

    def qkv_slab(kind, g, ref, scratch_slot):
        r = DILATED_GROUPS[g][1]
        p = project(("q", "k", "v").index(kind) * N_GROUPS * GROUP_WIDTH + g * GROUP_WIDTH)
        if kind == "q":
            p = rope(p) * SCORE_SCALE_LOG2
        elif kind == "k":
            p = rope(p)
        if r == 1:
            ref[0, 0] = p.astype(BF16)
            return
        for half in range(GROUP_WIDTH // LANES):
            lanes = slice(half * LANES, (half + 1) * LANES)
            slab_ref[scratch_slot + half] = p[:, lanes]
            for c in range(r):
                ref[0, c, :, lanes] = slab_ref[scratch_slot + half, pl.ds(c, tm // r, stride=r), :].astype(BF16)

    for s in range(POOL_WIDTH // GROUP_WIDTH):
        lanes = slice(s * GROUP_WIDTH, (s + 1) * GROUP_WIDTH)
        pext_ref[POOL_HALO:, lanes] = project(qkv_width + s * GROUP_WIDTH)
    deltas = _pool_deltas(pext_ref, t * tm)
    pext_ref[0:POOL_HALO, :] = pext_ref[tm:tm + POOL_HALO, :]
    for s in range(d_model // GROUP_WIDTH):
        lanes = slice(s * GROUP_WIDTH, (s + 1) * GROUP_WIDTH)
        ga_ref[:, lanes] = jax.nn.sigmoid(project(gate_col + s * GROUP_WIDTH)).astype(BF16)
    halves = GROUP_WIDTH // LANES
    slot = 0
    for g, ref in enumerate((q0_ref, q1_ref, q2_ref)):
        qkv_slab("q", g, ref, slot)
        slot += halves if DILATED_GROUPS[g][1] != 1 else 0
    y_pool = _pool_project(deltas, poolw_ref, pscale_ref, wb_ref)
    for s in range(d_model // GROUP_WIDTH):
        lanes = slice(s * GROUP_WIDTH, (s + 1) * GROUP_WIDTH)
        gate_pool = jax.nn.sigmoid(project(gate_col + d_model + s * GROUP_WIDTH))
        gpool_ref[:, lanes] = (gate_pool * y_pool[:, lanes]).astype(BF16)
    for kind, refs in (("k", (k0_ref, k1_ref, k2_ref)), ("v", (v0_ref, v1_ref, v2_ref))):
        for g in reversed(range(N_GROUPS)):
            qkv_slab(kind, g, refs[g], slot)
            slot += halves if DILATED_GROUPS[g][1] != 1 else 0


def _attn_kernel(q_ref, k_ref, v_ref, kh_ref, vh_ref, *rest, blocks_per_residue, n_back, n_cast):
    cast_in, (o_ref, lse_ref), cast_out = rest[:n_cast], rest[n_cast:n_cast + 2], rest[n_cast + 2:2 * n_cast + 2]
    s_ref, p_ref = rest[2 * n_cast + 2:]
    for w_ref, w_bf_ref in zip(cast_in, cast_out):
        w_bf_ref[...] = w_ref[...].astype(BF16)
    rows = q_ref.shape[1]
    nblk = rows // ATTN_BLOCK
    step = pl.program_id(1)
    qi = lax.broadcasted_iota(jnp.int32, (ATTN_BLOCK, 2 * ATTN_BLOCK), 0)
    kj = lax.broadcasted_iota(jnp.int32, (ATTN_BLOCK, 2 * ATTN_BLOCK), 1)
    dist = qi + ATTN_BLOCK - kj
    band = (dist >= 0) & (dist <= n_back)
    in_current = kj >= ATTN_BLOCK
    head_of_lane = lax.broadcasted_iota(jnp.int32, (1, GROUP_WIDTH), 1) // HEAD_DIM
    heads = range(HEADS_PER_GROUP)
    heads_per_vreg = LANES // HEAD_DIM

    def head_rows(h):
        return slice(h * ATTN_BLOCK, (h + 1) * ATTN_BLOCK)

    def keys_values(j, ref, halo_ref):
        if j == 0:
            return jnp.concatenate([halo_ref[0], ref[0, 0:ATTN_BLOCK, :]], axis=0)
        return ref[0, (j - 1) * ATTN_BLOCK:(j + 1) * ATTN_BLOCK, :]

    def scores(j):
        qb = q_ref[0, j * ATTN_BLOCK:(j + 1) * ATTN_BLOCK, :]
        qs = jnp.concatenate([jnp.where(head_of_lane == h, qb, jnp.zeros_like(qb)) for h in heads], axis=0)
        return lax.dot_general(qs, keys_values(j, k_ref, kh_ref), (((1,), (1,)), ((), ())),
                               preferred_element_type=F32)

    def lanes_by_head(per_head):
        first_in_vreg = (head_of_lane[:, :LANES] % heads_per_vreg) == 0
        vregs = [jnp.where(first_in_vreg, per_head[v * heads_per_vreg], per_head[v * heads_per_vreg + 1])
                 for v in range(GROUP_WIDTH // LANES)]
        return jnp.concatenate(vregs, axis=1)

    s_ref[0] = scores(0)
    for j in range(nblk):
        if j + 1 < nblk:
            s_ref[(j + 1) % 2] = scores(j + 1)
        has_prev = ((step * nblk + j) % blocks_per_residue) != 0
        allowed = band & (in_current | has_prev)
        inv_l, lse = [], []
        for h in heads:
            s = jnp.where(allowed, s_ref[j % 2, head_rows(h), :], -jnp.inf)
            m = jnp.max(s, axis=-1, keepdims=True)
            p = jnp.exp2(s - m)
            l = jnp.sum(p, axis=-1, keepdims=True)
            p_ref[head_rows(h), :] = p.astype(BF16)
            inv_l.append(1.0 / l)
            lse.append(m + jnp.log2(l))
        pv = jnp.dot(p_ref[...], keys_values(j, v_ref, vh_ref), preferred_element_type=F32)
        o_heads = [pv[head_rows(h), (h // heads_per_vreg) * LANES:(h // heads_per_vreg + 1) * LANES] * inv_l[h]
                   for h in heads]
        cur = slice(j * ATTN_BLOCK, (j + 1) * ATTN_BLOCK)
        o_ref[0, cur, :] = lanes_by_head(o_heads).astype(BF16)
        lse_ref[0, cur, :] = lanes_by_head([jnp.broadcast_to(x, (ATTN_BLOCK, LANES)) for x in lse])


def _merge_ffn2_kernel(h1_ref, o0_ref, o1_ref, o2_ref, l0_ref, l1_ref, l2_ref, ga_ref, gpool_ref,
                       wa_ref, wo_ref, g2_ref, w1_ref, w3_ref, w2_ref, gf_ref,
                       out_ref, hid_ref, nat_ref):
    tm = h1_ref.shape[0]

    def natural_order(ref, r, slot):
        if r == 1:
            return ref[0, 0].astype(F32)
        halves = []
        for half in range(GROUP_WIDTH // LANES):
            lanes = slice(half * LANES, (half + 1) * LANES)
            for c in range(r):
                nat_ref[slot + half, pl.ds(c, tm // r, stride=r), :] = ref[0, c, :, lanes].astype(F32)
            halves.append(nat_ref[slot + half])
        return jnp.concatenate(halves, axis=1)

    outs, lses = [], []
    slot = 0
    halves_per_slab = GROUP_WIDTH // LANES
    for (_, r), o_ref, l_ref in zip(DILATED_GROUPS, (o0_ref, o1_ref, o2_ref), (l0_ref, l1_ref, l2_ref)):
        outs.append(natural_order(o_ref, r, slot))
        lses.append(natural_order(l_ref, r, slot + halves_per_slab))
        if r != 1:
            slot += 2 * halves_per_slab
    top = functools.reduce(jnp.maximum, lses)
    weights = [jnp.exp2(l - top) for l in lses]
    o = sum(w * og for w, og in zip(weights, outs)) / sum(weights)
    y_attn = jnp.dot(o.astype(BF16), wa_ref[...], preferred_element_type=F32)

    merged = ga_ref[...].astype(F32) * y_attn + gpool_ref[...].astype(F32)
    h2 = h1_ref[...] + jnp.dot(merged.astype(BF16), wo_ref[...], preferred_element_type=F32)
    u2 = _rms(h2, g2_ref[...]).astype(BF16)
    h3 = h2 + 0.5 * _swiglu(u2, w1_ref, w3_ref, w2_ref, hid_ref)
    out_ref[...] = _rms(h3, gf_ref[...])


def _resident(shape):
    return pl.BlockSpec(shape, lambda *_: (0,) * len(shape), pipeline_mode=pl.Buffered(1))


def _rope_tables(seq):
    pos = jnp.arange(seq, dtype=F32)
    inv = ROPE_THETA ** (-jnp.arange(0, ROT_DIM, 2, dtype=F32) / ROT_DIM)
    ang = pos[:, None] * inv[None, :]
    cos, sin = jnp.cos(ang), jnp.sin(ang)
    rest = HEAD_DIM - ROT_DIM
    cos_h = jnp.concatenate([cos, cos, jnp.ones((seq, rest), F32)], axis=1)
    sin_h = jnp.concatenate([-sin, sin, jnp.zeros((seq, rest), F32)], axis=1)
    reps = LANES // HEAD_DIM
    return jnp.tile(cos_h, (1, reps)), jnp.tile(sin_h, (1, reps))


def _layer(h, B, S, ffn1_norm, ffn1_w1, ffn1_w3, ffn1_w2, mix_norm, w_in, w_branch_attn, w_branch_pool,
           pool_w, pool_scale, w_out, ffn2_norm, ffn2_w1, ffn2_w3, ffn2_w2, final_gain, cos_t, sin_t):
    T, D = h.shape
    d_ff = ffn1_w1.shape[1]
    in_width = w_in.shape[1]
    assert in_width == 3 * N_GROUPS * GROUP_WIDTH + POOL_WIDTH + 2 * D
    tm = TOKEN_TILE
    tiles_per_seq = S // tm
    n_tiles = T // tm
    bf = lambda w: w.astype(BF16)
    row = lambda g: g.reshape(1, -1).astype(F32)
    params = pltpu.CompilerParams(dimension_semantics=("arbitrary",), vmem_limit_bytes=VMEM_LIMIT)

    tok = lambda width: pl.BlockSpec((tm, width), lambda i: (i, 0))
    in_hbm = pl.BlockSpec(memory_space=pl.ANY)
    assert D % CAST_CHUNK_ROWS_WIDE == 0 and d_ff % CAST_CHUNK_ROWS_TALL == 0

    def residue_major(width, r):
        return pl.BlockSpec((1, r, tm // r, width), lambda i: (i // tiles_per_seq, 0, i % tiles_per_seq, 0))

    qkv_shapes = [jax.ShapeDtypeStruct((B, r, S // r, GROUP_WIDTH), BF16) for _, r in DILATED_GROUPS]
    qkv_specs = [residue_major(GROUP_WIDTH, r) for _, r in DILATED_GROUPS]
    outs = pl.pallas_call(
        functools.partial(_ffn1_proj_kernel, tiles_per_seq=tiles_per_seq),
        name="ffn1_proj",
        grid=(n_tiles,),
        in_specs=[tok(D), _resident((1, D)), in_hbm, in_hbm, in_hbm, _resident((1, D)), in_hbm,
                  pl.BlockSpec((tm, LANES), lambda i: (i % tiles_per_seq, 0)),
                  pl.BlockSpec((tm, LANES), lambda i: (i % tiles_per_seq, 0)),
                  _resident(pool_w.shape), _resident((1, POOL_WIDTH)), _resident(w_branch_pool.shape)],
        out_specs=[tok(D)] + qkv_specs * 3 + [tok(D), tok(D)],
        out_shape=[jax.ShapeDtypeStruct((T, D), F32)] + qkv_shapes * 3
        + [jax.ShapeDtypeStruct((T, D), BF16), jax.ShapeDtypeStruct((T, D), BF16)],
        scratch_shapes=[pltpu.VMEM((tm, d_ff), BF16),
                        pltpu.VMEM((3 * (N_GROUPS - 1) * GROUP_WIDTH // LANES, tm, LANES), F32),
                        pltpu.VMEM((POOL_HALO + tm, POOL_WIDTH), F32),
                        pltpu.VMEM((D, d_ff), BF16), pltpu.VMEM((D, d_ff), BF16), pltpu.VMEM((d_ff, D), BF16),
                        pltpu.VMEM((D, in_width), BF16),
                        pltpu.VMEM((2, CAST_CHUNK_ROWS_WIDE, max(d_ff, in_width)), F32),
                        pltpu.VMEM((2, CAST_CHUNK_ROWS_TALL, D), F32),
                        pltpu.SemaphoreType.DMA((2,))],
        compiler_params=params,
    )(h, row(ffn1_norm), ffn1_w1, ffn1_w3, ffn1_w2, row(mix_norm), w_in, cos_t, sin_t,
      bf(pool_w), row(pool_scale), bf(w_branch_pool))
    h1 = outs[0]
    qs, ks, vs = outs[1:4], outs[4:7], outs[7:10]
    gate_attn, gated_pool = outs[10], outs[11]

    attn_o, attn_lse = [], []
    rows = ATTN_ROWS
    blocks_per_step = rows // ATTN_BLOCK
    steps_per_batch = S // rows
    attn_steps = B * steps_per_batch
    cast_jobs = ((ffn2_w1, w_out), (ffn2_w3, w_branch_attn), (ffn2_w2,))
    cast_done = []
    for g, (window, r) in enumerate(DILATED_GROUPS):
        L = S // r
        flat = lambda a: a.reshape(B, S, GROUP_WIDTH)
        cur = pl.BlockSpec((1, rows, GROUP_WIDTH), lambda b, s: (b, s, 0))
        prev = pl.BlockSpec((1, ATTN_BLOCK, GROUP_WIDTH),
                            lambda b, s: (b, jnp.maximum(s * blocks_per_step - 1, 0), 0))
        weights = cast_jobs[g]
        assert all(w.shape[0] % (16 * attn_steps) == 0 for w in weights)
        w_specs = [pl.BlockSpec((w.shape[0] // attn_steps, w.shape[1]), lambda b, s: (b * steps_per_batch + s, 0))
                   for w in weights]
        o_g, lse_g, *w_bf = pl.pallas_call(
            functools.partial(_attn_kernel, blocks_per_residue=L // ATTN_BLOCK, n_back=window // r,
                              n_cast=len(weights)),
            name=f"dilated_attn_g{g}",
            grid=(B, steps_per_batch),
            in_specs=[cur, cur, cur, prev, prev] + w_specs,
            out_specs=[cur, cur] + w_specs,
            out_shape=[jax.ShapeDtypeStruct((B, S, GROUP_WIDTH), BF16),
                       jax.ShapeDtypeStruct((B, S, GROUP_WIDTH), F32)]
            + [jax.ShapeDtypeStruct(w.shape, BF16) for w in weights],
            scratch_shapes=[pltpu.VMEM((2, HEADS_PER_GROUP * ATTN_BLOCK, 2 * ATTN_BLOCK), F32),
                            pltpu.VMEM((HEADS_PER_GROUP * ATTN_BLOCK, 2 * ATTN_BLOCK), BF16)],
            compiler_params=pltpu.CompilerParams(dimension_semantics=("arbitrary", "arbitrary")),
        )(flat(qs[g]), flat(ks[g]), flat(vs[g]), flat(ks[g]), flat(vs[g]), *weights)
        attn_o.append(o_g.reshape(B, r, L, GROUP_WIDTH))
        attn_lse.append(lse_g.reshape(B, r, L, GROUP_WIDTH))
        cast_done.extend(w_bf)
    ffn2_w1_bf, w_out_bf, ffn2_w3_bf, w_branch_attn_bf, ffn2_w2_bf = cast_done

    o_specs = [residue_major(GROUP_WIDTH, r) for _, r in DILATED_GROUPS]
    return pl.pallas_call(
        _merge_ffn2_kernel,
        name="merge_ffn2",
        grid=(n_tiles,),
        in_specs=[tok(D)] + o_specs + o_specs + [tok(D), tok(D)]
        + [_resident(w_branch_attn.shape), _resident(w_out.shape), _resident((1, D)),
           _resident((D, d_ff)), _resident((D, d_ff)), _resident((d_ff, D)), _resident((1, D))],
        out_specs=tok(D),
        out_shape=jax.ShapeDtypeStruct((T, D), F32),
        scratch_shapes=[pltpu.VMEM((tm, d_ff), BF16),
                        pltpu.VMEM((2 * (N_GROUPS - 1) * GROUP_WIDTH // LANES, tm, LANES), F32)],
        compiler_params=params,
    )(h1, *attn_o, *attn_lse, gate_attn, gated_pool, w_branch_attn_bf, w_out_bf, row(ffn2_norm),
      ffn2_w1_bf, ffn2_w3_bf, ffn2_w2_bf, row(final_gain))


def kernel(x, ffn1_norm, ffn1_w1, ffn1_w3, ffn1_w2, mix_norm, w_in, w_branch_attn, w_branch_pool, pool_w,
           pool_scale, w_out, ffn2_norm, ffn2_w1, ffn2_w3, ffn2_w2, final_norm):
    B, S, D = x.shape
    depth = ffn1_norm.shape[0]
    assert S % TOKEN_TILE == 0 and S % ATTN_ROWS == 0
    assert all(window // r == ATTN_BLOCK and (S // r) % ATTN_BLOCK == 0 for window, r in DILATED_GROUPS)
    assert all(TOKEN_TILE % (16 * r) == 0 for _, r in DILATED_GROUPS)
    cos_t, sin_t = _rope_tables(S)
    h = x.reshape(B * S, D)
    for l in range(depth):
        assert l == depth - 1, "only the last layer's output norm is implemented"
        h = _layer(h, B, S, ffn1_norm[l], ffn1_w1[l], ffn1_w3[l], ffn1_w2[l], mix_norm[l], w_in[l],
                   w_branch_attn[l], w_branch_pool[l], pool_w[l], pool_scale[l], w_out[l], ffn2_norm[l],
                   ffn2_w1[l], ffn2_w3[l], ffn2_w2[l], final_norm, cos_t, sin_t)
    return h.reshape(B, S, D)
```

```python
import functools

import jax
import jax.numpy as jnp
from jax import lax
from jax.experimental import pallas as pl
from jax.experimental.pallas import tpu as pltpu

F32 = jnp.float32
BF16 = jnp.bfloat16

HEAD_DIM = 64
HEADS_PER_GROUP = 4
GROUP_WIDTH = HEADS_PER_GROUP * HEAD_DIM
DILATED_GROUPS = ((128, 1), (512, 4), (2048, 16))
N_GROUPS = len(DILATED_GROUPS)
ATTN_BLOCK = 128
ROT_DIM = HEAD_DIM // 4
ROPE_THETA = 500000.0
POOL_WINDOWS = (2, 4, 8, 16)
POOL_GROUP_WIDTH = 128
POOL_WIDTH = len(POOL_WINDOWS) * POOL_GROUP_WIDTH
POOL_HALO = max(POOL_WINDOWS)
RMS_EPS = 1e-6
SCORE_SCALE_LOG2 = HEAD_DIM ** -0.5 * 1.4426950408889634

LANES = 128
V7X_VMEM_BYTES = 64 * 1024 * 1024

TOKEN_TILE = 512
FF_CHUNK = 256
ATTN_ROWS = 1024
DEINTERLEAVE_STEP = 4
VMEM_LIMIT = 56 * 1024 * 1024


def _rms(x, gain):
    return x * lax.rsqrt(jnp.mean(x * x, axis=-1, keepdims=True) + RMS_EPS) * gain


def _sigmoid(x):
    return 0.5 * jnp.tanh(0.5 * x) + 0.5


def _swiglu(u, w1_ref, w3_ref, w2_ref, hid_ref):
    d_ff = w1_ref.shape[1]
    for f in range(d_ff // FF_CHUNK):
        cols = slice(f * FF_CHUNK, (f + 1) * FF_CHUNK)
        a = jnp.dot(u, w1_ref[:, cols], preferred_element_type=F32)
        b = jnp.dot(u, w3_ref[:, cols], preferred_element_type=F32)
        half = 0.5 * a
        hid_ref[:, cols] = ((half * jnp.tanh(half) + half) * b).astype(BF16)
    return jnp.dot(hid_ref[...], w2_ref[...], preferred_element_type=F32)


def _slots_per_half(r):
    return 0 if r == 1 else 1 if r <= DEINTERLEAVE_STEP else 2


def _rope(x, cos, sin):
    j = lax.broadcasted_iota(jnp.int32, x.shape, 1) % HEAD_DIM
    half = ROT_DIM // 2
    partner = jnp.where(j < half, pltpu.roll(x, LANES - half, axis=1), pltpu.roll(x, half, axis=1))
    return jnp.where(j < ROT_DIM, x * cos + partner * sin, x)


def _pool_deltas(pext_ref, first_pos):
    tm = pext_ref.shape[0] - POOL_HALO
    pos = first_pos + lax.broadcasted_iota(jnp.int32, (tm, 1), 0)
    deltas = []
    for gi, win in enumerate(POOL_WINDOWS):
        lanes = slice(gi * POOL_GROUP_WIDTH, (gi + 1) * POOL_GROUP_WIDTH)
        own = pext_ref[POOL_HALO:POOL_HALO + tm, lanes]
        total = own
        for back in range(1, win):
            total = total + pext_ref[POOL_HALO - back:POOL_HALO - back + tm, lanes]
        count = jnp.minimum(pos + 1, win).astype(F32)
        deltas.append((total / count - own).astype(BF16))
    return deltas


def _pool_project(deltas, poolw_ref, pscale_ref, wb_ref):
    pooled = []
    for gi, d in enumerate(deltas):
        lanes = slice(gi * POOL_GROUP_WIDTH, (gi + 1) * POOL_GROUP_WIDTH)
        pooled.append(jnp.dot(d, poolw_ref[gi], preferred_element_type=F32) * pscale_ref[:, lanes])
    return jnp.dot(jnp.concatenate(pooled, axis=1).astype(BF16), wb_ref[...], preferred_element_type=F32)


def _ffn1_proj_kernel(x_ref, g1_ref, w1_ref, w3_ref, w2_ref, gm_ref, win_ref, cos_ref, sin_ref,
                      poolw_ref, pscale_ref, wb_ref,
                      h1_ref, q0_ref, q1_ref, q2_ref, k0_ref, k1_ref, k2_ref, v0_ref, v1_ref, v2_ref,
                      ga_ref, gpool_ref, hid_ref, slab_ref, pext_ref, *, tiles_per_seq):
    tm = x_ref.shape[0]
    d_model = x_ref.shape[1]
    t = pl.program_id(0) % tiles_per_seq

    @pl.when(t == 0)
    def _():
        pext_ref[0:POOL_HALO, :] = jnp.zeros((POOL_HALO, POOL_WIDTH), F32)

    x = x_ref[...]
    u = _rms(x, g1_ref[...]).astype(BF16)
    h1 = x + 0.5 * _swiglu(u, w1_ref, w3_ref, w2_ref, hid_ref)
    h1_ref[...] = h1
    um = _rms(h1, gm_ref[...]).astype(BF16)
    cos = cos_ref[...]
    sin = sin_ref[...]

    def project(col):
        return jnp.dot(um, win_ref[:, col:col + GROUP_WIDTH], preferred_element_type=F32)

    def rope(p):
        return jnp.concatenate([_rope(p[:, :LANES], cos, sin), _rope(p[:, LANES:], cos, sin)], axis=1)

    qkv_width = 3 * N_GROUPS * GROUP_WIDTH
    gate_col = qkv_width + POOL_WIDTH

    def qkv_slab(kind, g, ref, scratch_slot):
        r = DILATED_GROUPS[g][1]
        p = project(("q", "k", "v").index(kind) * N_GROUPS * GROUP_WIDTH + g * GROUP_WIDTH)
        if kind == "q":
            p = rope(p) * SCORE_SCALE_LOG2
        elif kind == "k":
            p = rope(p)
        if r == 1:
            ref[0, 0] = p.astype(BF16)
            return
        for half in range(GROUP_WIDTH // LANES):
            lanes = slice(half * LANES, (half + 1) * LANES)
            first = scratch_slot + half * _slots_per_half(r)
            slab_ref[first] = p[:, lanes]
            if _slots_per_half(r) == 1:
                for c in range(r):
                    ref[0, c, :, lanes] = slab_ref[first, pl.ds(c, tm // r, stride=r), :].astype(BF16)
                continue
            step, rest = DEINTERLEAVE_STEP, r // DEINTERLEAVE_STEP
            for c0 in range(step):
                slab_ref[first + 1, c0 * (tm // step):(c0 + 1) * (tm // step), :] = (
                    slab_ref[first, pl.ds(c0, tm // step, stride=step), :])
            for c0 in range(step):
                for c1 in range(rest):
                    ref[0, c1 * step + c0, :, lanes] = slab_ref[
                        first + 1, pl.ds(c0 * (tm // step) + c1, tm // r, stride=rest), :].astype(BF16)

    for s in range(POOL_WIDTH // GROUP_WIDTH):
        lanes = slice(s * GROUP_WIDTH, (s + 1) * GROUP_WIDTH)
        pext_ref[POOL_HALO:, lanes] = project(qkv_width + s * GROUP_WIDTH)
    deltas = _pool_deltas(pext_ref, t * tm)
    pext_ref[0:POOL_HALO, :] = pext_ref[tm:tm + POOL_HALO, :]
    for s in range(d_model // GROUP_WIDTH):
        lanes = slice(s * GROUP_WIDTH, (s + 1) * GROUP_WIDTH)
        ga_ref[:, lanes] = _sigmoid(project(gate_col + s * GROUP_WIDTH)).astype(BF16)
    halves = GROUP_WIDTH // LANES
    slot = 0
    for g, ref in enumerate((q0_ref, q1_ref, q2_ref)):
        qkv_slab("q", g, ref, slot)
        slot += halves * _slots_per_half(DILATED_GROUPS[g][1])
    y_pool = _pool_project(deltas, poolw_ref, pscale_ref, wb_ref)
    for s in range(d_model // GROUP_WIDTH):
        lanes = slice(s * GROUP_WIDTH, (s + 1) * GROUP_WIDTH)
        gate_pool = _sigmoid(project(gate_col + d_model + s * GROUP_WIDTH))
        gpool_ref[:, lanes] = (gate_pool * y_pool[:, lanes]).astype(BF16)
    for kind, refs in (("k", (k0_ref, k1_ref, k2_ref)), ("v", (v0_ref, v1_ref, v2_ref))):
        for g in reversed(range(N_GROUPS)):
            qkv_slab(kind, g, refs[g], slot)
            slot += halves * _slots_per_half(DILATED_GROUPS[g][1])


def _attn_kernel(q_ref, k_ref, v_ref, kh_ref, vh_ref, *rest, blocks_per_residue, n_back, n_cast):
    cast_in, (o_ref, lse_ref), cast_out = rest[:n_cast], rest[n_cast:n_cast + 2], rest[n_cast + 2:2 * n_cast + 2]
    s_ref, p_ref = rest[2 * n_cast + 2:]
    for w_ref, w_bf_ref in zip(cast_in, cast_out):
        w_bf_ref[...] = w_ref[...].astype(BF16)
    rows = q_ref.shape[1]
    nblk = rows // ATTN_BLOCK
    step = pl.program_id(1)
    qi = lax.broadcasted_iota(jnp.int32, (ATTN_BLOCK, 2 * ATTN_BLOCK), 0)
    kj = lax.broadcasted_iota(jnp.int32, (ATTN_BLOCK, 2 * ATTN_BLOCK), 1)
    dist = qi + ATTN_BLOCK - kj
    band = (dist >= 0) & (dist <= n_back)
    in_current = kj >= ATTN_BLOCK
    head_of_lane = lax.broadcasted_iota(jnp.int32, (1, GROUP_WIDTH), 1) // HEAD_DIM
    heads = range(HEADS_PER_GROUP)
    heads_per_vreg = LANES // HEAD_DIM

    def head_rows(h):
        return slice(h * ATTN_BLOCK, (h + 1) * ATTN_BLOCK)

    def keys_values(j, ref, halo_ref):
        if j == 0:
            return jnp.concatenate([halo_ref[0], ref[0, 0:ATTN_BLOCK, :]], axis=0)
        return ref[0, (j - 1) * ATTN_BLOCK:(j + 1) * ATTN_BLOCK, :]

    def scores(j):
        qb = q_ref[0, j * ATTN_BLOCK:(j + 1) * ATTN_BLOCK, :]
        qs = jnp.concatenate([jnp.where(head_of_lane == h, qb, jnp.zeros_like(qb)) for h in heads], axis=0)
        return lax.dot_general(qs, keys_values(j, k_ref, kh_ref), (((1,), (1,)), ((), ())),
                               preferred_element_type=F32)

    def lanes_by_head(per_head):
        first_in_vreg = (head_of_lane[:, :LANES] % heads_per_vreg) == 0
        vregs = [jnp.where(first_in_vreg, per_head[v * heads_per_vreg], per_head[v * heads_per_vreg + 1])
                 for v in range(GROUP_WIDTH // LANES)]
        return jnp.concatenate(vregs, axis=1)

    s_ref[0] = scores(0)
    for j in range(nblk):
        if j + 1 < nblk:
            s_ref[(j + 1) % 2] = scores(j + 1)
        has_prev = ((step * nblk + j) % blocks_per_residue) != 0
        allowed = band & (in_current | has_prev)
        inv_l, lse = [], []
        for h in heads:
            s = jnp.where(allowed, s_ref[j % 2, head_rows(h), :], -jnp.inf)
            m = jnp.max(s, axis=-1, keepdims=True)
            p = jnp.exp2(s - m)
            l = jnp.sum(p, axis=-1, keepdims=True)
            p_ref[head_rows(h), :] = p.astype(BF16)
            inv_l.append(1.0 / l)
            lse.append(m + jnp.log2(l))
        pv = jnp.dot(p_ref[...], keys_values(j, v_ref, vh_ref), preferred_element_type=F32)
        o_heads = [pv[head_rows(h), (h // heads_per_vreg) * LANES:(h // heads_per_vreg + 1) * LANES] * inv_l[h]
                   for h in heads]
        cur = slice(j * ATTN_BLOCK, (j + 1) * ATTN_BLOCK)
        o_ref[0, cur, :] = lanes_by_head(o_heads).astype(BF16)
        lse_ref[0, cur, :] = lanes_by_head([jnp.broadcast_to(x, (ATTN_BLOCK, LANES)) for x in lse])


def _merge_ffn2_kernel(h1_ref, o0_ref, o1_ref, o2_ref, l0_ref, l1_ref, l2_ref, ga_ref, gpool_ref,
                       wa_ref, wo_ref, g2_ref, w1_ref, w3_ref, w2_ref, gf_ref,
                       out_ref, hid_ref, nat_ref):
    tm = h1_ref.shape[0]

    def natural_order(ref, r, slot):
        if r == 1:
            return ref[0, 0].astype(F32)
        halves = []
        for half in range(GROUP_WIDTH // LANES):
            lanes = slice(half * LANES, (half + 1) * LANES)
            for c in range(r):
                nat_ref[slot + half, pl.ds(c, tm // r, stride=r), :] = ref[0, c, :, lanes].astype(F32)
            halves.append(nat_ref[slot + half])
        return jnp.concatenate(halves, axis=1)

    outs, lses = [], []
    slot = 0
    halves_per_slab = GROUP_WIDTH // LANES
    for (_, r), o_ref, l_ref in zip(DILATED_GROUPS, (o0_ref, o1_ref, o2_ref), (l0_ref, l1_ref, l2_ref)):
        outs.append(natural_order(o_ref, r, slot))
        lses.append(natural_order(l_ref, r, slot + halves_per_slab))
        if r != 1:
            slot += 2 * halves_per_slab
    top = functools.reduce(jnp.maximum, lses)
    weights = [jnp.exp2(l - top) for l in lses]
    o = sum(w * og for w, og in zip(weights, outs)) / sum(weights)
    y_attn = jnp.dot(o.astype(BF16), wa_ref[...], preferred_element_type=F32)

    merged = ga_ref[...].astype(F32) * y_attn + gpool_ref[...].astype(F32)
    h2 = h1_ref[...] + jnp.dot(merged.astype(BF16), wo_ref[...], preferred_element_type=F32)
    u2 = _rms(h2, g2_ref[...]).astype(BF16)
    h3 = h2 + 0.5 * _swiglu(u2, w1_ref, w3_ref, w2_ref, hid_ref)
    out_ref[...] = _rms(h3, gf_ref[...])


def _resident(shape):
    return pl.BlockSpec(shape, lambda *_: (0,) * len(shape), pipeline_mode=pl.Buffered(1))


def _rope_tables(seq):
    pos = jnp.arange(seq, dtype=F32)
    inv = ROPE_THETA ** (-jnp.arange(0, ROT_DIM, 2, dtype=F32) / ROT_DIM)
    ang = pos[:, None] * inv[None, :]
    cos, sin = jnp.cos(ang), jnp.sin(ang)
    rest = HEAD_DIM - ROT_DIM
    cos_h = jnp.concatenate([cos, cos, jnp.ones((seq, rest), F32)], axis=1)
    sin_h = jnp.concatenate([-sin, sin, jnp.zeros((seq, rest), F32)], axis=1)
    reps = LANES // HEAD_DIM
    return jnp.tile(cos_h, (1, reps)), jnp.tile(sin_h, (1, reps))


def _layer(h, B, S, ffn1_norm, ffn1_w1, ffn1_w3, ffn1_w2, mix_norm, w_in, w_branch_attn, w_branch_pool,
           pool_w, pool_scale, w_out, ffn2_norm, ffn2_w1, ffn2_w3, ffn2_w2, final_gain, cos_t, sin_t):
    T, D = h.shape
    d_ff = ffn1_w1.shape[1]
    in_width = w_in.shape[1]
    assert in_width == 3 * N_GROUPS * GROUP_WIDTH + POOL_WIDTH + 2 * D
    tm = TOKEN_TILE
    tiles_per_seq = S // tm
    n_tiles = T // tm
    bf = lambda w: w.astype(BF16)
    row = lambda g: g.reshape(1, -1).astype(F32)
    params = pltpu.CompilerParams(dimension_semantics=("arbitrary",), vmem_limit_bytes=VMEM_LIMIT)

    tok = lambda width: pl.BlockSpec((tm, width), lambda i: (i, 0))

    def residue_major(width, r):
        return pl.BlockSpec((1, r, tm // r, width), lambda i: (i // tiles_per_seq, 0, i % tiles_per_seq, 0))

    qkv_shapes = [jax.ShapeDtypeStruct((B, r, S // r, GROUP_WIDTH), BF16) for _, r in DILATED_GROUPS]
    qkv_specs = [residue_major(GROUP_WIDTH, r) for _, r in DILATED_GROUPS]
    outs = pl.pallas_call(
        functools.partial(_ffn1_proj_kernel, tiles_per_seq=tiles_per_seq),
        name="ffn1_proj",
        grid=(n_tiles,),
        in_specs=[tok(D), _resident((1, D)), _resident((D, d_ff)), _resident((D, d_ff)), _resident((d_ff, D)),
                  _resident((1, D)), _resident((D, in_width)),
                  pl.BlockSpec((tm, LANES), lambda i: (i % tiles_per_seq, 0)),
                  pl.BlockSpec((tm, LANES), lambda i: (i % tiles_per_seq, 0)),
                  _resident(pool_w.shape), _resident((1, POOL_WIDTH)), _resident(w_branch_pool.shape)],
        out_specs=[tok(D)] + qkv_specs * 3 + [tok(D), tok(D)],
        out_shape=[jax.ShapeDtypeStruct((T, D), F32)] + qkv_shapes * 3
        + [jax.ShapeDtypeStruct((T, D), BF16), jax.ShapeDtypeStruct((T, D), BF16)],
        scratch_shapes=[pltpu.VMEM((tm, d_ff), BF16),
                        pltpu.VMEM((3 * sum(_slots_per_half(r) for _, r in DILATED_GROUPS) * GROUP_WIDTH // LANES,
                                    tm, LANES), F32),
                        pltpu.VMEM((POOL_HALO + tm, POOL_WIDTH), F32)],
        compiler_params=params,
    )(h, row(ffn1_norm), bf(ffn1_w1), bf(ffn1_w3), bf(ffn1_w2), row(mix_norm), bf(w_in), cos_t, sin_t,
      bf(pool_w), row(pool_scale), bf(w_branch_pool))
    h1 = outs[0]
    qs, ks, vs = outs[1:4], outs[4:7], outs[7:10]
    gate_attn, gated_pool = outs[10], outs[11]

    attn_o, attn_lse = [], []
    rows = ATTN_ROWS
    blocks_per_step = rows // ATTN_BLOCK
    steps_per_batch = S // rows
    attn_steps = B * steps_per_batch
    cast_jobs = ((ffn2_w1, w_out), (ffn2_w3, w_branch_attn), (ffn2_w2,))
    cast_done = []
    for g, (window, r) in enumerate(DILATED_GROUPS):
        L = S // r
        flat = lambda a: a.reshape(B, S, GROUP_WIDTH)
        cur = pl.BlockSpec((1, rows, GROUP_WIDTH), lambda b, s: (b, s, 0))
        prev = pl.BlockSpec((1, ATTN_BLOCK, GROUP_WIDTH),
                            lambda b, s: (b, jnp.maximum(s * blocks_per_step - 1, 0), 0))
        weights = cast_jobs[g]
        assert all(w.shape[0] % (16 * attn_steps) == 0 for w in weights)
        w_specs = [pl.BlockSpec((w.shape[0] // attn_steps, w.shape[1]), lambda b, s: (b * steps_per_batch + s, 0))
                   for w in weights]
        o_g, lse_g, *w_bf = pl.pallas_call(
            functools.partial(_attn_kernel, blocks_per_residue=L // ATTN_BLOCK, n_back=window // r,
                              n_cast=len(weights)),
            name=f"dilated_attn_g{g}",
            grid=(B, steps_per_batch),
            in_specs=[cur, cur, cur, prev, prev] + w_specs,
            out_specs=[cur, cur] + w_specs,
            out_shape=[jax.ShapeDtypeStruct((B, S, GROUP_WIDTH), BF16),
                       jax.ShapeDtypeStruct((B, S, GROUP_WIDTH), F32)]
            + [jax.ShapeDtypeStruct(w.shape, BF16) for w in weights],
            scratch_shapes=[pltpu.VMEM((2, HEADS_PER_GROUP * ATTN_BLOCK, 2 * ATTN_BLOCK), F32),
                            pltpu.VMEM((HEADS_PER_GROUP * ATTN_BLOCK, 2 * ATTN_BLOCK), BF16)],
            compiler_params=pltpu.CompilerParams(dimension_semantics=("arbitrary", "arbitrary")),
        )(flat(qs[g]), flat(ks[g]), flat(vs[g]), flat(ks[g]), flat(vs[g]), *weights)
        attn_o.append(o_g.reshape(B, r, L, GROUP_WIDTH))
        attn_lse.append(lse_g.reshape(B, r, L, GROUP_WIDTH))
        cast_done.extend(w_bf)
    ffn2_w1_bf, w_out_bf, ffn2_w3_bf, w_branch_attn_bf, ffn2_w2_bf = cast_done

    o_specs = [residue_major(GROUP_WIDTH, r) for _, r in DILATED_GROUPS]
    return pl.pallas_call(
        _merge_ffn2_kernel,
        name="merge_ffn2",
        grid=(n_tiles,),
        in_specs=[tok(D)] + o_specs + o_specs + [tok(D), tok(D)]
        + [_resident(w_branch_attn.shape), _resident(w_out.shape), _resident((1, D)),
           _resident((D, d_ff)), _resident((D, d_ff)), _resident((d_ff, D)), _resident((1, D))],
        out_specs=tok(D),
        out_shape=jax.ShapeDtypeStruct((T, D), F32),
        scratch_shapes=[pltpu.VMEM((tm, d_ff), BF16),
                        pltpu.VMEM((2 * (N_GROUPS - 1) * GROUP_WIDTH // LANES, tm, LANES), F32)],
        compiler_params=params,
    )(h1, *attn_o, *attn_lse, gate_attn, gated_pool, w_branch_attn_bf, w_out_bf, row(ffn2_norm),
      ffn2_w1_bf, ffn2_w3_bf, ffn2_w2_bf, row(final_gain))


def kernel(x, ffn1_norm, ffn1_w1, ffn1_w3, ffn1_w2, mix_norm, w_in, w_branch_attn, w_branch_pool, pool_w,
           pool_scale, w_out, ffn2_norm, ffn2_w1, ffn2_w3, ffn2_w2, final_norm):
    B, S, D = x.shape
    depth = ffn1_norm.shape[0]
    assert S % TOKEN_TILE == 0 and S % ATTN_ROWS == 0
    assert all(window // r == ATTN_BLOCK and (S // r) % ATTN_BLOCK == 0 for window, r in DILATED_GROUPS)
    assert all(TOKEN_TILE % (16 * r) == 0 for _, r in DILATED_GROUPS)
    cos_t, sin_t = _rope_tables(S)
    h = x.reshape(B * S, D)
    for l in range(depth):
        assert l == depth - 1, "only the last layer's output norm is implemented"
        h = _layer(h, B, S, ffn1_norm[l], ffn1_w1[l], ffn1_w3[l], ffn1_w2[l], mix_norm[l], w_in[l],
                   w_branch_attn[l], w_branch_pool[l], pool_w[l], pool_scale[l], w_out[l], ffn2_norm[l],
                   ffn2_w1[l], ffn2_w3[l], ffn2_w2[l], final_norm, cos_t, sin_t)
    return h.reshape(B, S, D)
```

```python
import functools

import jax
import jax.numpy as jnp
from jax import lax
from jax.experimental import pallas as pl
from jax.experimental.pallas import tpu as pltpu

F32 = jnp.float32
BF16 = jnp.bfloat16

HEAD_DIM = 64
HEADS_PER_GROUP = 4
GROUP_WIDTH = HEADS_PER_GROUP * HEAD_DIM
DILATED_GROUPS = ((128, 1), (512, 4), (2048, 16))
N_GROUPS = len(DILATED_GROUPS)
ATTN_BLOCK = 128
ROT_DIM = HEAD_DIM // 4
ROPE_THETA = 500000.0
POOL_WINDOWS = (2, 4, 8, 16)
POOL_GROUP_WIDTH = 128
POOL_WIDTH = len(POOL_WINDOWS) * POOL_GROUP_WIDTH
POOL_HALO = max(POOL_WINDOWS)
RMS_EPS = 1e-6
SCORE_SCALE_LOG2 = HEAD_DIM ** -0.5 * 1.4426950408889634

LANES = 128
V7X_SCOPED_VMEM_MAX_BYTES = 60000 * 1024
COMPILER_TEMP_BYTES = 5 * 1024 * 1024

TOKEN_TILE = 512
FF_CHUNK = 256
ATTN_ROWS = 4096
DEINTERLEAVE_STEP = 4


def _rms(x, gain):
    return x * lax.rsqrt(jnp.mean(x * x, axis=-1, keepdims=True) + RMS_EPS) * gain


def _sigmoid(x):
    return 0.5 * jnp.tanh(0.5 * x) + 0.5


def _swiglu(u, w1_ref, w3_ref, w2_ref, hid_ref):
    d_ff = w1_ref.shape[1]
    for f in range(d_ff // FF_CHUNK):
        cols = slice(f * FF_CHUNK, (f + 1) * FF_CHUNK)
        a = jnp.dot(u, w1_ref[:, cols], preferred_element_type=F32)
        b = jnp.dot(u, w3_ref[:, cols], preferred_element_type=F32)
        half = 0.5 * a
        hid_ref[:, cols] = ((half * jnp.tanh(half) + half) * b).astype(BF16)
    return jnp.dot(hid_ref[...], w2_ref[...], preferred_element_type=F32)


def _slots_per_half(r):
    return 0 if r == 1 else 1 if r <= DEINTERLEAVE_STEP else 2


def _rope(x, cos, sin):
    j = lax.broadcasted_iota(jnp.int32, x.shape, 1) % HEAD_DIM
    half = ROT_DIM // 2
    partner = jnp.where(j < half, pltpu.roll(x, LANES - half, axis=1), pltpu.roll(x, half, axis=1))
    return jnp.where(j < ROT_DIM, x * cos + partner * sin, x)


def _pool_deltas(pext_ref, first_pos):
    tm = pext_ref.shape[0] - POOL_HALO
    pos = first_pos + lax.broadcasted_iota(jnp.int32, (tm, 1), 0)
    deltas = []
    for gi, win in enumerate(POOL_WINDOWS):
        lanes = slice(gi * POOL_GROUP_WIDTH, (gi + 1) * POOL_GROUP_WIDTH)
        own = pext_ref[POOL_HALO:POOL_HALO + tm, lanes]
        total = own
        for back in range(1, win):
            total = total + pext_ref[POOL_HALO - back:POOL_HALO - back + tm, lanes]
        count = jnp.minimum(pos + 1, win).astype(F32)
        deltas.append((total / count - own).astype(BF16))
    return deltas


def _pool_project(deltas, poolw_ref, pscale_ref, wb_ref):
    pooled = []
    for gi, d in enumerate(deltas):
        lanes = slice(gi * POOL_GROUP_WIDTH, (gi + 1) * POOL_GROUP_WIDTH)
        pooled.append(jnp.dot(d, poolw_ref[gi], preferred_element_type=F32) * pscale_ref[:, lanes])
    return jnp.dot(jnp.concatenate(pooled, axis=1).astype(BF16), wb_ref[...], preferred_element_type=F32)


def _ffn1_proj_kernel(x_ref, g1_ref, w1_ref, w3_ref, w2_ref, gm_ref, win_ref, cos_ref, sin_ref,
                      poolw_ref, pscale_ref, wb_ref,
                      h1_ref, q0_ref, q1_ref, q2_ref, k0_ref, k1_ref, k2_ref, v0_ref, v1_ref, v2_ref,
                      ga_ref, gpool_ref, hid_ref, slab_ref, pext_ref, *, tiles_per_seq):
    tm = x_ref.shape[0]
    d_model = x_ref.shape[1]
    t = pl.program_id(0) % tiles_per_seq

    @pl.when(t == 0)
    def _():
        pext_ref[0:POOL_HALO, :] = jnp.zeros((POOL_HALO, POOL_WIDTH), F32)

    x = x_ref[...]
    u = _rms(x, g1_ref[...]).astype(BF16)
    h1 = x + 0.5 * _swiglu(u, w1_ref, w3_ref, w2_ref, hid_ref)
    h1_ref[...] = h1
    um = _rms(h1, gm_ref[...]).astype(BF16)
    cos = cos_ref[...]
    sin = sin_ref[...]

    def project(col):
        return jnp.dot(um, win_ref[:, col:col + GROUP_WIDTH], preferred_element_type=F32)

    def rope(p):
        return jnp.concatenate([_rope(p[:, :LANES], cos, sin), _rope(p[:, LANES:], cos, sin)], axis=1)

    qkv_width = 3 * N_GROUPS * GROUP_WIDTH
    gate_col = qkv_width + POOL_WIDTH

    def qkv_slab(kind, g, ref, scratch_slot):
        r = DILATED_GROUPS[g][1]
        p = project(("q", "k", "v").index(kind) * N_GROUPS * GROUP_WIDTH + g * GROUP_WIDTH)
        if kind == "q":
            p = rope(p) * SCORE_SCALE_LOG2
        elif kind == "k":
            p = rope(p)
        if r == 1:
            ref[0, 0] = p.astype(BF16)
            return
        for half in range(GROUP_WIDTH // LANES):
            lanes = slice(half * LANES, (half + 1) * LANES)
            first = scratch_slot + half * _slots_per_half(r)
            slab_ref[first] = p[:, lanes]
            if _slots_per_half(r) == 1:
                for c in range(r):
                    ref[0, c, :, lanes] = slab_ref[first, pl.ds(c, tm // r, stride=r), :].astype(BF16)
                continue
            step, rest = DEINTERLEAVE_STEP, r // DEINTERLEAVE_STEP
            for c0 in range(step):
                slab_ref[first + 1, c0 * (tm // step):(c0 + 1) * (tm // step), :] = (
                    slab_ref[first, pl.ds(c0, tm // step, stride=step), :])
            for c0 in range(step):
                for c1 in range(rest):
                    ref[0, c1 * step + c0, :, lanes] = slab_ref[
                        first + 1, pl.ds(c0 * (tm // step) + c1, tm // r, stride=rest), :].astype(BF16)

    for s in range(POOL_WIDTH // GROUP_WIDTH):
        lanes = slice(s * GROUP_WIDTH, (s + 1) * GROUP_WIDTH)
        pext_ref[POOL_HALO:, lanes] = project(qkv_width + s * GROUP_WIDTH)
    deltas = _pool_deltas(pext_ref, t * tm)
    pext_ref[0:POOL_HALO, :] = pext_ref[tm:tm + POOL_HALO, :]
    for s in range(d_model // GROUP_WIDTH):
        lanes = slice(s * GROUP_WIDTH, (s + 1) * GROUP_WIDTH)
        ga_ref[:, lanes] = _sigmoid(project(gate_col + s * GROUP_WIDTH)).astype(BF16)
    halves = GROUP_WIDTH // LANES
    slot = 0
    for g, ref in enumerate((q0_ref, q1_ref, q2_ref)):
        qkv_slab("q", g, ref, slot)
        slot += halves * _slots_per_half(DILATED_GROUPS[g][1])
    y_pool = _pool_project(deltas, poolw_ref, pscale_ref, wb_ref)
    for s in range(d_model // GROUP_WIDTH):
        lanes = slice(s * GROUP_WIDTH, (s + 1) * GROUP_WIDTH)
        gate_pool = _sigmoid(project(gate_col + d_model + s * GROUP_WIDTH))
        gpool_ref[:, lanes] = (gate_pool * y_pool[:, lanes]).astype(BF16)
    for kind, refs in (("k", (k0_ref, k1_ref, k2_ref)), ("v", (v0_ref, v1_ref, v2_ref))):
        for g in reversed(range(N_GROUPS)):
            qkv_slab(kind, g, refs[g], slot)
            slot += halves * _slots_per_half(DILATED_GROUPS[g][1])


def _attn_kernel(q_ref, k_ref, v_ref, kh_ref, vh_ref, *rest, blocks_per_residue, n_back, n_cast):
    cast_in, (o_ref, lse_ref), cast_out = rest[:n_cast], rest[n_cast:n_cast + 2], rest[n_cast + 2:2 * n_cast + 2]
    s_ref, p_ref = rest[2 * n_cast + 2:]
    for w_ref, w_bf_ref in zip(cast_in, cast_out):
        w_bf_ref[...] = w_ref[...].astype(BF16)
    rows = q_ref.shape[1]
    nblk = rows // ATTN_BLOCK
    step = pl.program_id(1)
    qi = lax.broadcasted_iota(jnp.int32, (ATTN_BLOCK, 2 * ATTN_BLOCK), 0)
    kj = lax.broadcasted_iota(jnp.int32, (ATTN_BLOCK, 2 * ATTN_BLOCK), 1)
    dist = qi + ATTN_BLOCK - kj
    band = (dist >= 0) & (dist <= n_back)
    in_current = kj >= ATTN_BLOCK
    head_of_lane = lax.broadcasted_iota(jnp.int32, (1, GROUP_WIDTH), 1) // HEAD_DIM
    heads = range(HEADS_PER_GROUP)
    heads_per_vreg = LANES // HEAD_DIM

    def head_rows(h):
        return slice(h * ATTN_BLOCK, (h + 1) * ATTN_BLOCK)

    def keys_values(j, ref, halo_ref):
        if j == 0:
            return jnp.concatenate([halo_ref[0], ref[0, 0:ATTN_BLOCK, :]], axis=0)
        return ref[0, (j - 1) * ATTN_BLOCK:(j + 1) * ATTN_BLOCK, :]

    def scores(j):
        qb = q_ref[0, j * ATTN_BLOCK:(j + 1) * ATTN_BLOCK, :]
        qs = jnp.concatenate([jnp.where(head_of_lane == h, qb, jnp.zeros_like(qb)) for h in heads], axis=0)
        return lax.dot_general(qs, keys_values(j, k_ref, kh_ref), (((1,), (1,)), ((), ())),
                               preferred_element_type=F32)

    def lanes_by_head(per_head):
        first_in_vreg = (head_of_lane[:, :LANES] % heads_per_vreg) == 0
        vregs = [jnp.where(first_in_vreg, per_head[v * heads_per_vreg], per_head[v * heads_per_vreg + 1])
                 for v in range(GROUP_WIDTH // LANES)]
        return jnp.concatenate(vregs, axis=1)

    s_ref[0] = scores(0)
    for j in range(nblk):
        if j + 1 < nblk:
            s_ref[(j + 1) % 2] = scores(j + 1)
        has_prev = ((step * nblk + j) % blocks_per_residue) != 0
        allowed = band & (in_current | has_prev)
        inv_l, lse = [], []
        for h in heads:
            s = jnp.where(allowed, s_ref[j % 2, head_rows(h), :], -jnp.inf)
            m = jnp.max(s, axis=-1, keepdims=True)
            p = jnp.exp2(s - m)
            l = jnp.sum(p, axis=-1, keepdims=True)
            p_ref[head_rows(h), :] = p.astype(BF16)
            inv_l.append(1.0 / l)
            lse.append(m + jnp.log2(l))
        pv = jnp.dot(p_ref[...], keys_values(j, v_ref, vh_ref), preferred_element_type=F32)
        o_heads = [pv[head_rows(h), (h // heads_per_vreg) * LANES:(h // heads_per_vreg + 1) * LANES] * inv_l[h]
                   for h in heads]
        cur = slice(j * ATTN_BLOCK, (j + 1) * ATTN_BLOCK)
        o_ref[0, cur, :] = lanes_by_head(o_heads).astype(BF16)
        lse_ref[0, cur, :] = lanes_by_head([jnp.broadcast_to(x, (ATTN_BLOCK, LANES)) for x in lse])


def _merge_ffn2_kernel(h1_ref, o0_ref, o1_ref, o2_ref, l0_ref, l1_ref, l2_ref, ga_ref, gpool_ref,
                       wa_ref, wo_ref, g2_ref, w1_ref, w3_ref, w2_ref, gf_ref,
                       out_ref, hid_ref, nat_ref):
    tm = h1_ref.shape[0]

    def natural_order(ref, r, slot):
        if r == 1:
            return ref[0, 0].astype(F32)
        halves = []
        for half in range(GROUP_WIDTH // LANES):
            lanes = slice(half * LANES, (half + 1) * LANES)
            for c in range(r):
                nat_ref[slot + half, pl.ds(c, tm // r, stride=r), :] = ref[0, c, :, lanes].astype(F32)
            halves.append(nat_ref[slot + half])
        return jnp.concatenate(halves, axis=1)

    outs, lses = [], []
    slot = 0
    halves_per_slab = GROUP_WIDTH // LANES
    for (_, r), o_ref, l_ref in zip(DILATED_GROUPS, (o0_ref, o1_ref, o2_ref), (l0_ref, l1_ref, l2_ref)):
        outs.append(natural_order(o_ref, r, slot))
        lses.append(natural_order(l_ref, r, slot + halves_per_slab))
        if r != 1:
            slot += 2 * halves_per_slab
    top = functools.reduce(jnp.maximum, lses)
    weights = [jnp.exp2(l - top) for l in lses]
    o = sum(w * og for w, og in zip(weights, outs)) / sum(weights)
    y_attn = jnp.dot(o.astype(BF16), wa_ref[...], preferred_element_type=F32)

    merged = ga_ref[...].astype(F32) * y_attn + gpool_ref[...].astype(F32)
    h2 = h1_ref[...] + jnp.dot(merged.astype(BF16), wo_ref[...], preferred_element_type=F32)
    u2 = _rms(h2, g2_ref[...]).astype(BF16)
    h3 = h2 + 0.5 * _swiglu(u2, w1_ref, w3_ref, w2_ref, hid_ref)
    out_ref[...] = _rms(h3, gf_ref[...])


def _resident(shape):
    return pl.BlockSpec(shape, lambda *_: (0,) * len(shape), pipeline_mode=pl.Buffered(1))


def _pallas_call(body, name, grid, inputs, outputs, scratch):
    def window_bytes(aval, spec):
        buffers = 1 if isinstance(spec.pipeline_mode, pl.Buffered) and spec.pipeline_mode.buffer_count == 1 else 2
        elems = 1
        for d in spec.block_shape:
            elems *= d
        return buffers * elems * jnp.dtype(aval.dtype).itemsize

    request = COMPILER_TEMP_BYTES + sum(window_bytes(a, spec) for a, spec in (*inputs, *outputs))
    for buf in scratch:
        elems = 1
        for d in buf.shape:
            elems *= d
        request += elems * jnp.dtype(buf.dtype).itemsize
    assert request <= V7X_SCOPED_VMEM_MAX_BYTES, (name, request)
    return pl.pallas_call(
        body, name=name, grid=grid,
        in_specs=[spec for _, spec in inputs], out_specs=[spec for _, spec in outputs],
        out_shape=[shape for shape, _ in outputs], scratch_shapes=scratch,
        compiler_params=pltpu.CompilerParams(dimension_semantics=("arbitrary",) * len(grid),
                                             vmem_limit_bytes=request),
    )(*[a for a, _ in inputs])


def _rope_tables(seq):
    pos = jnp.arange(seq, dtype=F32)
    inv = ROPE_THETA ** (-jnp.arange(0, ROT_DIM, 2, dtype=F32) / ROT_DIM)
    ang = pos[:, None] * inv[None, :]
    cos, sin = jnp.cos(ang), jnp.sin(ang)
    rest = HEAD_DIM - ROT_DIM
    cos_h = jnp.concatenate([cos, cos, jnp.ones((seq, rest), F32)], axis=1)
    sin_h = jnp.concatenate([-sin, sin, jnp.zeros((seq, rest), F32)], axis=1)
    reps = LANES // HEAD_DIM
    return jnp.tile(cos_h, (1, reps)), jnp.tile(sin_h, (1, reps))


def _layer(h, B, S, ffn1_norm, ffn1_w1, ffn1_w3, ffn1_w2, mix_norm, w_in, w_branch_attn, w_branch_pool,
           pool_w, pool_scale, w_out, ffn2_norm, ffn2_w1, ffn2_w3, ffn2_w2, final_gain, cos_t, sin_t):
    T, D = h.shape
    d_ff = ffn1_w1.shape[1]
    in_width = w_in.shape[1]
    assert in_width == 3 * N_GROUPS * GROUP_WIDTH + POOL_WIDTH + 2 * D
    tm = TOKEN_TILE
    tiles_per_seq = S // tm
    n_tiles = T // tm
    bf = lambda w: w.astype(BF16)
    row = lambda g: g.reshape(1, -1).astype(F32)
    tok = lambda width: pl.BlockSpec((tm, width), lambda i: (i, 0))
    table = pl.BlockSpec((tm, LANES), lambda i: (i % tiles_per_seq, 0))
    whole = lambda a: (a, _resident(a.shape))

    def residue_major(r):
        return pl.BlockSpec((1, r, tm // r, GROUP_WIDTH), lambda i: (i // tiles_per_seq, 0, i % tiles_per_seq, 0))

    qkv_outputs = [(jax.ShapeDtypeStruct((B, r, S // r, GROUP_WIDTH), BF16), residue_major(r))
                   for _, r in DILATED_GROUPS] * 3
    slab_slots = 3 * sum(_slots_per_half(r) for _, r in DILATED_GROUPS) * GROUP_WIDTH // LANES
    outs = _pallas_call(
        functools.partial(_ffn1_proj_kernel, tiles_per_seq=tiles_per_seq), "ffn1_proj", (n_tiles,),
        inputs=[(h, tok(D)), whole(row(ffn1_norm)), whole(bf(ffn1_w1)), whole(bf(ffn1_w3)), whole(bf(ffn1_w2)),
                whole(row(mix_norm)), whole(bf(w_in)), (cos_t, table), (sin_t, table),
                whole(bf(pool_w)), whole(row(pool_scale)), whole(bf(w_branch_pool))],
        outputs=[(jax.ShapeDtypeStruct((T, D), F32), tok(D))] + qkv_outputs
        + [(jax.ShapeDtypeStruct((T, D), BF16), tok(D))] * 2,
        scratch=[pltpu.VMEM((tm, d_ff), BF16), pltpu.VMEM((slab_slots, tm, LANES), F32),
                 pltpu.VMEM((POOL_HALO + tm, POOL_WIDTH), F32)])
    h1 = outs[0]
    qs, ks, vs = outs[1:4], outs[4:7], outs[7:10]
    gate_attn, gated_pool = outs[10], outs[11]

    attn_o, attn_lse = [], []
    rows = ATTN_ROWS
    blocks_per_step = rows // ATTN_BLOCK
    steps_per_batch = S // rows
    attn_steps = B * steps_per_batch
    cast_jobs = ((ffn2_w1, w_out), (ffn2_w3, w_branch_attn), (ffn2_w2,))
    cast_done = []
    for g, (window, r) in enumerate(DILATED_GROUPS):
        L = S // r
        flat = lambda a: a.reshape(B, S, GROUP_WIDTH)
        cur = pl.BlockSpec((1, rows, GROUP_WIDTH), lambda b, s: (b, s, 0))
        prev = pl.BlockSpec((1, ATTN_BLOCK, GROUP_WIDTH),
                            lambda b, s: (b, jnp.maximum(s * blocks_per_step - 1, 0), 0))
        weights = cast_jobs[g]
        assert all(w.shape[0] % (16 * attn_steps) == 0 for w in weights)
        w_specs = [pl.BlockSpec((w.shape[0] // attn_steps, w.shape[1]), lambda b, s: (b * steps_per_batch + s, 0))
                   for w in weights]
        o_g, lse_g, *w_bf = _pallas_call(
            functools.partial(_attn_kernel, blocks_per_residue=L // ATTN_BLOCK, n_back=window // r,
                              n_cast=len(weights)),
            f"dilated_attn_g{g}", (B, steps_per_batch),
            inputs=[(flat(qs[g]), cur), (flat(ks[g]), cur), (flat(vs[g]), cur), (flat(ks[g]), prev),
                    (flat(vs[g]), prev)] + list(zip(weights, w_specs)),
            outputs=[(jax.ShapeDtypeStruct((B, S, GROUP_WIDTH), BF16), cur),
                     (jax.ShapeDtypeStruct((B, S, GROUP_WIDTH), F32), cur)]
            + [(jax.ShapeDtypeStruct(w.shape, BF16), spec) for w, spec in zip(weights, w_specs)],
            scratch=[pltpu.VMEM((2, HEADS_PER_GROUP * ATTN_BLOCK, 2 * ATTN_BLOCK), F32),
                     pltpu.VMEM((HEADS_PER_GROUP * ATTN_BLOCK, 2 * ATTN_BLOCK), BF16)])
        attn_o.append(o_g.reshape(B, r, L, GROUP_WIDTH))
        attn_lse.append(lse_g.reshape(B, r, L, GROUP_WIDTH))
        cast_done.extend(w_bf)
    ffn2_w1_bf, w_out_bf, ffn2_w3_bf, w_branch_attn_bf, ffn2_w2_bf = cast_done

    o_specs = [residue_major(r) for _, r in DILATED_GROUPS]
    (out,) = _pallas_call(
        _merge_ffn2_kernel, "merge_ffn2", (n_tiles,),
        inputs=[(h1, tok(D))] + list(zip(attn_o, o_specs)) + list(zip(attn_lse, o_specs))
        + [(gate_attn, tok(D)), (gated_pool, tok(D)), whole(w_branch_attn_bf), whole(w_out_bf),
           whole(row(ffn2_norm)), whole(ffn2_w1_bf), whole(ffn2_w3_bf), whole(ffn2_w2_bf), whole(row(final_gain))],
        outputs=[(jax.ShapeDtypeStruct((T, D), F32), tok(D))],
        scratch=[pltpu.VMEM((tm, d_ff), BF16),
                 pltpu.VMEM((2 * (N_GROUPS - 1) * GROUP_WIDTH // LANES, tm, LANES), F32)])
    return out


def kernel(x, ffn1_norm, ffn1_w1, ffn1_w3, ffn1_w2, mix_norm, w_in, w_branch_attn, w_branch_pool, pool_w,
           pool_scale, w_out, ffn2_norm, ffn2_w1, ffn2_w3, ffn2_w2, final_norm):
    B, S, D = x.shape
    depth = ffn1_norm.shape[0]
    assert S % TOKEN_TILE == 0 and S % ATTN_ROWS == 0
    assert all(window // r == ATTN_BLOCK and (S // r) % ATTN_BLOCK == 0 for window, r in DILATED_GROUPS)
    assert all(TOKEN_TILE % (16 * r) == 0 for _, r in DILATED_GROUPS)
    cos_t, sin_t = _rope_tables(S)
    h = x.reshape(B * S, D)
    for l in range(depth):
        assert l == depth - 1, "only the last layer's output norm is implemented"
        h = _layer(h, B, S, ffn1_norm[l], ffn1_w1[l], ffn1_w3[l], ffn1_w2[l], mix_norm[l], w_in[l],
                   w_branch_attn[l], w_branch_pool[l], pool_w[l], pool_scale[l], w_out[l], ffn2_norm[l],
                   ffn2_w1[l], ffn2_w3[l], ffn2_w2[l], final_norm, cos_t, sin_t)
    return h.reshape(B, S, D)
```

```python
import functools

import jax
import jax.numpy as jnp
from jax import lax
from jax.experimental import pallas as pl
from jax.experimental.pallas import tpu as pltpu

F32 = jnp.float32
BF16 = jnp.bfloat16

HEAD_DIM = 64
HEADS_PER_GROUP = 4
GROUP_WIDTH = HEADS_PER_GROUP * HEAD_DIM
DILATED_GROUPS = ((128, 1), (512, 4), (2048, 16))
N_GROUPS = len(DILATED_GROUPS)
ATTN_BLOCK = 128
ROT_DIM = HEAD_DIM // 4
ROPE_THETA = 500000.0
POOL_WINDOWS = (2, 4, 8, 16)
POOL_GROUP_WIDTH = 128
POOL_WIDTH = len(POOL_WINDOWS) * POOL_GROUP_WIDTH
POOL_HALO = max(POOL_WINDOWS)
RMS_EPS = 1e-6
SCORE_SCALE_LOG2 = HEAD_DIM ** -0.5 * 1.4426950408889634

LANES = 128
V7X_SCOPED_VMEM_MAX_BYTES = 60000 * 1024
COMPILER_TEMP_BYTES = 5 * 1024 * 1024

TOKEN_TILE = 512
FF_CHUNK = 256
ATTN_ROWS = 4096
DEINTERLEAVE_STEP = 4


def _rms(x, gain):
    return x * lax.rsqrt(jnp.mean(x * x, axis=-1, keepdims=True) + RMS_EPS) * gain


def _sigmoid(x):
    return 0.5 * jnp.tanh(0.5 * x) + 0.5


def _swiglu(u, w1_ref, w3_ref, w2_ref, hid_ref):
    d_ff = w1_ref.shape[1]
    for f in range(d_ff // FF_CHUNK):
        cols = slice(f * FF_CHUNK, (f + 1) * FF_CHUNK)
        a = jnp.dot(u, w1_ref[:, cols], preferred_element_type=F32)
        b = jnp.dot(u, w3_ref[:, cols], preferred_element_type=F32)
        half = 0.5 * a
        hid_ref[:, cols] = ((half * jnp.tanh(half) + half) * b).astype(BF16)
    return jnp.dot(hid_ref[...], w2_ref[...], preferred_element_type=F32)


def _slots_per_half(r):
    return 0 if r == 1 else 1 if r <= DEINTERLEAVE_STEP else 2


def _rope(x, cos, sin):
    j = lax.broadcasted_iota(jnp.int32, x.shape, 1) % HEAD_DIM
    half = ROT_DIM // 2
    partner = jnp.where(j < half, pltpu.roll(x, LANES - half, axis=1), pltpu.roll(x, half, axis=1))
    return jnp.where(j < ROT_DIM, x * cos + partner * sin, x)


def _pool_deltas(pext_ref, first_pos):
    tm = pext_ref.shape[0] - POOL_HALO
    pos = first_pos + lax.broadcasted_iota(jnp.int32, (tm, 1), 0)
    deltas = []
    for gi, win in enumerate(POOL_WINDOWS):
        lanes = slice(gi * POOL_GROUP_WIDTH, (gi + 1) * POOL_GROUP_WIDTH)
        own = pext_ref[POOL_HALO:POOL_HALO + tm, lanes]
        total = own
        for back in range(1, win):
            total = total + pext_ref[POOL_HALO - back:POOL_HALO - back + tm, lanes]
        count = jnp.minimum(pos + 1, win).astype(F32)
        deltas.append((total / count - own).astype(BF16))
    return deltas


def _pool_project(deltas, poolw_ref, pscale_ref, wb_ref):
    pooled = []
    for gi, d in enumerate(deltas):
        lanes = slice(gi * POOL_GROUP_WIDTH, (gi + 1) * POOL_GROUP_WIDTH)
        pooled.append(jnp.dot(d, poolw_ref[gi], preferred_element_type=F32) * pscale_ref[:, lanes])
    return jnp.dot(jnp.concatenate(pooled, axis=1).astype(BF16), wb_ref[...], preferred_element_type=F32)


def _ffn1_proj_kernel(x_ref, g1_ref, w1_ref, w3_ref, w2_ref, gm_ref, win_ref, cos_ref, sin_ref,
                      poolw_ref, pscale_ref, wb_ref,
                      h1_ref, q0_ref, q1_ref, q2_ref, k0_ref, k1_ref, k2_ref, v0_ref, v1_ref, v2_ref,
                      ga_ref, gpool_ref, hid_ref, slab_ref, pext_ref, *, tiles_per_seq):
    tm = x_ref.shape[0]
    d_model = x_ref.shape[1]
    t = pl.program_id(0) % tiles_per_seq

    @pl.when(t == 0)
    def _():
        pext_ref[0:POOL_HALO, :] = jnp.zeros((POOL_HALO, POOL_WIDTH), F32)

    x = x_ref[...]
    u = _rms(x, g1_ref[...]).astype(BF16)
    h1 = x + 0.5 * _swiglu(u, w1_ref, w3_ref, w2_ref, hid_ref)
    h1_ref[...] = h1
    um = _rms(h1, gm_ref[...]).astype(BF16)
    cos = cos_ref[...]
    sin = sin_ref[...]

    def project(col):
        return jnp.dot(um, win_ref[:, col:col + GROUP_WIDTH], preferred_element_type=F32)

    def rope(p):
        return jnp.concatenate([_rope(p[:, :LANES], cos, sin), _rope(p[:, LANES:], cos, sin)], axis=1)

    qkv_width = 3 * N_GROUPS * GROUP_WIDTH
    gate_col = qkv_width + POOL_WIDTH

    def qkv_slab(kind, g, ref, scratch_slot):
        r = DILATED_GROUPS[g][1]
        p = project(("q", "k", "v").index(kind) * N_GROUPS * GROUP_WIDTH + g * GROUP_WIDTH)
        if kind == "q":
            p = rope(p) * SCORE_SCALE_LOG2
        elif kind == "k":
            p = rope(p)
        if r == 1:
            ref[0, 0] = p.astype(BF16)
            return
        for half in range(GROUP_WIDTH // LANES):
            lanes = slice(half * LANES, (half + 1) * LANES)
            first = scratch_slot + half * _slots_per_half(r)
            slab_ref[first] = p[:, lanes]
            if _slots_per_half(r) == 1:
                for c in range(r):
                    ref[0, c, :, lanes] = slab_ref[first, pl.ds(c, tm // r, stride=r), :].astype(BF16)
                continue
            step, rest = DEINTERLEAVE_STEP, r // DEINTERLEAVE_STEP
            for c0 in range(step):
                slab_ref[first + 1, c0 * (tm // step):(c0 + 1) * (tm // step), :] = (
                    slab_ref[first, pl.ds(c0, tm // step, stride=step), :])
            for c0 in range(step):
                for c1 in range(rest):
                    ref[0, c1 * step + c0, :, lanes] = slab_ref[
                        first + 1, pl.ds(c0 * (tm // step) + c1, tm // r, stride=rest), :].astype(BF16)

    for s in range(POOL_WIDTH // GROUP_WIDTH):
        lanes = slice(s * GROUP_WIDTH, (s + 1) * GROUP_WIDTH)
        pext_ref[POOL_HALO:, lanes] = project(qkv_width + s * GROUP_WIDTH)
    deltas = _pool_deltas(pext_ref, t * tm)
    pext_ref[0:POOL_HALO, :] = pext_ref[tm:tm + POOL_HALO, :]
    for s in range(d_model // GROUP_WIDTH):
        lanes = slice(s * GROUP_WIDTH, (s + 1) * GROUP_WIDTH)
        ga_ref[:, lanes] = _sigmoid(project(gate_col + s * GROUP_WIDTH)).astype(BF16)
    halves = GROUP_WIDTH // LANES
    slot = 0
    for g, ref in enumerate((q0_ref, q1_ref, q2_ref)):
        qkv_slab("q", g, ref, slot)
        slot += halves * _slots_per_half(DILATED_GROUPS[g][1])
    y_pool = _pool_project(deltas, poolw_ref, pscale_ref, wb_ref)
    for s in range(d_model // GROUP_WIDTH):
        lanes = slice(s * GROUP_WIDTH, (s + 1) * GROUP_WIDTH)
        gate_pool = _sigmoid(project(gate_col + d_model + s * GROUP_WIDTH))
        gpool_ref[:, lanes] = (gate_pool * y_pool[:, lanes]).astype(BF16)
    for kind, refs in (("k", (k0_ref, k1_ref, k2_ref)), ("v", (v0_ref, v1_ref, v2_ref))):
        for g in reversed(range(N_GROUPS)):
            qkv_slab(kind, g, refs[g], slot)
            slot += halves * _slots_per_half(DILATED_GROUPS[g][1])


def _attn_kernel(q_ref, k_ref, v_ref, kh_ref, vh_ref, *rest, blocks_per_residue, n_back, n_cast):
    cast_in, (o_ref, lse_ref), cast_out = rest[:n_cast], rest[n_cast:n_cast + 2], rest[n_cast + 2:2 * n_cast + 2]
    s_ref, p_ref = rest[2 * n_cast + 2:]
    for w_ref, w_bf_ref in zip(cast_in, cast_out):
        w_bf_ref[...] = w_ref[...].astype(BF16)
    rows = q_ref.shape[1]
    nblk = rows // ATTN_BLOCK
    step = pl.program_id(1)
    qi = lax.broadcasted_iota(jnp.int32, (ATTN_BLOCK, 2 * ATTN_BLOCK), 0)
    kj = lax.broadcasted_iota(jnp.int32, (ATTN_BLOCK, 2 * ATTN_BLOCK), 1)
    dist = qi + ATTN_BLOCK - kj
    band = (dist >= 0) & (dist <= n_back)
    in_current = kj >= ATTN_BLOCK
    head_of_lane = lax.broadcasted_iota(jnp.int32, (1, GROUP_WIDTH), 1) // HEAD_DIM
    heads = range(HEADS_PER_GROUP)
    heads_per_vreg = LANES // HEAD_DIM

    def head_rows(h):
        return slice(h * ATTN_BLOCK, (h + 1) * ATTN_BLOCK)

    def keys_values(j, ref, halo_ref):
        if j == 0:
            return jnp.concatenate([halo_ref[0], ref[0, 0:ATTN_BLOCK, :]], axis=0)
        return ref[0, (j - 1) * ATTN_BLOCK:(j + 1) * ATTN_BLOCK, :]

    def scores(j):
        qb = q_ref[0, j * ATTN_BLOCK:(j + 1) * ATTN_BLOCK, :]
        qs = jnp.concatenate([jnp.where(head_of_lane == h, qb, jnp.zeros_like(qb)) for h in heads], axis=0)
        return lax.dot_general(qs, keys_values(j, k_ref, kh_ref), (((1,), (1,)), ((), ())),
                               preferred_element_type=F32)

    def lanes_by_head(per_head):
        first_in_vreg = (head_of_lane[:, :LANES] % heads_per_vreg) == 0
        vregs = [jnp.where(first_in_vreg, per_head[v * heads_per_vreg], per_head[v * heads_per_vreg + 1])
                 for v in range(GROUP_WIDTH // LANES)]
        return jnp.concatenate(vregs, axis=1)

    s_ref[0] = scores(0)
    for j in range(nblk):
        if j + 1 < nblk:
            s_ref[(j + 1) % 2] = scores(j + 1)
        has_prev = ((step * nblk + j) % blocks_per_residue) != 0
        allowed = band & (in_current | has_prev)
        inv_l, lse = [], []
        for h in heads:
            s = jnp.where(allowed, s_ref[j % 2, head_rows(h), :], -jnp.inf)
            m = jnp.max(s, axis=-1, keepdims=True)
            p = jnp.exp2(s - m)
            l = jnp.sum(p, axis=-1, keepdims=True)
            p_ref[head_rows(h), :] = p.astype(BF16)
            inv_l.append(1.0 / l)
            lse.append(m + jnp.log2(l))
        pv = jnp.dot(p_ref[...], keys_values(j, v_ref, vh_ref), preferred_element_type=F32)
        o_heads = [pv[head_rows(h), (h // heads_per_vreg) * LANES:(h // heads_per_vreg + 1) * LANES] * inv_l[h]
                   for h in heads]
        cur = slice(j * ATTN_BLOCK, (j + 1) * ATTN_BLOCK)
        o_ref[0, cur, :] = lanes_by_head(o_heads).astype(BF16)
        lse_ref[0, cur, :] = lanes_by_head([jnp.broadcast_to(x, (ATTN_BLOCK, LANES)) for x in lse])


def _merge_ffn2_kernel(h1_ref, o0_ref, o1_ref, o2_ref, l0_ref, l1_ref, l2_ref, ga_ref, gpool_ref,
                       wa_ref, wo_ref, g2_ref, w1_ref, w3_ref, w2_ref, gf_ref,
                       out_ref, hid_ref, nat_ref):
    tm = h1_ref.shape[0]

    def natural_order(ref, r, slot):
        if r == 1:
            return ref[0, 0].astype(F32)
        halves = []
        for half in range(GROUP_WIDTH // LANES):
            lanes = slice(half * LANES, (half + 1) * LANES)
            for c in range(r):
                nat_ref[slot + half, pl.ds(c, tm // r, stride=r), :] = ref[0, c, :, lanes].astype(F32)
            halves.append(nat_ref[slot + half])
        return jnp.concatenate(halves, axis=1)

    outs, lses = [], []
    slot = 0
    halves_per_slab = GROUP_WIDTH // LANES
    for (_, r), o_ref, l_ref in zip(DILATED_GROUPS, (o0_ref, o1_ref, o2_ref), (l0_ref, l1_ref, l2_ref)):
        outs.append(natural_order(o_ref, r, slot))
        lses.append(natural_order(l_ref, r, slot + halves_per_slab))
        if r != 1:
            slot += 2 * halves_per_slab
    top = functools.reduce(jnp.maximum, lses)
    weights = [jnp.exp2(l - top) for l in lses]
    o = sum(w * og for w, og in zip(weights, outs)) / sum(weights)
    y_attn = jnp.dot(o.astype(BF16), wa_ref[...], preferred_element_type=F32)

    merged = ga_ref[...].astype(F32) * y_attn + gpool_ref[...].astype(F32)
    h2 = h1_ref[...] + jnp.dot(merged.astype(BF16), wo_ref[...], preferred_element_type=F32)
    u2 = _rms(h2, g2_ref[...]).astype(BF16)
    h3 = h2 + 0.5 * _swiglu(u2, w1_ref, w3_ref, w2_ref, hid_ref)
    out_ref[...] = _rms(h3, gf_ref[...])


def _resident(shape):
    return pl.BlockSpec(shape, lambda *_: (0,) * len(shape), pipeline_mode=pl.Buffered(1))


def _pallas_call(body, name, grid, inputs, outputs, scratch):
    def window_bytes(aval, spec):
        buffers = 1 if isinstance(spec.pipeline_mode, pl.Buffered) and spec.pipeline_mode.buffer_count == 1 else 2
        elems = 1
        for d in spec.block_shape:
            elems *= d
        return buffers * elems * jnp.dtype(aval.dtype).itemsize

    request = COMPILER_TEMP_BYTES + sum(window_bytes(a, spec) for a, spec in (*inputs, *outputs))
    for buf in scratch:
        elems = 1
        for d in buf.shape:
            elems *= d
        request += elems * jnp.dtype(buf.dtype).itemsize
    assert request <= V7X_SCOPED_VMEM_MAX_BYTES, (name, request)
    return pl.pallas_call(
        body, name=name, grid=grid,
        in_specs=[spec for _, spec in inputs], out_specs=[spec for _, spec in outputs],
        out_shape=[shape for shape, _ in outputs], scratch_shapes=scratch,
        compiler_params=pltpu.CompilerParams(dimension_semantics=("arbitrary",) * len(grid),
                                             vmem_limit_bytes=V7X_SCOPED_VMEM_MAX_BYTES),
    )(*[a for a, _ in inputs])


def _rope_tables(seq):
    half = ROT_DIM // 2
    j = jnp.arange(LANES) % HEAD_DIM
    inv = jnp.where(j < ROT_DIM, ROPE_THETA ** (-(2 * (j % half)).astype(F32) / ROT_DIM), 0.0)
    sign = jnp.where(j < half, -1.0, 1.0).astype(F32)
    ang = jnp.arange(seq, dtype=F32)[:, None] * inv[None, :]
    return jnp.cos(ang), jnp.sin(ang) * sign[None, :]


def _layer(h, B, S, ffn1_norm, ffn1_w1, ffn1_w3, ffn1_w2, mix_norm, w_in, w_branch_attn, w_branch_pool,
           pool_w, pool_scale, w_out, ffn2_norm, ffn2_w1, ffn2_w3, ffn2_w2, final_gain, cos_t, sin_t):
    T, D = h.shape
    d_ff = ffn1_w1.shape[1]
    in_width = w_in.shape[1]
    assert in_width == 3 * N_GROUPS * GROUP_WIDTH + POOL_WIDTH + 2 * D
    tm = TOKEN_TILE
    tiles_per_seq = S // tm
    n_tiles = T // tm
    bf = lambda w: w.astype(BF16)
    row = lambda g: g.reshape(1, -1).astype(F32)
    tok = lambda width: pl.BlockSpec((tm, width), lambda i: (i, 0))
    table = pl.BlockSpec((tm, LANES), lambda i: (i % tiles_per_seq, 0))
    whole = lambda a: (a, _resident(a.shape))

    def residue_major(r):
        return pl.BlockSpec((1, r, tm // r, GROUP_WIDTH), lambda i: (i // tiles_per_seq, 0, i % tiles_per_seq, 0))

    qkv_outputs = [(jax.ShapeDtypeStruct((B, r, S // r, GROUP_WIDTH), BF16), residue_major(r))
                   for _, r in DILATED_GROUPS] * 3
    slab_slots = 3 * sum(_slots_per_half(r) for _, r in DILATED_GROUPS) * GROUP_WIDTH // LANES
    outs = _pallas_call(
        functools.partial(_ffn1_proj_kernel, tiles_per_seq=tiles_per_seq), "ffn1_proj", (n_tiles,),
        inputs=[(h, tok(D)), whole(row(ffn1_norm)), whole(bf(ffn1_w1)), whole(bf(ffn1_w3)), whole(bf(ffn1_w2)),
                whole(row(mix_norm)), whole(bf(w_in)), (cos_t, table), (sin_t, table),
                whole(bf(pool_w)), whole(row(pool_scale)), whole(bf(w_branch_pool))],
        outputs=[(jax.ShapeDtypeStruct((T, D), F32), tok(D))] + qkv_outputs
        + [(jax.ShapeDtypeStruct((T, D), BF16), tok(D))] * 2,
        scratch=[pltpu.VMEM((tm, d_ff), BF16), pltpu.VMEM((slab_slots, tm, LANES), F32),
                 pltpu.VMEM((POOL_HALO + tm, POOL_WIDTH), F32)])
    h1 = outs[0]
    qs, ks, vs = outs[1:4], outs[4:7], outs[7:10]
    gate_attn, gated_pool = outs[10], outs[11]

    attn_o, attn_lse = [], []
    rows = ATTN_ROWS
    blocks_per_step = rows // ATTN_BLOCK
    steps_per_batch = S // rows
    attn_steps = B * steps_per_batch
    cast_jobs = ((ffn2_w1, w_out), (ffn2_w3, w_branch_attn), (ffn2_w2,))
    cast_done = []
    for g, (window, r) in enumerate(DILATED_GROUPS):
        L = S // r
        flat = lambda a: a.reshape(B, S, GROUP_WIDTH)
        cur = pl.BlockSpec((1, rows, GROUP_WIDTH), lambda b, s: (b, s, 0))
        prev = pl.BlockSpec((1, ATTN_BLOCK, GROUP_WIDTH),
                            lambda b, s: (b, jnp.maximum(s * blocks_per_step - 1, 0), 0))
        weights = cast_jobs[g]
        assert all(w.shape[0] % (16 * attn_steps) == 0 for w in weights)
        w_specs = [pl.BlockSpec((w.shape[0] // attn_steps, w.shape[1]), lambda b, s: (b * steps_per_batch + s, 0))
                   for w in weights]
        o_g, lse_g, *w_bf = _pallas_call(
            functools.partial(_attn_kernel, blocks_per_residue=L // ATTN_BLOCK, n_back=window // r,
                              n_cast=len(weights)),
            f"dilated_attn_g{g}", (B, steps_per_batch),
            inputs=[(flat(qs[g]), cur), (flat(ks[g]), cur), (flat(vs[g]), cur), (flat(ks[g]), prev),
                    (flat(vs[g]), prev)] + list(zip(weights, w_specs)),
            outputs=[(jax.ShapeDtypeStruct((B, S, GROUP_WIDTH), BF16), cur),
                     (jax.ShapeDtypeStruct((B, S, GROUP_WIDTH), F32), cur)]
            + [(jax.ShapeDtypeStruct(w.shape, BF16), spec) for w, spec in zip(weights, w_specs)],
            scratch=[pltpu.VMEM((2, HEADS_PER_GROUP * ATTN_BLOCK, 2 * ATTN_BLOCK), F32),
                     pltpu.VMEM((HEADS_PER_GROUP * ATTN_BLOCK, 2 * ATTN_BLOCK), BF16)])
        attn_o.append(o_g.reshape(B, r, L, GROUP_WIDTH))
        attn_lse.append(lse_g.reshape(B, r, L, GROUP_WIDTH))
        cast_done.extend(w_bf)
    ffn2_w1_bf, w_out_bf, ffn2_w3_bf, w_branch_attn_bf, ffn2_w2_bf = cast_done

    o_specs = [residue_major(r) for _, r in DILATED_GROUPS]
    (out,) = _pallas_call(
        _merge_ffn2_kernel, "merge_ffn2", (n_tiles,),
        inputs=[(h1, tok(D))] + list(zip(attn_o, o_specs)) + list(zip(attn_lse, o_specs))
        + [(gate_attn, tok(D)), (gated_pool, tok(D)), whole(w_branch_attn_bf), whole(w_out_bf),
           whole(row(ffn2_norm)), whole(ffn2_w1_bf), whole(ffn2_w3_bf), whole(ffn2_w2_bf), whole(row(final_gain))],
        outputs=[(jax.ShapeDtypeStruct((T, D), F32), tok(D))],
        scratch=[pltpu.VMEM((tm, d_ff), BF16),
                 pltpu.VMEM((2 * (N_GROUPS - 1) * GROUP_WIDTH // LANES, tm, LANES), F32)])
    return out


def kernel(x, ffn1_norm, ffn1_w1, ffn1_w3, ffn1_w2, mix_norm, w_in, w_branch_attn, w_branch_pool, pool_w,
           pool_scale, w_out, ffn2_norm, ffn2_w1, ffn2_w3, ffn2_w2, final_norm):
    B, S, D = x.shape
    depth = ffn1_norm.shape[0]
    assert S % TOKEN_TILE == 0 and S % ATTN_ROWS == 0
    assert all(window // r == ATTN_BLOCK and (S // r) % ATTN_BLOCK == 0 for window, r in DILATED_GROUPS)
    assert all(TOKEN_TILE % (16 * r) == 0 for _, r in DILATED_GROUPS)
    cos_t, sin_t = _rope_tables(S)
    h = x.reshape(B * S, D)
    for l in range(depth):
        assert l == depth - 1, "only the last layer's output norm is implemented"
        h = _layer(h, B, S, ffn1_norm[l], ffn1_w1[l], ffn1_w3[l], ffn1_w2[l], mix_norm[l], w_in[l],
                   w_branch_attn[l], w_branch_pool[l], pool_w[l], pool_scale[l], w_out[l], ffn2_norm[l],
                   ffn2_w1[l], ffn2_w3[l], ffn2_w2[l], final_norm, cos_t, sin_t)
    return h.reshape(B, S, D)
```

```python
import functools

import jax
import jax.numpy as jnp
import numpy as np
from jax import lax
from jax.experimental import pallas as pl
from jax.experimental.pallas import tpu as pltpu

F32 = jnp.float32
BF16 = jnp.bfloat16

HEAD_DIM = 64
HEADS_PER_GROUP = 4
GROUP_WIDTH = HEADS_PER_GROUP * HEAD_DIM
DILATED_GROUPS = ((128, 1), (512, 4), (2048, 16))
N_GROUPS = len(DILATED_GROUPS)
ATTN_BLOCK = 128
ROT_DIM = HEAD_DIM // 4
ROPE_THETA = 500000.0
POOL_WINDOWS = (2, 4, 8, 16)
POOL_GROUP_WIDTH = 128
POOL_WIDTH = len(POOL_WINDOWS) * POOL_GROUP_WIDTH
POOL_HALO = max(POOL_WINDOWS)
RMS_EPS = 1e-6
SCORE_SCALE_LOG2 = HEAD_DIM ** -0.5 * 1.4426950408889634

LANES = 128
V7X_SCOPED_VMEM_MAX_BYTES = 60000 * 1024
COMPILER_TEMP_BYTES = 5 * 1024 * 1024

TOKEN_TILE = 512
FF_CHUNK = 256
ATTN_ROWS = 4096
DEINTERLEAVE_STEP = 4


def _rms(x, gain):
    return x * lax.rsqrt(jnp.mean(x * x, axis=-1, keepdims=True) + RMS_EPS) * gain


def _sigmoid(x):
    return 0.5 * jnp.tanh(0.5 * x) + 0.5


def _swiglu(u, w1_ref, w3_ref, w2_ref, hid_ref):
    d_ff = w1_ref.shape[1]
    for f in range(d_ff // FF_CHUNK):
        cols = slice(f * FF_CHUNK, (f + 1) * FF_CHUNK)
        a = jnp.dot(u, w1_ref[:, cols], preferred_element_type=F32)
        b = jnp.dot(u, w3_ref[:, cols], preferred_element_type=F32)
        half = 0.5 * a
        hid_ref[:, cols] = ((half * jnp.tanh(half) + half) * b).astype(BF16)
    return jnp.dot(hid_ref[...], w2_ref[...], preferred_element_type=F32)


def _slots_per_half(r):
    return 0 if r == 1 else 1 if r <= DEINTERLEAVE_STEP else 2


def _rope(x, cos, sin):
    j = lax.broadcasted_iota(jnp.int32, x.shape, 1) % HEAD_DIM
    half = ROT_DIM // 2
    partner = jnp.where(j < half, pltpu.roll(x, LANES - half, axis=1), pltpu.roll(x, half, axis=1))
    return jnp.where(j < ROT_DIM, x * cos + partner * sin, x)


def _pool_deltas(pext_ref, first_pos):
    tm = pext_ref.shape[0] - POOL_HALO
    pos = first_pos + lax.broadcasted_iota(jnp.int32, (tm, 1), 0)
    deltas = []
    for gi, win in enumerate(POOL_WINDOWS):
        lanes = slice(gi * POOL_GROUP_WIDTH, (gi + 1) * POOL_GROUP_WIDTH)
        own = pext_ref[POOL_HALO:POOL_HALO + tm, lanes]
        total = own
        for back in range(1, win):
            total = total + pext_ref[POOL_HALO - back:POOL_HALO - back + tm, lanes]
        count = jnp.minimum(pos + 1, win).astype(F32)
        deltas.append((total / count - own).astype(BF16))
    return deltas


def _pool_project(deltas, poolw_ref, pscale_ref, wb_ref):
    pooled = []
    for gi, d in enumerate(deltas):
        lanes = slice(gi * POOL_GROUP_WIDTH, (gi + 1) * POOL_GROUP_WIDTH)
        pooled.append(jnp.dot(d, poolw_ref[gi], preferred_element_type=F32) * pscale_ref[:, lanes])
    return jnp.dot(jnp.concatenate(pooled, axis=1).astype(BF16), wb_ref[...], preferred_element_type=F32)


def _ffn1_proj_kernel(x_ref, g1_ref, w1_ref, w3_ref, w2_ref, gm_ref, win_ref, cos_ref, sin_ref,
                      poolw_ref, pscale_ref, wb_ref,
                      h1_ref, q0_ref, q1_ref, q2_ref, k0_ref, k1_ref, k2_ref, v0_ref, v1_ref, v2_ref,
                      ga_ref, gpool_ref, hid_ref, slab_ref, pext_ref, *, tiles_per_seq):
    tm = x_ref.shape[0]
    d_model = x_ref.shape[1]
    t = pl.program_id(0) % tiles_per_seq

    @pl.when(t == 0)
    def _():
        pext_ref[0:POOL_HALO, :] = jnp.zeros((POOL_HALO, POOL_WIDTH), F32)

    x = x_ref[...]
    u = _rms(x, g1_ref[...]).astype(BF16)
    h1 = x + 0.5 * _swiglu(u, w1_ref, w3_ref, w2_ref, hid_ref)
    h1_ref[...] = h1
    um = _rms(h1, gm_ref[...]).astype(BF16)
    cos = cos_ref[...]
    sin = sin_ref[...]

    def project(col):
        return jnp.dot(um, win_ref[:, col:col + GROUP_WIDTH], preferred_element_type=F32)

    def rope(p):
        return jnp.concatenate([_rope(p[:, :LANES], cos, sin), _rope(p[:, LANES:], cos, sin)], axis=1)

    qkv_width = 3 * N_GROUPS * GROUP_WIDTH
    gate_col = qkv_width + POOL_WIDTH

    def qkv_slab(kind, g, ref, scratch_slot):
        r = DILATED_GROUPS[g][1]
        p = project(("q", "k", "v").index(kind) * N_GROUPS * GROUP_WIDTH + g * GROUP_WIDTH)
        if kind == "q":
            p = rope(p) * SCORE_SCALE_LOG2
        elif kind == "k":
            p = rope(p)
        if r == 1:
            ref[0, 0] = p.astype(BF16)
            return
        for half in range(GROUP_WIDTH // LANES):
            lanes = slice(half * LANES, (half + 1) * LANES)
            first = scratch_slot + half * _slots_per_half(r)
            slab_ref[first] = p[:, lanes]
            if _slots_per_half(r) == 1:
                for c in range(r):
                    ref[0, c, :, lanes] = slab_ref[first, pl.ds(c, tm // r, stride=r), :].astype(BF16)
                continue
            step, rest = DEINTERLEAVE_STEP, r // DEINTERLEAVE_STEP
            for c0 in range(step):
                slab_ref[first + 1, c0 * (tm // step):(c0 + 1) * (tm // step), :] = (
                    slab_ref[first, pl.ds(c0, tm // step, stride=step), :])
            for c0 in range(step):
                for c1 in range(rest):
                    ref[0, c1 * step + c0, :, lanes] = slab_ref[
                        first + 1, pl.ds(c0 * (tm // step) + c1, tm // r, stride=rest), :].astype(BF16)

    for s in range(POOL_WIDTH // GROUP_WIDTH):
        lanes = slice(s * GROUP_WIDTH, (s + 1) * GROUP_WIDTH)
        pext_ref[POOL_HALO:, lanes] = project(qkv_width + s * GROUP_WIDTH)
    deltas = _pool_deltas(pext_ref, t * tm)
    pext_ref[0:POOL_HALO, :] = pext_ref[tm:tm + POOL_HALO, :]
    for s in range(d_model // GROUP_WIDTH):
        lanes = slice(s * GROUP_WIDTH, (s + 1) * GROUP_WIDTH)
        ga_ref[:, lanes] = _sigmoid(project(gate_col + s * GROUP_WIDTH)).astype(BF16)
    halves = GROUP_WIDTH // LANES
    slot = 0
    for g, ref in enumerate((q0_ref, q1_ref, q2_ref)):
        qkv_slab("q", g, ref, slot)
        slot += halves * _slots_per_half(DILATED_GROUPS[g][1])
    y_pool = _pool_project(deltas, poolw_ref, pscale_ref, wb_ref)
    for s in range(d_model // GROUP_WIDTH):
        lanes = slice(s * GROUP_WIDTH, (s + 1) * GROUP_WIDTH)
        gate_pool = _sigmoid(project(gate_col + d_model + s * GROUP_WIDTH))
        gpool_ref[:, lanes] = (gate_pool * y_pool[:, lanes]).astype(BF16)
    for kind, refs in (("k", (k0_ref, k1_ref, k2_ref)), ("v", (v0_ref, v1_ref, v2_ref))):
        for g in reversed(range(N_GROUPS)):
            qkv_slab(kind, g, refs[g], slot)
            slot += halves * _slots_per_half(DILATED_GROUPS[g][1])


def _attn_kernel(q_ref, k_ref, v_ref, kh_ref, vh_ref, *rest, blocks_per_residue, n_back, n_cast):
    cast_in, (o_ref, lse_ref), cast_out = rest[:n_cast], rest[n_cast:n_cast + 2], rest[n_cast + 2:2 * n_cast + 2]
    s_ref, p_ref = rest[2 * n_cast + 2:]
    for w_ref, w_bf_ref in zip(cast_in, cast_out):
        w_bf_ref[...] = w_ref[...].astype(BF16)
    rows = q_ref.shape[1]
    nblk = rows // ATTN_BLOCK
    step = pl.program_id(1)
    qi = lax.broadcasted_iota(jnp.int32, (ATTN_BLOCK, 2 * ATTN_BLOCK), 0)
    kj = lax.broadcasted_iota(jnp.int32, (ATTN_BLOCK, 2 * ATTN_BLOCK), 1)
    dist = qi + ATTN_BLOCK - kj
    band = (dist >= 0) & (dist <= n_back)
    in_current = kj >= ATTN_BLOCK
    head_of_lane = lax.broadcasted_iota(jnp.int32, (1, GROUP_WIDTH), 1) // HEAD_DIM
    heads = range(HEADS_PER_GROUP)
    heads_per_vreg = LANES // HEAD_DIM

    def head_rows(h):
        return slice(h * ATTN_BLOCK, (h + 1) * ATTN_BLOCK)

    def keys_values(j, ref, halo_ref):
        if j == 0:
            return jnp.concatenate([halo_ref[0], ref[0, 0:ATTN_BLOCK, :]], axis=0)
        return ref[0, (j - 1) * ATTN_BLOCK:(j + 1) * ATTN_BLOCK, :]

    def scores(j):
        qb = q_ref[0, j * ATTN_BLOCK:(j + 1) * ATTN_BLOCK, :]
        qs = jnp.concatenate([jnp.where(head_of_lane == h, qb, jnp.zeros_like(qb)) for h in heads], axis=0)
        return lax.dot_general(qs, keys_values(j, k_ref, kh_ref), (((1,), (1,)), ((), ())),
                               preferred_element_type=F32)

    def lanes_by_head(per_head):
        first_in_vreg = (head_of_lane[:, :LANES] % heads_per_vreg) == 0
        vregs = [jnp.where(first_in_vreg, per_head[v * heads_per_vreg], per_head[v * heads_per_vreg + 1])
                 for v in range(GROUP_WIDTH // LANES)]
        return jnp.concatenate(vregs, axis=1)

    s_ref[0] = scores(0)
    for j in range(nblk):
        if j + 1 < nblk:
            s_ref[(j + 1) % 2] = scores(j + 1)
        has_prev = ((step * nblk + j) % blocks_per_residue) != 0
        allowed = band & (in_current | has_prev)
        inv_l, lse = [], []
        for h in heads:
            s = jnp.where(allowed, s_ref[j % 2, head_rows(h), :], -jnp.inf)
            m = jnp.max(s, axis=-1, keepdims=True)
            p = jnp.exp2(s - m)
            l = jnp.sum(p, axis=-1, keepdims=True)
            p_ref[head_rows(h), :] = p.astype(BF16)
            inv_l.append(1.0 / l)
            lse.append(m + jnp.log2(l))
        pv = jnp.dot(p_ref[...], keys_values(j, v_ref, vh_ref), preferred_element_type=F32)
        o_heads = [pv[head_rows(h), (h // heads_per_vreg) * LANES:(h // heads_per_vreg + 1) * LANES] * inv_l[h]
                   for h in heads]
        cur = slice(j * ATTN_BLOCK, (j + 1) * ATTN_BLOCK)
        o_ref[0, cur, :] = lanes_by_head(o_heads).astype(BF16)
        lse_ref[0, cur, :] = lanes_by_head([jnp.broadcast_to(x, (ATTN_BLOCK, LANES)) for x in lse])


def _merge_ffn2_kernel(h1_ref, o0_ref, o1_ref, o2_ref, l0_ref, l1_ref, l2_ref, ga_ref, gpool_ref,
                       wa_ref, wo_ref, g2_ref, w1_ref, w3_ref, w2_ref, gf_ref,
                       out_ref, hid_ref, nat_ref):
    tm = h1_ref.shape[0]

    def natural_order(ref, r, slot):
        if r == 1:
            return ref[0, 0].astype(F32)
        halves = []
        for half in range(GROUP_WIDTH // LANES):
            lanes = slice(half * LANES, (half + 1) * LANES)
            for c in range(r):
                nat_ref[slot + half, pl.ds(c, tm // r, stride=r), :] = ref[0, c, :, lanes].astype(F32)
            halves.append(nat_ref[slot + half])
        return jnp.concatenate(halves, axis=1)

    outs, lses = [], []
    slot = 0
    halves_per_slab = GROUP_WIDTH // LANES
    for (_, r), o_ref, l_ref in zip(DILATED_GROUPS, (o0_ref, o1_ref, o2_ref), (l0_ref, l1_ref, l2_ref)):
        outs.append(natural_order(o_ref, r, slot))
        lses.append(natural_order(l_ref, r, slot + halves_per_slab))
        if r != 1:
            slot += 2 * halves_per_slab
    top = functools.reduce(jnp.maximum, lses)
    weights = [jnp.exp2(l - top) for l in lses]
    o = sum(w * og for w, og in zip(weights, outs)) / sum(weights)
    y_attn = jnp.dot(o.astype(BF16), wa_ref[...], preferred_element_type=F32)

    merged = ga_ref[...].astype(F32) * y_attn + gpool_ref[...].astype(F32)
    h2 = h1_ref[...] + jnp.dot(merged.astype(BF16), wo_ref[...], preferred_element_type=F32)
    u2 = _rms(h2, g2_ref[...]).astype(BF16)
    h3 = h2 + 0.5 * _swiglu(u2, w1_ref, w3_ref, w2_ref, hid_ref)
    out_ref[...] = _rms(h3, gf_ref[...])


def _resident(shape):
    return pl.BlockSpec(shape, lambda *_: (0,) * len(shape), pipeline_mode=pl.Buffered(1))


def _pallas_call(body, name, grid, inputs, outputs, scratch):
    def window_bytes(aval, spec):
        buffers = 1 if isinstance(spec.pipeline_mode, pl.Buffered) and spec.pipeline_mode.buffer_count == 1 else 2
        elems = 1
        for d in spec.block_shape:
            elems *= d
        return buffers * elems * jnp.dtype(aval.dtype).itemsize

    request = COMPILER_TEMP_BYTES + sum(window_bytes(a, spec) for a, spec in (*inputs, *outputs))
    for buf in scratch:
        elems = 1
        for d in buf.shape:
            elems *= d
        request += elems * jnp.dtype(buf.dtype).itemsize
    assert request <= V7X_SCOPED_VMEM_MAX_BYTES, (name, request)
    return pl.pallas_call(
        body, name=name, grid=grid,
        in_specs=[spec for _, spec in inputs], out_specs=[spec for _, spec in outputs],
        out_shape=[shape for shape, _ in outputs], scratch_shapes=scratch,
        compiler_params=pltpu.CompilerParams(dimension_semantics=("arbitrary",) * len(grid),
                                             vmem_limit_bytes=V7X_SCOPED_VMEM_MAX_BYTES),
    )(*[a for a, _ in inputs])


def _rope_tables(seq):
    half = ROT_DIM // 2
    j = np.arange(LANES) % HEAD_DIM
    inv = np.where(j < ROT_DIM, ROPE_THETA ** (-(2.0 * (j % half)) / ROT_DIM), 0.0)
    sign = np.where(j < half, -1.0, 1.0)
    ang = np.arange(seq, dtype=np.float64)[:, None] * inv[None, :]
    return jnp.asarray(np.cos(ang), F32), jnp.asarray(np.sin(ang) * sign[None, :], F32)


def _layer(h, B, S, ffn1_norm, ffn1_w1, ffn1_w3, ffn1_w2, mix_norm, w_in, w_branch_attn, w_branch_pool,
           pool_w, pool_scale, w_out, ffn2_norm, ffn2_w1, ffn2_w3, ffn2_w2, final_gain, cos_t, sin_t):
    T, D = h.shape
    d_ff = ffn1_w1.shape[1]
    in_width = w_in.shape[1]
    assert in_width == 3 * N_GROUPS * GROUP_WIDTH + POOL_WIDTH + 2 * D
    tm = TOKEN_TILE
    tiles_per_seq = S // tm
    n_tiles = T // tm
    bf = lambda w: w.astype(BF16)
    row = lambda g: g.reshape(1, -1).astype(F32)
    tok = lambda width: pl.BlockSpec((tm, width), lambda i: (i, 0))
    table = pl.BlockSpec((tm, LANES), lambda i: (i % tiles_per_seq, 0))
    whole = lambda a: (a, _resident(a.shape))

    def residue_major(r):
        return pl.BlockSpec((1, r, tm // r, GROUP_WIDTH), lambda i: (i // tiles_per_seq, 0, i % tiles_per_seq, 0))

    qkv_outputs = [(jax.ShapeDtypeStruct((B, r, S // r, GROUP_WIDTH), BF16), residue_major(r))
                   for _, r in DILATED_GROUPS] * 3
    slab_slots = 3 * sum(_slots_per_half(r) for _, r in DILATED_GROUPS) * GROUP_WIDTH // LANES
    outs = _pallas_call(
        functools.partial(_ffn1_proj_kernel, tiles_per_seq=tiles_per_seq), "ffn1_proj", (n_tiles,),
        inputs=[(h, tok(D)), whole(row(ffn1_norm)), whole(bf(ffn1_w1)), whole(bf(ffn1_w3)), whole(bf(ffn1_w2)),
                whole(row(mix_norm)), whole(bf(w_in)), (cos_t, table), (sin_t, table),
                whole(bf(pool_w)), whole(row(pool_scale)), whole(bf(w_branch_pool))],
        outputs=[(jax.ShapeDtypeStruct((T, D), F32), tok(D))] + qkv_outputs
        + [(jax.ShapeDtypeStruct((T, D), BF16), tok(D))] * 2,
        scratch=[pltpu.VMEM((tm, d_ff), BF16), pltpu.VMEM((slab_slots, tm, LANES), F32),
                 pltpu.VMEM((POOL_HALO + tm, POOL_WIDTH), F32)])
    h1 = outs[0]
    qs, ks, vs = outs[1:4], outs[4:7], outs[7:10]
    gate_attn, gated_pool = outs[10], outs[11]

    attn_o, attn_lse = [], []
    rows = ATTN_ROWS
    blocks_per_step = rows // ATTN_BLOCK
    steps_per_batch = S // rows
    attn_steps = B * steps_per_batch
    cast_jobs = ((ffn2_w1, w_out), (ffn2_w3, w_branch_attn), (ffn2_w2,))
    cast_done = []
    for g, (window, r) in enumerate(DILATED_GROUPS):
        L = S // r
        flat = lambda a: a.reshape(B, S, GROUP_WIDTH)
        cur = pl.BlockSpec((1, rows, GROUP_WIDTH), lambda b, s: (b, s, 0))
        prev = pl.BlockSpec((1, ATTN_BLOCK, GROUP_WIDTH),
                            lambda b, s: (b, jnp.maximum(s * blocks_per_step - 1, 0), 0))
        weights = cast_jobs[g]
        assert all(w.shape[0] % (16 * attn_steps) == 0 for w in weights)
        w_specs = [pl.BlockSpec((w.shape[0] // attn_steps, w.shape[1]), lambda b, s: (b * steps_per_batch + s, 0))
                   for w in weights]
        o_g, lse_g, *w_bf = _pallas_call(
            functools.partial(_attn_kernel, blocks_per_residue=L // ATTN_BLOCK, n_back=window // r,
                              n_cast=len(weights)),
            f"dilated_attn_g{g}", (B, steps_per_batch),
            inputs=[(flat(qs[g]), cur), (flat(ks[g]), cur), (flat(vs[g]), cur), (flat(ks[g]), prev),
                    (flat(vs[g]), prev)] + list(zip(weights, w_specs)),
            outputs=[(jax.ShapeDtypeStruct((B, S, GROUP_WIDTH), BF16), cur),
                     (jax.ShapeDtypeStruct((B, S, GROUP_WIDTH), F32), cur)]
            + [(jax.ShapeDtypeStruct(w.shape, BF16), spec) for w, spec in zip(weights, w_specs)],
            scratch=[pltpu.VMEM((2, HEADS_PER_GROUP * ATTN_BLOCK, 2 * ATTN_BLOCK), F32),
                     pltpu.VMEM((HEADS_PER_GROUP * ATTN_BLOCK, 2 * ATTN_BLOCK), BF16)])
        attn_o.append(o_g.reshape(B, r, L, GROUP_WIDTH))
        attn_lse.append(lse_g.reshape(B, r, L, GROUP_WIDTH))
        cast_done.extend(w_bf)
    ffn2_w1_bf, w_out_bf, ffn2_w3_bf, w_branch_attn_bf, ffn2_w2_bf = cast_done

    o_specs = [residue_major(r) for _, r in DILATED_GROUPS]
    (out,) = _pallas_call(
        _merge_ffn2_kernel, "merge_ffn2", (n_tiles,),
        inputs=[(h1, tok(D))] + list(zip(attn_o, o_specs)) + list(zip(attn_lse, o_specs))
        + [(gate_attn, tok(D)), (gated_pool, tok(D)), whole(w_branch_attn_bf), whole(w_out_bf),
           whole(row(ffn2_norm)), whole(ffn2_w1_bf), whole(ffn2_w3_bf), whole(ffn2_w2_bf), whole(row(final_gain))],
        outputs=[(jax.ShapeDtypeStruct((T, D), F32), tok(D))],
        scratch=[pltpu.VMEM((tm, d_ff), BF16),
                 pltpu.VMEM((2 * (N_GROUPS - 1) * GROUP_WIDTH // LANES, tm, LANES), F32)])
    return out


def kernel(x, ffn1_norm, ffn1_w1, ffn1_w3, ffn1_w2, mix_norm, w_in, w_branch_attn, w_branch_pool, pool_w,
           pool_scale, w_out, ffn2_norm, ffn2_w1, ffn2_w3, ffn2_w2, final_norm):
    B, S, D = x.shape
    depth = ffn1_norm.shape[0]
    assert S % TOKEN_TILE == 0 and S % ATTN_ROWS == 0
    assert all(window // r == ATTN_BLOCK and (S // r) % ATTN_BLOCK == 0 for window, r in DILATED_GROUPS)
    assert all(TOKEN_TILE % (16 * r) == 0 for _, r in DILATED_GROUPS)
    cos_t, sin_t = _rope_tables(S)
    h = x.reshape(B * S, D)
    for l in range(depth):
        assert l == depth - 1, "only the last layer's output norm is implemented"
        h = _layer(h, B, S, ffn1_norm[l], ffn1_w1[l], ffn1_w3[l], ffn1_w2[l], mix_norm[l], w_in[l],
                   w_branch_attn[l], w_branch_pool[l], pool_w[l], pool_scale[l], w_out[l], ffn2_norm[l],
                   ffn2_w1[l], ffn2_w3[l], ffn2_w2[l], final_norm, cos_t, sin_t)
    return h.reshape(B, S, D)
```

```python
import functools

import jax
import jax.numpy as jnp
import numpy as np
from jax import lax
from jax.experimental import pallas as pl
from jax.experimental.pallas import tpu as pltpu

F32 = jnp.float32
BF16 = jnp.bfloat16

HEAD_DIM = 64
HEADS_PER_GROUP = 4
GROUP_WIDTH = HEADS_PER_GROUP * HEAD_DIM
DILATED_GROUPS = ((128, 1), (512, 4), (2048, 16))
N_GROUPS = len(DILATED_GROUPS)
ATTN_BLOCK = 128
ROT_DIM = HEAD_DIM // 4
ROPE_THETA = 500000.0
POOL_WINDOWS = (2, 4, 8, 16)
POOL_GROUP_WIDTH = 128
POOL_WIDTH = len(POOL_WINDOWS) * POOL_GROUP_WIDTH
POOL_HALO = max(POOL_WINDOWS)
RMS_EPS = 1e-6
SCORE_SCALE_LOG2 = HEAD_DIM ** -0.5 * 1.4426950408889634

LANES = 128
BF16_SUBLANES = 16
V7X_SCOPED_VMEM_MAX_BYTES = 60000 * 1024
COMPILER_TEMP_BYTES = 4 * 1024 * 1024

TOKEN_TILE = 512
FF_CHUNK = 256
ATTN_ROWS = 4096
DEINTERLEAVE_STEP = 4


def _rms(x, gain):
    return x * lax.rsqrt(jnp.mean(x * x, axis=-1, keepdims=True) + RMS_EPS) * gain


def _sigmoid(x):
    return 0.5 * jnp.tanh(0.5 * x) + 0.5


def _swiglu(u, w1_ref, w3_ref, w2_ref, hid_ref):
    d_ff = w1_ref.shape[1]
    for f in range(d_ff // FF_CHUNK):
        cols = slice(f * FF_CHUNK, (f + 1) * FF_CHUNK)
        a = jnp.dot(u, w1_ref[:, cols], preferred_element_type=F32)
        b = jnp.dot(u, w3_ref[:, cols], preferred_element_type=F32)
        half = 0.5 * a
        hid_ref[:, cols] = ((half * jnp.tanh(half) + half) * b).astype(BF16)
    return jnp.dot(hid_ref[...], w2_ref[...], preferred_element_type=F32)


def _slots_per_half(r):
    return 0 if r == 1 else 1 if r <= DEINTERLEAVE_STEP else 2


def _rope(x, cos, sin):
    j = lax.broadcasted_iota(jnp.int32, x.shape, 1) % HEAD_DIM
    half = ROT_DIM // 2
    partner = jnp.where(j < half, pltpu.roll(x, LANES - half, axis=1), pltpu.roll(x, half, axis=1))
    return jnp.where(j < ROT_DIM, x * cos + partner * sin, x)


def _pool_deltas(pext_ref, first_pos):
    tm = pext_ref.shape[0] - POOL_HALO
    pos = first_pos + lax.broadcasted_iota(jnp.int32, (tm, 1), 0)
    deltas = []
    for gi, win in enumerate(POOL_WINDOWS):
        lanes = slice(gi * POOL_GROUP_WIDTH, (gi + 1) * POOL_GROUP_WIDTH)
        own = pext_ref[POOL_HALO:POOL_HALO + tm, lanes]
        total = own
        for back in range(1, win):
            total = total + pext_ref[POOL_HALO - back:POOL_HALO - back + tm, lanes]
        count = jnp.minimum(pos + 1, win).astype(F32)
        deltas.append((total / count - own).astype(BF16))
    return deltas


def _pool_project(deltas, poolw_ref, pscale_ref, wb_ref):
    pooled = []
    for gi, d in enumerate(deltas):
        lanes = slice(gi * POOL_GROUP_WIDTH, (gi + 1) * POOL_GROUP_WIDTH)
        pooled.append(jnp.dot(d, poolw_ref[gi], preferred_element_type=F32) * pscale_ref[:, lanes])
    return jnp.dot(jnp.concatenate(pooled, axis=1).astype(BF16), wb_ref[...], preferred_element_type=F32)


def _ffn1_proj_kernel(x_ref, g1_ref, w1_ref, w3_ref, w2_ref, gm_ref, win_ref, cos_ref, sin_ref,
                      poolw_ref, pscale_ref, wb_ref, *rest, tiles_per_seq, n_cast):
    cast_in, rest = rest[:n_cast], rest[n_cast:]
    (h1_ref, q0_ref, q1_ref, q2_ref, k0_ref, k1_ref, k2_ref, v0_ref, v1_ref, v2_ref, ga_ref, gpool_ref), rest = (
        rest[:12], rest[12:])
    cast_out, (hid_ref, slab_ref, pext_ref) = rest[:n_cast], rest[n_cast:]
    for w_ref, w_bf_ref in zip(cast_in, cast_out):
        w_bf_ref[...] = w_ref[...].astype(BF16)
    tm = x_ref.shape[0]
    d_model = x_ref.shape[1]
    t = pl.program_id(0) % tiles_per_seq

    @pl.when(t == 0)
    def _():
        pext_ref[0:POOL_HALO, :] = jnp.zeros((POOL_HALO, POOL_WIDTH), F32)

    x = x_ref[...]
    u = _rms(x, g1_ref[...]).astype(BF16)
    h1 = x + 0.5 * _swiglu(u, w1_ref, w3_ref, w2_ref, hid_ref)
    h1_ref[...] = h1
    um = _rms(h1, gm_ref[...]).astype(BF16)
    cos = cos_ref[...]
    sin = sin_ref[...]

    def project(col):
        return jnp.dot(um, win_ref[:, col:col + GROUP_WIDTH], preferred_element_type=F32)

    def rope(p):
        return jnp.concatenate([_rope(p[:, :LANES], cos, sin), _rope(p[:, LANES:], cos, sin)], axis=1)

    qkv_width = 3 * N_GROUPS * GROUP_WIDTH
    gate_col = qkv_width + POOL_WIDTH

    def qkv_slab(kind, g, ref, scratch_slot):
        r = DILATED_GROUPS[g][1]
        p = project(("q", "k", "v").index(kind) * N_GROUPS * GROUP_WIDTH + g * GROUP_WIDTH)
        if kind == "q":
            p = rope(p) * SCORE_SCALE_LOG2
        elif kind == "k":
            p = rope(p)
        if r == 1:
            ref[0, 0] = p.astype(BF16)
            return
        for half in range(GROUP_WIDTH // LANES):
            lanes = slice(half * LANES, (half + 1) * LANES)
            first = scratch_slot + half * _slots_per_half(r)
            slab_ref[first] = p[:, lanes]
            if _slots_per_half(r) == 1:
                for c in range(r):
                    ref[0, c, :, lanes] = slab_ref[first, pl.ds(c, tm // r, stride=r), :].astype(BF16)
                continue
            step, rest = DEINTERLEAVE_STEP, r // DEINTERLEAVE_STEP
            for c0 in range(step):
                slab_ref[first + 1, c0 * (tm // step):(c0 + 1) * (tm // step), :] = (
                    slab_ref[first, pl.ds(c0, tm // step, stride=step), :])
            for c0 in range(step):
                for c1 in range(rest):
                    ref[0, c1 * step + c0, :, lanes] = slab_ref[
                        first + 1, pl.ds(c0 * (tm // step) + c1, tm // r, stride=rest), :].astype(BF16)

    for s in range(POOL_WIDTH // GROUP_WIDTH):
        lanes = slice(s * GROUP_WIDTH, (s + 1) * GROUP_WIDTH)
        pext_ref[POOL_HALO:, lanes] = project(qkv_width + s * GROUP_WIDTH)
    deltas = _pool_deltas(pext_ref, t * tm)
    pext_ref[0:POOL_HALO, :] = pext_ref[tm:tm + POOL_HALO, :]
    for s in range(d_model // GROUP_WIDTH):
        lanes = slice(s * GROUP_WIDTH, (s + 1) * GROUP_WIDTH)
        ga_ref[:, lanes] = _sigmoid(project(gate_col + s * GROUP_WIDTH)).astype(BF16)
    halves = GROUP_WIDTH // LANES
    first_slot = [halves * sum(_slots_per_half(r) for _, r in DILATED_GROUPS[:g]) for g in range(N_GROUPS)]
    for g, ref in enumerate((q0_ref, q1_ref, q2_ref)):
        qkv_slab("q", g, ref, first_slot[g])
    y_pool = _pool_project(deltas, poolw_ref, pscale_ref, wb_ref)
    for s in range(d_model // GROUP_WIDTH):
        lanes = slice(s * GROUP_WIDTH, (s + 1) * GROUP_WIDTH)
        gate_pool = _sigmoid(project(gate_col + d_model + s * GROUP_WIDTH))
        gpool_ref[:, lanes] = (gate_pool * y_pool[:, lanes]).astype(BF16)
    for kind, refs in (("k", (k0_ref, k1_ref, k2_ref)), ("v", (v0_ref, v1_ref, v2_ref))):
        for g in reversed(range(N_GROUPS)):
            qkv_slab(kind, g, refs[g], first_slot[g])


def _attn_kernel(q_ref, k_ref, v_ref, kh_ref, vh_ref, o_ref, lse_ref, s_ref, p_ref, *,
                 blocks_per_residue, n_back):
    rows = q_ref.shape[1]
    nblk = rows // ATTN_BLOCK
    step = pl.program_id(1)
    qi = lax.broadcasted_iota(jnp.int32, (ATTN_BLOCK, 2 * ATTN_BLOCK), 0)
    kj = lax.broadcasted_iota(jnp.int32, (ATTN_BLOCK, 2 * ATTN_BLOCK), 1)
    dist = qi + ATTN_BLOCK - kj
    band = (dist >= 0) & (dist <= n_back)
    in_current = kj >= ATTN_BLOCK
    head_of_lane = lax.broadcasted_iota(jnp.int32, (1, GROUP_WIDTH), 1) // HEAD_DIM
    heads = range(HEADS_PER_GROUP)
    heads_per_vreg = LANES // HEAD_DIM

    def head_rows(h):
        return slice(h * ATTN_BLOCK, (h + 1) * ATTN_BLOCK)

    def keys_values(j, ref, halo_ref):
        if j == 0:
            return jnp.concatenate([halo_ref[0], ref[0, 0:ATTN_BLOCK, :]], axis=0)
        return ref[0, (j - 1) * ATTN_BLOCK:(j + 1) * ATTN_BLOCK, :]

    def scores(j):
        qb = q_ref[0, j * ATTN_BLOCK:(j + 1) * ATTN_BLOCK, :]
        qs = jnp.concatenate([jnp.where(head_of_lane == h, qb, jnp.zeros_like(qb)) for h in heads], axis=0)
        return lax.dot_general(qs, keys_values(j, k_ref, kh_ref), (((1,), (1,)), ((), ())),
                               preferred_element_type=F32)

    def lanes_by_head(per_head):
        first_in_vreg = (head_of_lane[:, :LANES] % heads_per_vreg) == 0
        vregs = [jnp.where(first_in_vreg, per_head[v * heads_per_vreg], per_head[v * heads_per_vreg + 1])
                 for v in range(GROUP_WIDTH // LANES)]
        return jnp.concatenate(vregs, axis=1)

    s_ref[0] = scores(0)
    for j in range(nblk):
        if j + 1 < nblk:
            s_ref[(j + 1) % 2] = scores(j + 1)
        has_prev = ((step * nblk + j) % blocks_per_residue) != 0
        allowed = band & (in_current | has_prev)
        inv_l, lse = [], []
        for h in heads:
            s = jnp.where(allowed, s_ref[j % 2, head_rows(h), :], -jnp.inf)
            m = jnp.max(s, axis=-1, keepdims=True)
            p = jnp.exp2(s - m)
            l = jnp.sum(p, axis=-1, keepdims=True)
            p_ref[head_rows(h), :] = p.astype(BF16)
            inv_l.append(1.0 / l)
            lse.append(m + jnp.log2(l))
        pv = jnp.dot(p_ref[...], keys_values(j, v_ref, vh_ref), preferred_element_type=F32)
        o_heads = [pv[head_rows(h), (h // heads_per_vreg) * LANES:(h // heads_per_vreg + 1) * LANES] * inv_l[h]
                   for h in heads]
        cur = slice(j * ATTN_BLOCK, (j + 1) * ATTN_BLOCK)
        o_ref[0, cur, :] = lanes_by_head(o_heads).astype(BF16)
        lse_ref[0, cur, :] = lanes_by_head([jnp.broadcast_to(x, (ATTN_BLOCK, LANES)) for x in lse])


def _merge_ffn2_kernel(h1_ref, o0_ref, o1_ref, o2_ref, l0_ref, l1_ref, l2_ref, ga_ref, gpool_ref,
                       wa_ref, wo_ref, g2_ref, w1_ref, w3_ref, w2_ref, gf_ref,
                       out_ref, hid_ref, nat_ref):
    tm = h1_ref.shape[0]

    def natural_order(ref, r, slot):
        if r == 1:
            return ref[0, 0].astype(F32)
        halves = []
        for half in range(GROUP_WIDTH // LANES):
            lanes = slice(half * LANES, (half + 1) * LANES)
            for c in range(r):
                nat_ref[slot + half, pl.ds(c, tm // r, stride=r), :] = ref[0, c, :, lanes].astype(F32)
            halves.append(nat_ref[slot + half])
        return jnp.concatenate(halves, axis=1)

    outs, lses = [], []
    slot = 0
    halves_per_slab = GROUP_WIDTH // LANES
    for (_, r), o_ref, l_ref in zip(DILATED_GROUPS, (o0_ref, o1_ref, o2_ref), (l0_ref, l1_ref, l2_ref)):
        outs.append(natural_order(o_ref, r, slot))
        lses.append(natural_order(l_ref, r, slot + halves_per_slab))
        if r != 1:
            slot += 2 * halves_per_slab
    top = functools.reduce(jnp.maximum, lses)
    weights = [jnp.exp2(l - top) for l in lses]
    o = sum(w * og for w, og in zip(weights, outs)) / sum(weights)
    y_attn = jnp.dot(o.astype(BF16), wa_ref[...], preferred_element_type=F32)

    merged = ga_ref[...].astype(F32) * y_attn + gpool_ref[...].astype(F32)
    h2 = h1_ref[...] + jnp.dot(merged.astype(BF16), wo_ref[...], preferred_element_type=F32)
    u2 = _rms(h2, g2_ref[...]).astype(BF16)
    h3 = h2 + 0.5 * _swiglu(u2, w1_ref, w3_ref, w2_ref, hid_ref)
    out_ref[...] = _rms(h3, gf_ref[...])


def _resident(shape):
    return pl.BlockSpec(shape, lambda *_: (0,) * len(shape), pipeline_mode=pl.Buffered(1))


def _pallas_call(body, name, grid, inputs, outputs, scratch):
    def window_bytes(aval, spec):
        buffers = 1 if isinstance(spec.pipeline_mode, pl.Buffered) and spec.pipeline_mode.buffer_count == 1 else 2
        elems = 1
        for d in spec.block_shape:
            elems *= d
        return buffers * elems * jnp.dtype(aval.dtype).itemsize

    request = COMPILER_TEMP_BYTES + sum(window_bytes(a, spec) for a, spec in (*inputs, *outputs))
    for buf in scratch:
        elems = 1
        for d in buf.shape:
            elems *= d
        request += elems * jnp.dtype(buf.dtype).itemsize
    assert request <= V7X_SCOPED_VMEM_MAX_BYTES, (name, request)
    return pl.pallas_call(
        body, name=name, grid=grid,
        in_specs=[spec for _, spec in inputs], out_specs=[spec for _, spec in outputs],
        out_shape=[shape for shape, _ in outputs], scratch_shapes=scratch,
        compiler_params=pltpu.CompilerParams(dimension_semantics=("arbitrary",) * len(grid),
                                             vmem_limit_bytes=V7X_SCOPED_VMEM_MAX_BYTES),
    )(*[a for a, _ in inputs])


def _rope_tables(seq):
    half = ROT_DIM // 2
    j = np.arange(LANES) % HEAD_DIM
    inv = np.where(j < ROT_DIM, ROPE_THETA ** (-(2.0 * (j % half)) / ROT_DIM), 0.0)
    sign = np.where(j < half, -1.0, 1.0)
    ang = np.arange(seq, dtype=np.float64)[:, None] * inv[None, :]
    return jnp.asarray(np.cos(ang), F32), jnp.asarray(np.sin(ang) * sign[None, :], F32)


def _layer(h, B, S, ffn1_norm, ffn1_w1, ffn1_w3, ffn1_w2, mix_norm, w_in, w_branch_attn, w_branch_pool,
           pool_w, pool_scale, w_out, ffn2_norm, ffn2_w1, ffn2_w3, ffn2_w2, final_gain, cos_t, sin_t):
    T, D = h.shape
    d_ff = ffn1_w1.shape[1]
    in_width = w_in.shape[1]
    assert in_width == 3 * N_GROUPS * GROUP_WIDTH + POOL_WIDTH + 2 * D
    tm = TOKEN_TILE
    tiles_per_seq = S // tm
    n_tiles = T // tm
    bf = lambda w: w.astype(BF16)
    row = lambda g: g.reshape(1, -1).astype(F32)
    tok = lambda width: pl.BlockSpec((tm, width), lambda i: (i, 0))
    table = pl.BlockSpec((tm, LANES), lambda i: (i % tiles_per_seq, 0))
    whole = lambda a: (a, _resident(a.shape))

    def residue_major(r):
        return pl.BlockSpec((1, r, tm // r, GROUP_WIDTH), lambda i: (i // tiles_per_seq, 0, i % tiles_per_seq, 0))

    qkv_outputs = [(jax.ShapeDtypeStruct((B, r, S // r, GROUP_WIDTH), BF16), residue_major(r))
                   for _, r in DILATED_GROUPS] * 3
    slab_slots = sum(_slots_per_half(r) for _, r in DILATED_GROUPS) * GROUP_WIDTH // LANES
    cast_jobs = ((ffn2_w1, D // n_tiles), (ffn2_w3, D // n_tiles), (ffn2_w2, 4 * D // n_tiles),
                 (w_out, D // n_tiles), (w_branch_attn, BF16_SUBLANES))
    assert all(rows % BF16_SUBLANES == 0 and w.shape[0] % rows == 0 and w.shape[0] // rows <= n_tiles
               for w, rows in cast_jobs)
    cast_specs = [pl.BlockSpec((rows, w.shape[1]), functools.partial(
        lambda i, blocks: (jnp.minimum(i, blocks - 1), 0), blocks=w.shape[0] // rows)) for w, rows in cast_jobs]
    outs = _pallas_call(
        functools.partial(_ffn1_proj_kernel, tiles_per_seq=tiles_per_seq, n_cast=len(cast_jobs)),
        "ffn1_proj", (n_tiles,),
        inputs=[(h, tok(D)), whole(row(ffn1_norm)), whole(bf(ffn1_w1)), whole(bf(ffn1_w3)), whole(bf(ffn1_w2)),
                whole(row(mix_norm)), whole(bf(w_in)), (cos_t, table), (sin_t, table),
                whole(bf(pool_w)), whole(row(pool_scale)), whole(bf(w_branch_pool))]
        + [(w, spec) for (w, _), spec in zip(cast_jobs, cast_specs)],
        outputs=[(jax.ShapeDtypeStruct((T, D), F32), tok(D))] + qkv_outputs
        + [(jax.ShapeDtypeStruct((T, D), BF16), tok(D))] * 2
        + [(jax.ShapeDtypeStruct(w.shape, BF16), spec) for (w, _), spec in zip(cast_jobs, cast_specs)],
        scratch=[pltpu.VMEM((tm, d_ff), BF16), pltpu.VMEM((slab_slots, tm, LANES), F32),
                 pltpu.VMEM((POOL_HALO + tm, POOL_WIDTH), F32)])
    h1 = outs[0]
    qs, ks, vs = outs[1:4], outs[4:7], outs[7:10]
    gate_attn, gated_pool = outs[10], outs[11]
    ffn2_w1_bf, ffn2_w3_bf, ffn2_w2_bf, w_out_bf, w_branch_attn_bf = outs[12:]

    attn_o, attn_lse = [], []
    rows = ATTN_ROWS
    blocks_per_step = rows // ATTN_BLOCK
    steps_per_batch = S // rows
    for g, (window, r) in enumerate(DILATED_GROUPS):
        L = S // r
        flat = lambda a: a.reshape(B, S, GROUP_WIDTH)
        cur = pl.BlockSpec((1, rows, GROUP_WIDTH), lambda b, s: (b, s, 0))
        prev = pl.BlockSpec((1, ATTN_BLOCK, GROUP_WIDTH),
                            lambda b, s: (b, jnp.maximum(s * blocks_per_step - 1, 0), 0))
        o_g, lse_g = _pallas_call(
            functools.partial(_attn_kernel, blocks_per_residue=L // ATTN_BLOCK, n_back=window // r),
            f"dilated_attn_g{g}", (B, steps_per_batch),
            inputs=[(flat(qs[g]), cur), (flat(ks[g]), cur), (flat(vs[g]), cur), (flat(ks[g]), prev),
                    (flat(vs[g]), prev)],
            outputs=[(jax.ShapeDtypeStruct((B, S, GROUP_WIDTH), BF16), cur),
                     (jax.ShapeDtypeStruct((B, S, GROUP_WIDTH), F32), cur)],
            scratch=[pltpu.VMEM((2, HEADS_PER_GROUP * ATTN_BLOCK, 2 * ATTN_BLOCK), F32),
                     pltpu.VMEM((HEADS_PER_GROUP * ATTN_BLOCK, 2 * ATTN_BLOCK), BF16)])
        attn_o.append(o_g.reshape(B, r, L, GROUP_WIDTH))
        attn_lse.append(lse_g.reshape(B, r, L, GROUP_WIDTH))

    o_specs = [residue_major(r) for _, r in DILATED_GROUPS]
    (out,) = _pallas_call(
        _merge_ffn2_kernel, "merge_ffn2", (n_tiles,),
        inputs=[(h1, tok(D))] + list(zip(attn_o, o_specs)) + list(zip(attn_lse, o_specs))
        + [(gate_attn, tok(D)), (gated_pool, tok(D)), whole(w_branch_attn_bf), whole(w_out_bf),
           whole(row(ffn2_norm)), whole(ffn2_w1_bf), whole(ffn2_w3_bf), whole(ffn2_w2_bf), whole(row(final_gain))],
        outputs=[(jax.ShapeDtypeStruct((T, D), F32), tok(D))],
        scratch=[pltpu.VMEM((tm, d_ff), BF16),
                 pltpu.VMEM((2 * (N_GROUPS - 1) * GROUP_WIDTH // LANES, tm, LANES), F32)])
    return out


def kernel(x, ffn1_norm, ffn1_w1, ffn1_w3, ffn1_w2, mix_norm, w_in, w_branch_attn, w_branch_pool, pool_w,
           pool_scale, w_out, ffn2_norm, ffn2_w1, ffn2_w3, ffn2_w2, final_norm):
    B, S, D = x.shape
    depth = ffn1_norm.shape[0]
    assert S % TOKEN_TILE == 0 and S % ATTN_ROWS == 0
    assert all(window // r == ATTN_BLOCK and (S // r) % ATTN_BLOCK == 0 for window, r in DILATED_GROUPS)
    assert all(TOKEN_TILE % (16 * r) == 0 for _, r in DILATED_GROUPS)
    cos_t, sin_t = _rope_tables(S)
    h = x.reshape(B * S, D)
    for l in range(depth):
        assert l == depth - 1, "only the last layer's output norm is implemented"
        h = _layer(h, B, S, ffn1_norm[l], ffn1_w1[l], ffn1_w3[l], ffn1_w2[l], mix_norm[l], w_in[l],
                   w_branch_attn[l], w_branch_pool[l], pool_w[l], pool_scale[l], w_out[l], ffn2_norm[l],
                   ffn2_w1[l], ffn2_w3[l], ffn2_w2[l], final_norm, cos_t, sin_t)
    return h.reshape(B, S, D)
```

```python
import functools

import jax
import jax.numpy as jnp
import numpy as np
from jax import lax
from jax.experimental import pallas as pl
from jax.experimental.pallas import tpu as pltpu

F32 = jnp.float32
BF16 = jnp.bfloat16

HEAD_DIM = 64
HEADS_PER_GROUP = 4
GROUP_WIDTH = HEADS_PER_GROUP * HEAD_DIM
DILATED_GROUPS = ((128, 1), (512, 4), (2048, 16))
N_GROUPS = len(DILATED_GROUPS)
ATTN_BLOCK = 128
ROT_DIM = HEAD_DIM // 4
ROPE_THETA = 500000.0
POOL_WINDOWS = (2, 4, 8, 16)
POOL_GROUP_WIDTH = 128
POOL_WIDTH = len(POOL_WINDOWS) * POOL_GROUP_WIDTH
POOL_HALO = max(POOL_WINDOWS)
RMS_EPS = 1e-6
SCORE_SCALE_LOG2 = HEAD_DIM ** -0.5 * 1.4426950408889634

LANES = 128
BF16_SUBLANES = 16
V7X_SCOPED_VMEM_MAX_BYTES = 60000 * 1024
COMPILER_TEMP_BYTES = 4 * 1024 * 1024

TOKEN_TILE = 512
FF_CHUNK = 256
ATTN_ROWS = 4096
DEINTERLEAVE_STEP = 4


def _rms(x, gain):
    return x * lax.rsqrt(jnp.mean(x * x, axis=-1, keepdims=True) + RMS_EPS) * gain


def _sigmoid(x):
    return 0.5 * jnp.tanh(0.5 * x) + 0.5


def _swiglu(u, w1_ref, w3_ref, w2_ref, hid_ref):
    d_ff = w1_ref.shape[1]
    for f in range(d_ff // FF_CHUNK):
        cols = slice(f * FF_CHUNK, (f + 1) * FF_CHUNK)
        ab = jnp.dot(u, jnp.concatenate([w1_ref[:, cols], w3_ref[:, cols]], axis=1), preferred_element_type=F32)
        a, b = ab[:, :FF_CHUNK], ab[:, FF_CHUNK:]
        half = 0.5 * a
        hid_ref[:, cols] = ((half * jnp.tanh(half) + half) * b).astype(BF16)
    return jnp.dot(hid_ref[...], w2_ref[...], preferred_element_type=F32)


def _slots_per_half(r):
    return 0 if r == 1 else 1 if r <= DEINTERLEAVE_STEP else 2


def _rope(x, cos, sin):
    j = lax.broadcasted_iota(jnp.int32, x.shape, 1) % HEAD_DIM
    half = ROT_DIM // 2
    partner = jnp.where(j < half, pltpu.roll(x, LANES - half, axis=1), pltpu.roll(x, half, axis=1))
    return jnp.where(j < ROT_DIM, x * cos + partner * sin, x)


def _pool_deltas(pext_ref, first_pos):
    tm = pext_ref.shape[0] - POOL_HALO
    pos = first_pos + lax.broadcasted_iota(jnp.int32, (tm, 1), 0)
    deltas = []
    for gi, win in enumerate(POOL_WINDOWS):
        lanes = slice(gi * POOL_GROUP_WIDTH, (gi + 1) * POOL_GROUP_WIDTH)
        own = pext_ref[POOL_HALO:POOL_HALO + tm, lanes]
        total = own
        for back in range(1, win):
            total = total + pext_ref[POOL_HALO - back:POOL_HALO - back + tm, lanes]
        count = jnp.minimum(pos + 1, win).astype(F32)
        deltas.append((total / count - own).astype(BF16))
    return deltas


def _pool_project(deltas, poolw_ref, pscale_ref, wb_ref):
    pooled = []
    for gi, d in enumerate(deltas):
        lanes = slice(gi * POOL_GROUP_WIDTH, (gi + 1) * POOL_GROUP_WIDTH)
        pooled.append(jnp.dot(d, poolw_ref[gi], preferred_element_type=F32) * pscale_ref[:, lanes])
    return jnp.dot(jnp.concatenate(pooled, axis=1).astype(BF16), wb_ref[...], preferred_element_type=F32)


def _ffn1_proj_kernel(x_ref, g1_ref, w1_ref, w3_ref, w2_ref, gm_ref, win_ref, cos_ref, sin_ref,
                      poolw_ref, pscale_ref, wb_ref, *rest, tiles_per_seq, n_cast):
    cast_in, rest = rest[:n_cast], rest[n_cast:]
    (h1_ref, q0_ref, q1_ref, q2_ref, k0_ref, k1_ref, k2_ref, v0_ref, v1_ref, v2_ref, ga_ref, gpool_ref), rest = (
        rest[:12], rest[12:])
    cast_out, (hid_ref, slab_ref, pext_ref) = rest[:n_cast], rest[n_cast:]
    for w_ref, w_bf_ref in zip(cast_in, cast_out):
        w_bf_ref[...] = w_ref[...].astype(BF16)
    tm = x_ref.shape[0]
    d_model = x_ref.shape[1]
    t = pl.program_id(0) % tiles_per_seq

    @pl.when(t == 0)
    def _():
        pext_ref[0:POOL_HALO, :] = jnp.zeros((POOL_HALO, POOL_WIDTH), F32)

    x = x_ref[...]
    u = _rms(x, g1_ref[...]).astype(BF16)
    h1 = x + 0.5 * _swiglu(u, w1_ref, w3_ref, w2_ref, hid_ref)
    h1_ref[...] = h1
    um = _rms(h1, gm_ref[...]).astype(BF16)
    cos = cos_ref[...]
    sin = sin_ref[...]

    def project(col, width):
        return jnp.dot(um, win_ref[:, col:col + width], preferred_element_type=F32)

    def rope(p):
        return jnp.concatenate([_rope(p[:, :LANES], cos, sin), _rope(p[:, LANES:], cos, sin)], axis=1)

    qkv_width = 3 * N_GROUPS * GROUP_WIDTH
    gate_col = qkv_width + POOL_WIDTH
    halves = GROUP_WIDTH // LANES
    first_slot = [halves * sum(_slots_per_half(r) for _, r in DILATED_GROUPS[:g]) for g in range(N_GROUPS)]

    def store_residue_major(p, ref, r, scratch_slot):
        if r == 1:
            ref[0, 0] = p.astype(BF16)
            return
        for half in range(halves):
            lanes = slice(half * LANES, (half + 1) * LANES)
            first = scratch_slot + half * _slots_per_half(r)
            slab_ref[first] = p[:, lanes]
            if _slots_per_half(r) == 1:
                for c in range(r):
                    ref[0, c, :, lanes] = slab_ref[first, pl.ds(c, tm // r, stride=r), :].astype(BF16)
                continue
            step, rest = DEINTERLEAVE_STEP, r // DEINTERLEAVE_STEP
            for c0 in range(step):
                slab_ref[first + 1, c0 * (tm // step):(c0 + 1) * (tm // step), :] = (
                    slab_ref[first, pl.ds(c0, tm // step, stride=step), :])
            for c0 in range(step):
                for c1 in range(rest):
                    ref[0, c1 * step + c0, :, lanes] = slab_ref[
                        first + 1, pl.ds(c0 * (tm // step) + c1, tm // r, stride=rest), :].astype(BF16)

    def qkv(kind, refs, group_order):
        both = project(("q", "k", "v").index(kind) * N_GROUPS * GROUP_WIDTH, N_GROUPS * GROUP_WIDTH)
        for g in group_order:
            p = both[:, g * GROUP_WIDTH:(g + 1) * GROUP_WIDTH]
            if kind == "q":
                p = rope(p) * SCORE_SCALE_LOG2
            elif kind == "k":
                p = rope(p)
            store_residue_major(p, refs[g], DILATED_GROUPS[g][1], first_slot[g])

    pext_ref[POOL_HALO:, :] = project(qkv_width, POOL_WIDTH)
    deltas = _pool_deltas(pext_ref, t * tm)
    pext_ref[0:POOL_HALO, :] = pext_ref[tm:tm + POOL_HALO, :]
    pair = 2 * GROUP_WIDTH
    for s in range(d_model // pair):
        lanes = slice(s * pair, (s + 1) * pair)
        ga_ref[:, lanes] = _sigmoid(project(gate_col + s * pair, pair)).astype(BF16)
    qkv("q", (q0_ref, q1_ref, q2_ref), range(N_GROUPS))
    y_pool = _pool_project(deltas, poolw_ref, pscale_ref, wb_ref)
    for s in range(d_model // pair):
        lanes = slice(s * pair, (s + 1) * pair)
        gate_pool = _sigmoid(project(gate_col + d_model + s * pair, pair))
        gpool_ref[:, lanes] = (gate_pool * y_pool[:, lanes]).astype(BF16)
    qkv("k", (k0_ref, k1_ref, k2_ref), reversed(range(N_GROUPS)))
    qkv("v", (v0_ref, v1_ref, v2_ref), reversed(range(N_GROUPS)))


def _attn_kernel(q_ref, k_ref, v_ref, kh_ref, vh_ref, o_ref, lse_ref, s_ref, p_ref, *,
                 blocks_per_residue, n_back):
    rows = q_ref.shape[1]
    nblk = rows // ATTN_BLOCK
    step = pl.program_id(1)
    qi = lax.broadcasted_iota(jnp.int32, (ATTN_BLOCK, 2 * ATTN_BLOCK), 0)
    kj = lax.broadcasted_iota(jnp.int32, (ATTN_BLOCK, 2 * ATTN_BLOCK), 1)
    dist = qi + ATTN_BLOCK - kj
    band = (dist >= 0) & (dist <= n_back)
    in_current = kj >= ATTN_BLOCK
    head_of_lane = lax.broadcasted_iota(jnp.int32, (1, GROUP_WIDTH), 1) // HEAD_DIM
    heads = range(HEADS_PER_GROUP)
    heads_per_vreg = LANES // HEAD_DIM

    def head_rows(h):
        return slice(h * ATTN_BLOCK, (h + 1) * ATTN_BLOCK)

    def keys_values(j, ref, halo_ref):
        if j == 0:
            return jnp.concatenate([halo_ref[0], ref[0, 0:ATTN_BLOCK, :]], axis=0)
        return ref[0, (j - 1) * ATTN_BLOCK:(j + 1) * ATTN_BLOCK, :]

    def scores(j):
        qb = q_ref[0, j * ATTN_BLOCK:(j + 1) * ATTN_BLOCK, :]
        qs = jnp.concatenate([jnp.where(head_of_lane == h, qb, jnp.zeros_like(qb)) for h in heads], axis=0)
        return lax.dot_general(qs, keys_values(j, k_ref, kh_ref), (((1,), (1,)), ((), ())),
                               preferred_element_type=F32)

    def lanes_by_head(per_head):
        first_in_vreg = (head_of_lane[:, :LANES] % heads_per_vreg) == 0
        vregs = [jnp.where(first_in_vreg, per_head[v * heads_per_vreg], per_head[v * heads_per_vreg + 1])
                 for v in range(GROUP_WIDTH // LANES)]
        return jnp.concatenate(vregs, axis=1)

    s_ref[0] = scores(0)
    for j in range(nblk):
        if j + 1 < nblk:
            s_ref[(j + 1) % 2] = scores(j + 1)
        has_prev = ((step * nblk + j) % blocks_per_residue) != 0
        allowed = band & (in_current | has_prev)
        inv_l, lse = [], []
        for h in heads:
            s = jnp.where(allowed, s_ref[j % 2, head_rows(h), :], -jnp.inf)
            m = jnp.max(s, axis=-1, keepdims=True)
            p = jnp.exp2(s - m)
            l = jnp.sum(p, axis=-1, keepdims=True)
            p_ref[head_rows(h), :] = p.astype(BF16)
            inv_l.append(1.0 / l)
            lse.append(m + jnp.log2(l))
        pv = jnp.dot(p_ref[...], keys_values(j, v_ref, vh_ref), preferred_element_type=F32)
        o_heads = [pv[head_rows(h), (h // heads_per_vreg) * LANES:(h // heads_per_vreg + 1) * LANES] * inv_l[h]
                   for h in heads]
        cur = slice(j * ATTN_BLOCK, (j + 1) * ATTN_BLOCK)
        o_ref[0, cur, :] = lanes_by_head(o_heads).astype(BF16)
        lse_ref[0, cur, :] = lanes_by_head([jnp.broadcast_to(x, (ATTN_BLOCK, LANES)) for x in lse])


def _merge_ffn2_kernel(h1_ref, o0_ref, o1_ref, o2_ref, l0_ref, l1_ref, l2_ref, ga_ref, gpool_ref,
                       wa_ref, wo_ref, g2_ref, w1_ref, w3_ref, w2_ref, gf_ref,
                       out_ref, hid_ref, nat_ref):
    tm = h1_ref.shape[0]

    def natural_order(ref, r, slot):
        if r == 1:
            return ref[0, 0].astype(F32)
        halves = []
        for half in range(GROUP_WIDTH // LANES):
            lanes = slice(half * LANES, (half + 1) * LANES)
            for c in range(r):
                nat_ref[slot + half, pl.ds(c, tm // r, stride=r), :] = ref[0, c, :, lanes].astype(F32)
            halves.append(nat_ref[slot + half])
        return jnp.concatenate(halves, axis=1)

    outs, lses = [], []
    slot = 0
    halves_per_slab = GROUP_WIDTH // LANES
    for (_, r), o_ref, l_ref in zip(DILATED_GROUPS, (o0_ref, o1_ref, o2_ref), (l0_ref, l1_ref, l2_ref)):
        outs.append(natural_order(o_ref, r, slot))
        lses.append(natural_order(l_ref, r, slot + halves_per_slab))
        if r != 1:
            slot += 2 * halves_per_slab
    top = functools.reduce(jnp.maximum, lses)
    weights = [jnp.exp2(l - top) for l in lses]
    o = sum(w * og for w, og in zip(weights, outs)) / sum(weights)
    y_attn = jnp.dot(o.astype(BF16), wa_ref[...], preferred_element_type=F32)

    merged = ga_ref[...].astype(F32) * y_attn + gpool_ref[...].astype(F32)
    h2 = h1_ref[...] + jnp.dot(merged.astype(BF16), wo_ref[...], preferred_element_type=F32)
    u2 = _rms(h2, g2_ref[...]).astype(BF16)
    h3 = h2 + 0.5 * _swiglu(u2, w1_ref, w3_ref, w2_ref, hid_ref)
    out_ref[...] = _rms(h3, gf_ref[...])


def _resident(shape):
    return pl.BlockSpec(shape, lambda *_: (0,) * len(shape), pipeline_mode=pl.Buffered(1))


def _pallas_call(body, name, grid, inputs, outputs, scratch):
    def window_bytes(aval, spec):
        buffers = 1 if isinstance(spec.pipeline_mode, pl.Buffered) and spec.pipeline_mode.buffer_count == 1 else 2
        elems = 1
        for d in spec.block_shape:
            elems *= d
        return buffers * elems * jnp.dtype(aval.dtype).itemsize

    request = COMPILER_TEMP_BYTES + sum(window_bytes(a, spec) for a, spec in (*inputs, *outputs))
    for buf in scratch:
        elems = 1
        for d in buf.shape:
            elems *= d
        request += elems * jnp.dtype(buf.dtype).itemsize
    assert request <= V7X_SCOPED_VMEM_MAX_BYTES, (name, request)
    return pl.pallas_call(
        body, name=name, grid=grid,
        in_specs=[spec for _, spec in inputs], out_specs=[spec for _, spec in outputs],
        out_shape=[shape for shape, _ in outputs], scratch_shapes=scratch,
        compiler_params=pltpu.CompilerParams(dimension_semantics=("arbitrary",) * len(grid),
                                             vmem_limit_bytes=V7X_SCOPED_VMEM_MAX_BYTES),
    )(*[a for a, _ in inputs])


def _rope_tables(seq):
    half = ROT_DIM // 2
    j = np.arange(LANES) % HEAD_DIM
    inv = np.where(j < ROT_DIM, ROPE_THETA ** (-(2.0 * (j % half)) / ROT_DIM), 0.0)
    sign = np.where(j < half, -1.0, 1.0)
    ang = np.arange(seq, dtype=np.float64)[:, None] * inv[None, :]
    return jnp.asarray(np.cos(ang), F32), jnp.asarray(np.sin(ang) * sign[None, :], F32)


def _layer(h, B, S, ffn1_norm, ffn1_w1, ffn1_w3, ffn1_w2, mix_norm, w_in, w_branch_attn, w_branch_pool,
           pool_w, pool_scale, w_out, ffn2_norm, ffn2_w1, ffn2_w3, ffn2_w2, final_gain, cos_t, sin_t):
    T, D = h.shape
    d_ff = ffn1_w1.shape[1]
    in_width = w_in.shape[1]
    assert in_width == 3 * N_GROUPS * GROUP_WIDTH + POOL_WIDTH + 2 * D
    tm = TOKEN_TILE
    tiles_per_seq = S // tm
    n_tiles = T // tm
    bf = lambda w: w.astype(BF16)
    row = lambda g: g.reshape(1, -1).astype(F32)
    tok = lambda width: pl.BlockSpec((tm, width), lambda i: (i, 0))
    table = pl.BlockSpec((tm, LANES), lambda i: (i % tiles_per_seq, 0))
    whole = lambda a: (a, _resident(a.shape))

    def residue_major(r):
        return pl.BlockSpec((1, r, tm // r, GROUP_WIDTH), lambda i: (i // tiles_per_seq, 0, i % tiles_per_seq, 0))

    qkv_outputs = [(jax.ShapeDtypeStruct((B, r, S // r, GROUP_WIDTH), BF16), residue_major(r))
                   for _, r in DILATED_GROUPS] * 3
    slab_slots = sum(_slots_per_half(r) for _, r in DILATED_GROUPS) * GROUP_WIDTH // LANES
    cast_jobs = ((ffn2_w1, D // n_tiles), (ffn2_w3, D // n_tiles), (ffn2_w2, 4 * D // n_tiles),
                 (w_out, D // n_tiles), (w_branch_attn, BF16_SUBLANES))
    assert all(rows % BF16_SUBLANES == 0 and w.shape[0] % rows == 0 and w.shape[0] // rows <= n_tiles
               for w, rows in cast_jobs)
    cast_specs = [pl.BlockSpec((rows, w.shape[1]), functools.partial(
        lambda i, blocks: (jnp.minimum(i, blocks - 1), 0), blocks=w.shape[0] // rows)) for w, rows in cast_jobs]
    outs = _pallas_call(
        functools.partial(_ffn1_proj_kernel, tiles_per_seq=tiles_per_seq, n_cast=len(cast_jobs)),
        "ffn1_proj", (n_tiles,),
        inputs=[(h, tok(D)), whole(row(ffn1_norm)), whole(bf(ffn1_w1)), whole(bf(ffn1_w3)), whole(bf(ffn1_w2)),
                whole(row(mix_norm)), whole(bf(w_in)), (cos_t, table), (sin_t, table),
                whole(bf(pool_w)), whole(row(pool_scale)), whole(bf(w_branch_pool))]
        + [(w, spec) for (w, _), spec in zip(cast_jobs, cast_specs)],
        outputs=[(jax.ShapeDtypeStruct((T, D), F32), tok(D))] + qkv_outputs
        + [(jax.ShapeDtypeStruct((T, D), BF16), tok(D))] * 2
        + [(jax.ShapeDtypeStruct(w.shape, BF16), spec) for (w, _), spec in zip(cast_jobs, cast_specs)],
        scratch=[pltpu.VMEM((tm, d_ff), BF16), pltpu.VMEM((slab_slots, tm, LANES), F32),
                 pltpu.VMEM((POOL_HALO + tm, POOL_WIDTH), F32)])
    h1 = outs[0]
    qs, ks, vs = outs[1:4], outs[4:7], outs[7:10]
    gate_attn, gated_pool = outs[10], outs[11]
    ffn2_w1_bf, ffn2_w3_bf, ffn2_w2_bf, w_out_bf, w_branch_attn_bf = outs[12:]

    attn_o, attn_lse = [], []
    rows = ATTN_ROWS
    blocks_per_step = rows // ATTN_BLOCK
    steps_per_batch = S // rows
    for g, (window, r) in enumerate(DILATED_GROUPS):
        L = S // r
        flat = lambda a: a.reshape(B, S, GROUP_WIDTH)
        cur = pl.BlockSpec((1, rows, GROUP_WIDTH), lambda b, s: (b, s, 0))
        prev = pl.BlockSpec((1, ATTN_BLOCK, GROUP_WIDTH),
                            lambda b, s: (b, jnp.maximum(s * blocks_per_step - 1, 0), 0))
        o_g, lse_g = _pallas_call(
            functools.partial(_attn_kernel, blocks_per_residue=L // ATTN_BLOCK, n_back=window // r),
            f"dilated_attn_g{g}", (B, steps_per_batch),
            inputs=[(flat(qs[g]), cur), (flat(ks[g]), cur), (flat(vs[g]), cur), (flat(ks[g]), prev),
                    (flat(vs[g]), prev)],
            outputs=[(jax.ShapeDtypeStruct((B, S, GROUP_WIDTH), BF16), cur),
                     (jax.ShapeDtypeStruct((B, S, GROUP_WIDTH), F32), cur)],
            scratch=[pltpu.VMEM((2, HEADS_PER_GROUP * ATTN_BLOCK, 2 * ATTN_BLOCK), F32),
                     pltpu.VMEM((HEADS_PER_GROUP * ATTN_BLOCK, 2 * ATTN_BLOCK), BF16)])
        attn_o.append(o_g.reshape(B, r, L, GROUP_WIDTH))
        attn_lse.append(lse_g.reshape(B, r, L, GROUP_WIDTH))

    o_specs = [residue_major(r) for _, r in DILATED_GROUPS]
    (out,) = _pallas_call(
        _merge_ffn2_kernel, "merge_ffn2", (n_tiles,),
        inputs=[(h1, tok(D))] + list(zip(attn_o, o_specs)) + list(zip(attn_lse, o_specs))
        + [(gate_attn, tok(D)), (gated_pool, tok(D)), whole(w_branch_attn_bf), whole(w_out_bf),
           whole(row(ffn2_norm)), whole(ffn2_w1_bf), whole(ffn2_w3_bf), whole(ffn2_w2_bf), whole(row(final_gain))],
        outputs=[(jax.ShapeDtypeStruct((T, D), F32), tok(D))],
        scratch=[pltpu.VMEM((tm, d_ff), BF16),
                 pltpu.VMEM((2 * (N_GROUPS - 1) * GROUP_WIDTH // LANES, tm, LANES), F32)])
    return out


def kernel(x, ffn1_norm, ffn1_w1, ffn1_w3, ffn1_w2, mix_norm, w_in, w_branch_attn, w_branch_pool, pool_w,
           pool_scale, w_out, ffn2_norm, ffn2_w1, ffn2_w3, ffn2_w2, final_norm):
    B, S, D = x.shape
    depth = ffn1_norm.shape[0]
    assert S % TOKEN_TILE == 0 and S % ATTN_ROWS == 0
    assert all(window // r == ATTN_BLOCK and (S // r) % ATTN_BLOCK == 0 for window, r in DILATED_GROUPS)
    assert all(TOKEN_TILE % (16 * r) == 0 for _, r in DILATED_GROUPS)
    cos_t, sin_t = _rope_tables(S)
    h = x.reshape(B * S, D)
    for l in range(depth):
        assert l == depth - 1, "only the last layer's output norm is implemented"
        h = _layer(h, B, S, ffn1_norm[l], ffn1_w1[l], ffn1_w3[l], ffn1_w2[l], mix_norm[l], w_in[l],
                   w_branch_attn[l], w_branch_pool[l], pool_w[l], pool_scale[l], w_out[l], ffn2_norm[l],
                   ffn2_w1[l], ffn2_w3[l], ffn2_w2[l], final_norm, cos_t, sin_t)
    return h.reshape(B, S, D)
```

```python
import functools

import jax
import jax.numpy as jnp
import numpy as np
from jax import lax
from jax.experimental import pallas as pl
from jax.experimental.pallas import tpu as pltpu

F32 = jnp.float32
BF16 = jnp.bfloat16

HEAD_DIM = 64
HEADS_PER_GROUP = 4
GROUP_WIDTH = HEADS_PER_GROUP * HEAD_DIM
DILATED_GROUPS = ((128, 1), (512, 4), (2048, 16))
N_GROUPS = len(DILATED_GROUPS)
ATTN_BLOCK = 128
ROT_DIM = HEAD_DIM // 4
ROPE_THETA = 500000.0
POOL_WINDOWS = (2, 4, 8, 16)
POOL_GROUP_WIDTH = 128
POOL_WIDTH = len(POOL_WINDOWS) * POOL_GROUP_WIDTH
POOL_HALO = max(POOL_WINDOWS)
RMS_EPS = 1e-6
SCORE_SCALE_LOG2 = HEAD_DIM ** -0.5 * 1.4426950408889634

LANES = 128
BF16_SUBLANES = 16
V7X_SCOPED_VMEM_MAX_BYTES = 60000 * 1024
COMPILER_TEMP_BYTES = 4 * 1024 * 1024

TOKEN_TILE = 512
FF_CHUNK = 256
ATTN_ROWS = 4096
DEINTERLEAVE_STEP = 4


def _rms(x, gain):
    return x * lax.rsqrt(jnp.mean(x * x, axis=-1, keepdims=True) + RMS_EPS) * gain


def _sigmoid(x):
    return 0.5 * jnp.tanh(0.5 * x) + 0.5


def _swiglu(u, w1_ref, w3_ref, w2_ref, hid_ref):
    d_ff = w1_ref.shape[1]
    for f in range(d_ff // FF_CHUNK):
        cols = slice(f * FF_CHUNK, (f + 1) * FF_CHUNK)
        ab = jnp.dot(u, jnp.concatenate([w1_ref[:, cols], w3_ref[:, cols]], axis=1), preferred_element_type=F32)
        a, b = ab[:, :FF_CHUNK], ab[:, FF_CHUNK:]
        half = 0.5 * a
        hid_ref[:, cols] = ((half * jnp.tanh(half) + half) * b).astype(BF16)
    return jnp.dot(hid_ref[...], w2_ref[...], preferred_element_type=F32)


def _slots_per_half(r):
    return 0 if r == 1 else 1 if r <= DEINTERLEAVE_STEP else 2


def _rope(x, cos, sin):
    j = lax.broadcasted_iota(jnp.int32, x.shape, 1) % HEAD_DIM
    half = ROT_DIM // 2
    partner = jnp.where(j < half, pltpu.roll(x, LANES - half, axis=1), pltpu.roll(x, half, axis=1))
    return jnp.where(j < ROT_DIM, x * cos + partner * sin, x)


def _pool_deltas(pext_ref, first_pos):
    tm = pext_ref.shape[0] - POOL_HALO
    pos = first_pos + lax.broadcasted_iota(jnp.int32, (tm, 1), 0)
    deltas = []
    for gi, win in enumerate(POOL_WINDOWS):
        lanes = slice(gi * POOL_GROUP_WIDTH, (gi + 1) * POOL_GROUP_WIDTH)
        own = pext_ref[POOL_HALO:POOL_HALO + tm, lanes]
        total = own
        for back in range(1, win):
            total = total + pext_ref[POOL_HALO - back:POOL_HALO - back + tm, lanes]
        count = jnp.minimum(pos + 1, win).astype(F32)
        deltas.append((total / count - own).astype(BF16))
    return deltas


def _pool_project(deltas, poolw_ref, pscale_ref, wb_ref):
    pooled = []
    for gi, d in enumerate(deltas):
        lanes = slice(gi * POOL_GROUP_WIDTH, (gi + 1) * POOL_GROUP_WIDTH)
        pooled.append(jnp.dot(d, poolw_ref[gi], preferred_element_type=F32) * pscale_ref[:, lanes])
    return jnp.dot(jnp.concatenate(pooled, axis=1).astype(BF16), wb_ref[...], preferred_element_type=F32)


def _ffn1_proj_kernel(x_ref, g1_ref, w1_ref, w3_ref, w2_ref, gm_ref, win_ref, cos_ref, sin_ref,
                      poolw_ref, pscale_ref, wb_ref, *rest, tiles_per_seq, n_cast):
    cast_in, rest = rest[:n_cast], rest[n_cast:]
    (h1_ref, q0_ref, q1_ref, q2_ref, k0_ref, k1_ref, k2_ref, v0_ref, v1_ref, v2_ref, ga_ref, gpool_ref), rest = (
        rest[:12], rest[12:])
    cast_out, (hid_ref, slab_ref, pext_ref) = rest[:n_cast], rest[n_cast:]
    for w_ref, w_bf_ref in zip(cast_in, cast_out):
        w_bf_ref[...] = w_ref[...].astype(BF16)
    tm = x_ref.shape[0]
    d_model = x_ref.shape[1]
    t = pl.program_id(0) % tiles_per_seq

    @pl.when(t == 0)
    def _():
        pext_ref[0:POOL_HALO, :] = jnp.zeros((POOL_HALO, POOL_WIDTH), F32)

    x = x_ref[...]
    u = _rms(x, g1_ref[...]).astype(BF16)
    h1 = x + 0.5 * _swiglu(u, w1_ref, w3_ref, w2_ref, hid_ref)
    h1_ref[...] = h1
    um = _rms(h1, gm_ref[...]).astype(BF16)
    cos = cos_ref[...]
    sin = sin_ref[...]

    def project(col, width):
        return jnp.dot(um, win_ref[:, col:col + width], preferred_element_type=F32)

    def rope(p):
        return jnp.concatenate([_rope(p[:, :LANES], cos, sin), _rope(p[:, LANES:], cos, sin)], axis=1)

    qkv_width = 3 * N_GROUPS * GROUP_WIDTH
    gate_col = qkv_width + POOL_WIDTH
    halves = GROUP_WIDTH // LANES
    first_slot = [halves * sum(_slots_per_half(r) for _, r in DILATED_GROUPS[:g]) for g in range(N_GROUPS)]

    def store_residue_major(p, ref, r, scratch_slot):
        if r == 1:
            ref[0, 0] = p.astype(BF16)
            return
        for half in range(halves):
            lanes = slice(half * LANES, (half + 1) * LANES)
            first = scratch_slot + half * _slots_per_half(r)
            slab_ref[first] = p[:, lanes]
            if _slots_per_half(r) == 1:
                for c in range(r):
                    ref[0, c, :, lanes] = slab_ref[first, pl.ds(c, tm // r, stride=r), :].astype(BF16)
                continue
            step, rest = DEINTERLEAVE_STEP, r // DEINTERLEAVE_STEP
            for c0 in range(step):
                slab_ref[first + 1, c0 * (tm // step):(c0 + 1) * (tm // step), :] = (
                    slab_ref[first, pl.ds(c0, tm // step, stride=step), :])
            for c0 in range(step):
                for c1 in range(rest):
                    ref[0, c1 * step + c0, :, lanes] = slab_ref[
                        first + 1, pl.ds(c0 * (tm // step) + c1, tm // r, stride=rest), :].astype(BF16)

    def qkv(kind, refs, group_order):
        both = project(("q", "k", "v").index(kind) * N_GROUPS * GROUP_WIDTH, N_GROUPS * GROUP_WIDTH)
        for g in group_order:
            p = both[:, g * GROUP_WIDTH:(g + 1) * GROUP_WIDTH]
            if kind == "q":
                p = rope(p) * SCORE_SCALE_LOG2
            elif kind == "k":
                p = rope(p)
            store_residue_major(p, refs[g], DILATED_GROUPS[g][1], first_slot[g])

    pext_ref[POOL_HALO:, :] = project(qkv_width, POOL_WIDTH)
    deltas = _pool_deltas(pext_ref, t * tm)
    pext_ref[0:POOL_HALO, :] = pext_ref[tm:tm + POOL_HALO, :]
    pair = 2 * GROUP_WIDTH
    for s in range(d_model // pair):
        lanes = slice(s * pair, (s + 1) * pair)
        ga_ref[:, lanes] = _sigmoid(project(gate_col + s * pair, pair)).astype(BF16)
    qkv("q", (q0_ref, q1_ref, q2_ref), range(N_GROUPS))
    y_pool = _pool_project(deltas, poolw_ref, pscale_ref, wb_ref)
    for s in range(d_model // pair):
        lanes = slice(s * pair, (s + 1) * pair)
        gate_pool = _sigmoid(project(gate_col + d_model + s * pair, pair))
        gpool_ref[:, lanes] = (gate_pool * y_pool[:, lanes]).astype(BF16)
    qkv("k", (k0_ref, k1_ref, k2_ref), reversed(range(N_GROUPS)))
    qkv("v", (v0_ref, v1_ref, v2_ref), reversed(range(N_GROUPS)))


def _attn_kernel(q_ref, k_ref, v_ref, o_ref, lse_ref, s_ref, p_ref, *, blocks_per_residue, n_back):
    rows = q_ref.shape[1]
    nblk = rows // ATTN_BLOCK
    assert nblk % blocks_per_residue == 0
    qi = lax.broadcasted_iota(jnp.int32, (ATTN_BLOCK, 2 * ATTN_BLOCK), 0)
    kj = lax.broadcasted_iota(jnp.int32, (ATTN_BLOCK, 2 * ATTN_BLOCK), 1)
    dist = qi + ATTN_BLOCK - kj
    band = (dist >= 0) & (dist <= n_back)
    head_of_lane = lax.broadcasted_iota(jnp.int32, (1, GROUP_WIDTH), 1) // HEAD_DIM
    heads = range(HEADS_PER_GROUP)
    heads_per_vreg = LANES // HEAD_DIM

    def head_rows(h):
        return slice(h * ATTN_BLOCK, (h + 1) * ATTN_BLOCK)

    def key_rows(j):
        first = j - 1 if j % blocks_per_residue else j
        return slice(first * ATTN_BLOCK, (j + 1) * ATTN_BLOCK)

    def n_keys(j):
        rows_j = key_rows(j)
        return rows_j.stop - rows_j.start

    def scores(j):
        qb = q_ref[0, j * ATTN_BLOCK:(j + 1) * ATTN_BLOCK, :]
        qs = jnp.concatenate([jnp.where(head_of_lane == h, qb, jnp.zeros_like(qb)) for h in heads], axis=0)
        return lax.dot_general(qs, k_ref[0, key_rows(j), :], (((1,), (1,)), ((), ())),
                               preferred_element_type=F32)

    def lanes_by_head(per_head):
        first_in_vreg = (head_of_lane[:, :LANES] % heads_per_vreg) == 0
        vregs = [jnp.where(first_in_vreg, per_head[v * heads_per_vreg], per_head[v * heads_per_vreg + 1])
                 for v in range(GROUP_WIDTH // LANES)]
        return jnp.concatenate(vregs, axis=1)

    s_ref[0, :, 0:n_keys(0)] = scores(0)
    for j in range(nblk):
        if j + 1 < nblk:
            s_ref[(j + 1) % 2, :, 0:n_keys(j + 1)] = scores(j + 1)
        width = n_keys(j)
        allowed = band[:, 2 * ATTN_BLOCK - width:]
        inv_l, lse = [], []
        for h in heads:
            s = jnp.where(allowed, s_ref[j % 2, head_rows(h), 0:width], -jnp.inf)
            m = jnp.max(s, axis=-1, keepdims=True)
            p = jnp.exp2(s - m)
            l = jnp.sum(p, axis=-1, keepdims=True)
            p_ref[head_rows(h), 0:width] = p.astype(BF16)
            inv_l.append(1.0 / l)
            lse.append(m + jnp.log2(l))
        pv = jnp.dot(p_ref[:, 0:width], v_ref[0, key_rows(j), :], preferred_element_type=F32)
        o_heads = [pv[head_rows(h), (h // heads_per_vreg) * LANES:(h // heads_per_vreg + 1) * LANES] * inv_l[h]
                   for h in heads]
        cur = slice(j * ATTN_BLOCK, (j + 1) * ATTN_BLOCK)
        o_ref[0, cur, :] = lanes_by_head(o_heads).astype(BF16)
        lse_ref[0, cur, :] = lanes_by_head([jnp.broadcast_to(x, (ATTN_BLOCK, LANES)) for x in lse])


def _merge_ffn2_kernel(h1_ref, o0_ref, o1_ref, o2_ref, l0_ref, l1_ref, l2_ref, ga_ref, gpool_ref,
                       wa_ref, wo_ref, g2_ref, w1_ref, w3_ref, w2_ref, gf_ref,
                       out_ref, hid_ref, nat_ref):
    tm = h1_ref.shape[0]

    def natural_order(ref, r, slot):
        if r == 1:
            return ref[0, 0].astype(F32)
        halves = []
        for half in range(GROUP_WIDTH // LANES):
            lanes = slice(half * LANES, (half + 1) * LANES)
            for c in range(r):
                nat_ref[slot + half, pl.ds(c, tm // r, stride=r), :] = ref[0, c, :, lanes].astype(F32)
            halves.append(nat_ref[slot + half])
        return jnp.concatenate(halves, axis=1)

    outs, lses = [], []
    slot = 0
    halves_per_slab = GROUP_WIDTH // LANES
    for (_, r), o_ref, l_ref in zip(DILATED_GROUPS, (o0_ref, o1_ref, o2_ref), (l0_ref, l1_ref, l2_ref)):
        outs.append(natural_order(o_ref, r, slot))
        lses.append(natural_order(l_ref, r, slot + halves_per_slab))
        if r != 1:
            slot += 2 * halves_per_slab
    top = functools.reduce(jnp.maximum, lses)
    weights = [jnp.exp2(l - top) for l in lses]
    o = sum(w * og for w, og in zip(weights, outs)) / sum(weights)
    y_attn = jnp.dot(o.astype(BF16), wa_ref[...], preferred_element_type=F32)

    merged = ga_ref[...].astype(F32) * y_attn + gpool_ref[...].astype(F32)
    h2 = h1_ref[...] + jnp.dot(merged.astype(BF16), wo_ref[...], preferred_element_type=F32)
    u2 = _rms(h2, g2_ref[...]).astype(BF16)
    h3 = h2 + 0.5 * _swiglu(u2, w1_ref, w3_ref, w2_ref, hid_ref)
    out_ref[...] = _rms(h3, gf_ref[...])


def _resident(shape):
    return pl.BlockSpec(shape, lambda *_: (0,) * len(shape), pipeline_mode=pl.Buffered(1))


def _pallas_call(body, name, grid, inputs, outputs, scratch):
    def window_bytes(aval, spec):
        buffers = 1 if isinstance(spec.pipeline_mode, pl.Buffered) and spec.pipeline_mode.buffer_count == 1 else 2
        elems = 1
        for d in spec.block_shape:
            elems *= d
        return buffers * elems * jnp.dtype(aval.dtype).itemsize

    request = COMPILER_TEMP_BYTES + sum(window_bytes(a, spec) for a, spec in (*inputs, *outputs))
    for buf in scratch:
        elems = 1
        for d in buf.shape:
            elems *= d
        request += elems * jnp.dtype(buf.dtype).itemsize
    assert request <= V7X_SCOPED_VMEM_MAX_BYTES, (name, request)
    return pl.pallas_call(
        body, name=name, grid=grid,
        in_specs=[spec for _, spec in inputs], out_specs=[spec for _, spec in outputs],
        out_shape=[shape for shape, _ in outputs], scratch_shapes=scratch,
        compiler_params=pltpu.CompilerParams(dimension_semantics=("arbitrary",) * len(grid),
                                             vmem_limit_bytes=V7X_SCOPED_VMEM_MAX_BYTES),
    )(*[a for a, _ in inputs])


def _rope_tables(seq):
    half = ROT_DIM // 2
    j = np.arange(LANES) % HEAD_DIM
    inv = np.where(j < ROT_DIM, ROPE_THETA ** (-(2.0 * (j % half)) / ROT_DIM), 0.0)
    sign = np.where(j < half, -1.0, 1.0)
    ang = np.arange(seq, dtype=np.float64)[:, None] * inv[None, :]
    return jnp.asarray(np.cos(ang), F32), jnp.asarray(np.sin(ang) * sign[None, :], F32)


def _layer(h, B, S, ffn1_norm, ffn1_w1, ffn1_w3, ffn1_w2, mix_norm, w_in, w_branch_attn, w_branch_pool,
           pool_w, pool_scale, w_out, ffn2_norm, ffn2_w1, ffn2_w3, ffn2_w2, final_gain, cos_t, sin_t):
    T, D = h.shape
    d_ff = ffn1_w1.shape[1]
    in_width = w_in.shape[1]
    assert in_width == 3 * N_GROUPS * GROUP_WIDTH + POOL_WIDTH + 2 * D
    tm = TOKEN_TILE
    tiles_per_seq = S // tm
    n_tiles = T // tm
    bf = lambda w: w.astype(BF16)
    row = lambda g: g.reshape(1, -1).astype(F32)
    tok = lambda width: pl.BlockSpec((tm, width), lambda i: (i, 0))
    table = pl.BlockSpec((tm, LANES), lambda i: (i % tiles_per_seq, 0))
    whole = lambda a: (a, _resident(a.shape))

    def residue_major(r):
        return pl.BlockSpec((1, r, tm // r, GROUP_WIDTH), lambda i: (i // tiles_per_seq, 0, i % tiles_per_seq, 0))

    qkv_outputs = [(jax.ShapeDtypeStruct((B, r, S // r, GROUP_WIDTH), BF16), residue_major(r))
                   for _, r in DILATED_GROUPS] * 3
    slab_slots = sum(_slots_per_half(r) for _, r in DILATED_GROUPS) * GROUP_WIDTH // LANES
    cast_jobs = ((ffn2_w1, D // n_tiles), (ffn2_w3, D // n_tiles), (ffn2_w2, 4 * D // n_tiles),
                 (w_out, D // n_tiles), (w_branch_attn, BF16_SUBLANES))
    assert all(rows % BF16_SUBLANES == 0 and w.shape[0] % rows == 0 and w.shape[0] // rows <= n_tiles
               for w, rows in cast_jobs)
    cast_specs = [pl.BlockSpec((rows, w.shape[1]), functools.partial(
        lambda i, blocks: (jnp.minimum(i, blocks - 1), 0), blocks=w.shape[0] // rows)) for w, rows in cast_jobs]
    outs = _pallas_call(
        functools.partial(_ffn1_proj_kernel, tiles_per_seq=tiles_per_seq, n_cast=len(cast_jobs)),
        "ffn1_proj", (n_tiles,),
        inputs=[(h, tok(D)), whole(row(ffn1_norm)), whole(bf(ffn1_w1)), whole(bf(ffn1_w3)), whole(bf(ffn1_w2)),
                whole(row(mix_norm)), whole(bf(w_in)), (cos_t, table), (sin_t, table),
                whole(bf(pool_w)), whole(row(pool_scale)), whole(bf(w_branch_pool))]
        + [(w, spec) for (w, _), spec in zip(cast_jobs, cast_specs)],
        outputs=[(jax.ShapeDtypeStruct((T, D), F32), tok(D))] + qkv_outputs
        + [(jax.ShapeDtypeStruct((T, D), BF16), tok(D))] * 2
        + [(jax.ShapeDtypeStruct(w.shape, BF16), spec) for (w, _), spec in zip(cast_jobs, cast_specs)],
        scratch=[pltpu.VMEM((tm, d_ff), BF16), pltpu.VMEM((slab_slots, tm, LANES), F32),
                 pltpu.VMEM((POOL_HALO + tm, POOL_WIDTH), F32)])
    h1 = outs[0]
    qs, ks, vs = outs[1:4], outs[4:7], outs[7:10]
    gate_attn, gated_pool = outs[10], outs[11]
    ffn2_w1_bf, ffn2_w3_bf, ffn2_w2_bf, w_out_bf, w_branch_attn_bf = outs[12:]

    attn_o, attn_lse = [], []
    rows = ATTN_ROWS
    for g, (window, r) in enumerate(DILATED_GROUPS):
        L = S // r
        flat = lambda a: a.reshape(B, S, GROUP_WIDTH)
        cur = pl.BlockSpec((1, rows, GROUP_WIDTH), lambda b, s: (b, s, 0))
        o_g, lse_g = _pallas_call(
            functools.partial(_attn_kernel, blocks_per_residue=L // ATTN_BLOCK, n_back=window // r),
            f"dilated_attn_g{g}", (B, S // rows),
            inputs=[(flat(qs[g]), cur), (flat(ks[g]), cur), (flat(vs[g]), cur)],
            outputs=[(jax.ShapeDtypeStruct((B, S, GROUP_WIDTH), BF16), cur),
                     (jax.ShapeDtypeStruct((B, S, GROUP_WIDTH), F32), cur)],
            scratch=[pltpu.VMEM((2, HEADS_PER_GROUP * ATTN_BLOCK, 2 * ATTN_BLOCK), F32),
                     pltpu.VMEM((HEADS_PER_GROUP * ATTN_BLOCK, 2 * ATTN_BLOCK), BF16)])
        attn_o.append(o_g.reshape(B, r, L, GROUP_WIDTH))
        attn_lse.append(lse_g.reshape(B, r, L, GROUP_WIDTH))

    o_specs = [residue_major(r) for _, r in DILATED_GROUPS]
    (out,) = _pallas_call(
        _merge_ffn2_kernel, "merge_ffn2", (n_tiles,),
        inputs=[(h1, tok(D))] + list(zip(attn_o, o_specs)) + list(zip(attn_lse, o_specs))
        + [(gate_attn, tok(D)), (gated_pool, tok(D)), whole(w_branch_attn_bf), whole(w_out_bf),
           whole(row(ffn2_norm)), whole(ffn2_w1_bf), whole(ffn2_w3_bf), whole(ffn2_w2_bf), whole(row(final_gain))],
        outputs=[(jax.ShapeDtypeStruct((T, D), F32), tok(D))],
        scratch=[pltpu.VMEM((tm, d_ff), BF16),
                 pltpu.VMEM((2 * (N_GROUPS - 1) * GROUP_WIDTH // LANES, tm, LANES), F32)])
    return out


def kernel(x, ffn1_norm, ffn1_w1, ffn1_w3, ffn1_w2, mix_norm, w_in, w_branch_attn, w_branch_pool, pool_w,
           pool_scale, w_out, ffn2_norm, ffn2_w1, ffn2_w3, ffn2_w2, final_norm):
    B, S, D = x.shape
    depth = ffn1_norm.shape[0]
    assert S % TOKEN_TILE == 0 and S % ATTN_ROWS == 0
    assert all(window // r == ATTN_BLOCK and (S // r) % ATTN_BLOCK == 0 for window, r in DILATED_GROUPS)
    assert all(TOKEN_TILE % (16 * r) == 0 for _, r in DILATED_GROUPS)
    cos_t, sin_t = _rope_tables(S)
    h = x.reshape(B * S, D)
    for l in range(depth):
        assert l == depth - 1, "only the last layer's output norm is implemented"
        h = _layer(h, B, S, ffn1_norm[l], ffn1_w1[l], ffn1_w3[l], ffn1_w2[l], mix_norm[l], w_in[l],
                   w_branch_attn[l], w_branch_pool[l], pool_w[l], pool_scale[l], w_out[l], ffn2_norm[l],
                   ffn2_w1[l], ffn2_w3[l], ffn2_w2[l], final_norm, cos_t, sin_t)
    return h.reshape(B, S, D)
```

```python
import functools

import jax
import jax.numpy as jnp
import numpy as np
from jax import lax
from jax.experimental import pallas as pl
from jax.experimental.pallas import tpu as pltpu

F32 = jnp.float32
BF16 = jnp.bfloat16

HEAD_DIM = 64
HEADS_PER_GROUP = 4
GROUP_WIDTH = HEADS_PER_GROUP * HEAD_DIM
DILATED_GROUPS = ((128, 1), (512, 4), (2048, 16))
N_GROUPS = len(DILATED_GROUPS)
ATTN_BLOCK = 128
ROT_DIM = HEAD_DIM // 4
ROPE_THETA = 500000.0
POOL_WINDOWS = (2, 4, 8, 16)
POOL_GROUP_WIDTH = 128
POOL_WIDTH = len(POOL_WINDOWS) * POOL_GROUP_WIDTH
POOL_HALO = max(POOL_WINDOWS)
RMS_EPS = 1e-6
SCORE_SCALE_LOG2 = HEAD_DIM ** -0.5 * 1.4426950408889634

LANES = 128
BF16_SUBLANES = 16
V7X_SCOPED_VMEM_MAX_BYTES = 60000 * 1024
COMPILER_TEMP_BYTES = 4 * 1024 * 1024

TOKEN_TILE = 512
FFN_TOKEN_TILE = 1024
PROJ_TOKEN_TILE = 1024
FF_CHUNK = 256
ATTN_ROWS = 4096
DEINTERLEAVE_STEP = 4


def _rms(x, gain):
    return x * lax.rsqrt(jnp.mean(x * x, axis=-1, keepdims=True) + RMS_EPS) * gain


def _sigmoid(x):
    return 0.5 * jnp.tanh(0.5 * x) + 0.5


def _swiglu(u, w1_ref, w3_ref, w2_ref, hid_ref):
    d_ff = w1_ref.shape[1]
    for f in range(d_ff // FF_CHUNK):
        cols = slice(f * FF_CHUNK, (f + 1) * FF_CHUNK)
        ab = jnp.dot(u, jnp.concatenate([w1_ref[:, cols], w3_ref[:, cols]], axis=1), preferred_element_type=F32)
        a, b = ab[:, :FF_CHUNK], ab[:, FF_CHUNK:]
        half = 0.5 * a
        hid_ref[:, cols] = ((half * jnp.tanh(half) + half) * b).astype(BF16)
    return jnp.dot(hid_ref[...], w2_ref[...], preferred_element_type=F32)


def _slots_per_half(r):
    return 0 if r == 1 else 1 if r <= DEINTERLEAVE_STEP else 2


def _rope(x, cos, sin):
    j = lax.broadcasted_iota(jnp.int32, x.shape, 1) % HEAD_DIM
    half = ROT_DIM // 2
    partner = jnp.where(j < half, pltpu.roll(x, LANES - half, axis=1), pltpu.roll(x, half, axis=1))
    return jnp.where(j < ROT_DIM, x * cos + partner * sin, x)


def _pool_deltas(pext_ref, first_pos):
    tm = pext_ref.shape[0] - POOL_HALO
    pos = first_pos + lax.broadcasted_iota(jnp.int32, (tm, 1), 0)
    deltas = []
    for gi, win in enumerate(POOL_WINDOWS):
        lanes = slice(gi * POOL_GROUP_WIDTH, (gi + 1) * POOL_GROUP_WIDTH)
        own = pext_ref[POOL_HALO:POOL_HALO + tm, lanes]
        total = own
        for back in range(1, win):
            total = total + pext_ref[POOL_HALO - back:POOL_HALO - back + tm, lanes]
        count = jnp.minimum(pos + 1, win).astype(F32)
        deltas.append((total / count - own).astype(BF16))
    return deltas


def _pool_project(deltas, poolw_ref, pscale_ref, wb_ref):
    pooled = []
    for gi, d in enumerate(deltas):
        lanes = slice(gi * POOL_GROUP_WIDTH, (gi + 1) * POOL_GROUP_WIDTH)
        pooled.append(jnp.dot(d, poolw_ref[gi], preferred_element_type=F32) * pscale_ref[:, lanes])
    return jnp.dot(jnp.concatenate(pooled, axis=1).astype(BF16), wb_ref[...], preferred_element_type=F32)


def _ffn1_kernel(x_ref, g1_ref, w1_ref, w3_ref, w2_ref, gm_ref, *rest, n_cast):
    cast_in, (h1_ref, um_ref), cast_out = rest[:n_cast], rest[n_cast:n_cast + 2], rest[n_cast + 2:2 * n_cast + 2]
    (hid_ref,) = rest[2 * n_cast + 2:]
    for w_ref, w_bf_ref in zip(cast_in, cast_out):
        w_bf_ref[...] = w_ref[...].astype(BF16)
    x = x_ref[...]
    u = _rms(x, g1_ref[...]).astype(BF16)
    h1 = x + 0.5 * _swiglu(u, w1_ref, w3_ref, w2_ref, hid_ref)
    h1_ref[...] = h1
    um_ref[...] = _rms(h1, gm_ref[...]).astype(BF16)


def _proj_kernel(um_ref, win_ref, cos_ref, sin_ref, poolw_ref, pscale_ref, wb_ref,
                 q0_ref, q1_ref, q2_ref, k0_ref, k1_ref, k2_ref, v0_ref, v1_ref, v2_ref,
                 ga_ref, gpool_ref, slab_ref, pext_ref, *, tiles_per_seq):
    tm = um_ref.shape[0]
    d_model = um_ref.shape[1]
    t = pl.program_id(0) % tiles_per_seq

    @pl.when(t == 0)
    def _():
        pext_ref[0:POOL_HALO, :] = jnp.zeros((POOL_HALO, POOL_WIDTH), F32)

    um = um_ref[...]
    cos = cos_ref[...]
    sin = sin_ref[...]

    def project(col, width):
        return jnp.dot(um, win_ref[:, col:col + width], preferred_element_type=F32)

    def rope(p):
        return jnp.concatenate([_rope(p[:, :LANES], cos, sin), _rope(p[:, LANES:], cos, sin)], axis=1)

    qkv_width = 3 * N_GROUPS * GROUP_WIDTH
    gate_col = qkv_width + POOL_WIDTH
    halves = GROUP_WIDTH // LANES
    first_slot = [halves * sum(_slots_per_half(r) for _, r in DILATED_GROUPS[:g]) for g in range(N_GROUPS)]

    def store_residue_major(p, ref, r, scratch_slot):
        if r == 1:
            ref[0, 0] = p.astype(BF16)
            return
        for half in range(halves):
            lanes = slice(half * LANES, (half + 1) * LANES)
            first = scratch_slot + half * _slots_per_half(r)
            slab_ref[first] = p[:, lanes]
            if _slots_per_half(r) == 1:
                for c in range(r):
                    ref[0, c, :, lanes] = slab_ref[first, pl.ds(c, tm // r, stride=r), :].astype(BF16)
                continue
            step, rest = DEINTERLEAVE_STEP, r // DEINTERLEAVE_STEP
            for c0 in range(step):
                slab_ref[first + 1, c0 * (tm // step):(c0 + 1) * (tm // step), :] = (
                    slab_ref[first, pl.ds(c0, tm // step, stride=step), :])
            for c0 in range(step):
                for c1 in range(rest):
                    ref[0, c1 * step + c0, :, lanes] = slab_ref[
                        first + 1, pl.ds(c0 * (tm // step) + c1, tm // r, stride=rest), :].astype(BF16)

    def qkv(kind, refs, group_order):
        both = project(("q", "k", "v").index(kind) * N_GROUPS * GROUP_WIDTH, N_GROUPS * GROUP_WIDTH)
        for g in group_order:
            p = both[:, g * GROUP_WIDTH:(g + 1) * GROUP_WIDTH]
            if kind == "q":
                p = rope(p) * SCORE_SCALE_LOG2
            elif kind == "k":
                p = rope(p)
            store_residue_major(p, refs[g], DILATED_GROUPS[g][1], first_slot[g])

    pext_ref[POOL_HALO:, :] = project(qkv_width, POOL_WIDTH)
    deltas = _pool_deltas(pext_ref, t * tm)
    pext_ref[0:POOL_HALO, :] = pext_ref[tm:tm + POOL_HALO, :]
    pair = 2 * GROUP_WIDTH
    for s in range(d_model // pair):
        lanes = slice(s * pair, (s + 1) * pair)
        ga_ref[:, lanes] = _sigmoid(project(gate_col + s * pair, pair)).astype(BF16)
    qkv("q", (q0_ref, q1_ref, q2_ref), range(N_GROUPS))
    y_pool = _pool_project(deltas, poolw_ref, pscale_ref, wb_ref)
    for s in range(d_model // pair):
        lanes = slice(s * pair, (s + 1) * pair)
        gate_pool = _sigmoid(project(gate_col + d_model + s * pair, pair))
        gpool_ref[:, lanes] = (gate_pool * y_pool[:, lanes]).astype(BF16)
    qkv("k", (k0_ref, k1_ref, k2_ref), reversed(range(N_GROUPS)))
    qkv("v", (v0_ref, v1_ref, v2_ref), reversed(range(N_GROUPS)))


def _attn_kernel(q_ref, k_ref, v_ref, kh_ref, vh_ref, o_ref, lse_ref, s_ref, p_ref, *,
                 blocks_per_residue, n_back):
    rows = q_ref.shape[1]
    nblk = rows // ATTN_BLOCK
    step = pl.program_id(1)
    qi = lax.broadcasted_iota(jnp.int32, (ATTN_BLOCK, 2 * ATTN_BLOCK), 0)
    kj = lax.broadcasted_iota(jnp.int32, (ATTN_BLOCK, 2 * ATTN_BLOCK), 1)
    dist = qi + ATTN_BLOCK - kj
    band = (dist >= 0) & (dist <= n_back)
    in_current = kj >= ATTN_BLOCK
    head_of_lane = lax.broadcasted_iota(jnp.int32, (1, GROUP_WIDTH), 1) // HEAD_DIM
    heads = range(HEADS_PER_GROUP)
    heads_per_vreg = LANES // HEAD_DIM

    def head_rows(h):
        return slice(h * ATTN_BLOCK, (h + 1) * ATTN_BLOCK)

    def keys_values(j, ref, halo_ref):
        if j == 0:
            return jnp.concatenate([halo_ref[0], ref[0, 0:ATTN_BLOCK, :]], axis=0)
        return ref[0, (j - 1) * ATTN_BLOCK:(j + 1) * ATTN_BLOCK, :]

    def scores(j):
        qb = q_ref[0, j * ATTN_BLOCK:(j + 1) * ATTN_BLOCK, :]
        qs = jnp.concatenate([jnp.where(head_of_lane == h, qb, jnp.zeros_like(qb)) for h in heads], axis=0)
        return lax.dot_general(qs, keys_values(j, k_ref, kh_ref), (((1,), (1,)), ((), ())),
                               preferred_element_type=F32)

    def lanes_by_head(per_head):
        first_in_vreg = (head_of_lane[:, :LANES] % heads_per_vreg) == 0
        vregs = [jnp.where(first_in_vreg, per_head[v * heads_per_vreg], per_head[v * heads_per_vreg + 1])
                 for v in range(GROUP_WIDTH // LANES)]
        return jnp.concatenate(vregs, axis=1)

    s_ref[0] = scores(0)
    for j in range(nblk):
        if j + 1 < nblk:
            s_ref[(j + 1) % 2] = scores(j + 1)
        has_prev = ((step * nblk + j) % blocks_per_residue) != 0
        allowed = band & (in_current | has_prev)
        inv_l, lse = [], []
        for h in heads:
            s = jnp.where(allowed, s_ref[j % 2, head_rows(h), :], -jnp.inf)
            m = jnp.max(s, axis=-1, keepdims=True)
            p = jnp.exp2(s - m)
            l = jnp.sum(p, axis=-1, keepdims=True)
            p_ref[head_rows(h), :] = p.astype(BF16)
            inv_l.append(1.0 / l)
            lse.append(m + jnp.log2(l))
        pv = jnp.dot(p_ref[...], keys_values(j, v_ref, vh_ref), preferred_element_type=F32)
        o_heads = [pv[head_rows(h), (h // heads_per_vreg) * LANES:(h // heads_per_vreg + 1) * LANES] * inv_l[h]
                   for h in heads]
        cur = slice(j * ATTN_BLOCK, (j + 1) * ATTN_BLOCK)
        o_ref[0, cur, :] = lanes_by_head(o_heads).astype(BF16)
        lse_ref[0, cur, :] = lanes_by_head([jnp.broadcast_to(x, (ATTN_BLOCK, LANES)) for x in lse])


def _merge_ffn2_kernel(h1_ref, o0_ref, o1_ref, o2_ref, l0_ref, l1_ref, l2_ref, ga_ref, gpool_ref,
                       wa_ref, wo_ref, g2_ref, w1_ref, w3_ref, w2_ref, gf_ref,
                       out_ref, hid_ref, nat_ref):
    tm = h1_ref.shape[0]

    def natural_order(ref, r, slot):
        if r == 1:
            return ref[0, 0].astype(F32)
        halves = []
        for half in range(GROUP_WIDTH // LANES):
            lanes = slice(half * LANES, (half + 1) * LANES)
            for c in range(r):
                nat_ref[slot + half, pl.ds(c, tm // r, stride=r), :] = ref[0, c, :, lanes].astype(F32)
            halves.append(nat_ref[slot + half])
        return jnp.concatenate(halves, axis=1)

    outs, lses = [], []
    slot = 0
    halves_per_slab = GROUP_WIDTH // LANES
    for (_, r), o_ref, l_ref in zip(DILATED_GROUPS, (o0_ref, o1_ref, o2_ref), (l0_ref, l1_ref, l2_ref)):
        outs.append(natural_order(o_ref, r, slot))
        lses.append(natural_order(l_ref, r, slot + halves_per_slab))
        if r != 1:
            slot += 2 * halves_per_slab
    top = functools.reduce(jnp.maximum, lses)
    weights = [jnp.exp2(l - top) for l in lses]
    o = sum(w * og for w, og in zip(weights, outs)) / sum(weights)
    y_attn = jnp.dot(o.astype(BF16), wa_ref[...], preferred_element_type=F32)

    merged = ga_ref[...].astype(F32) * y_attn + gpool_ref[...].astype(F32)
    h2 = h1_ref[...] + jnp.dot(merged.astype(BF16), wo_ref[...], preferred_element_type=F32)
    u2 = _rms(h2, g2_ref[...]).astype(BF16)
    h3 = h2 + 0.5 * _swiglu(u2, w1_ref, w3_ref, w2_ref, hid_ref)
    out_ref[...] = _rms(h3, gf_ref[...])


def _resident(shape):
    return pl.BlockSpec(shape, lambda *_: (0,) * len(shape), pipeline_mode=pl.Buffered(1))


def _pallas_call(body, name, grid, inputs, outputs, scratch):
    def window_bytes(aval, spec):
        buffers = 1 if isinstance(spec.pipeline_mode, pl.Buffered) and spec.pipeline_mode.buffer_count == 1 else 2
        elems = 1
        for d in spec.block_shape:
            elems *= d
        return buffers * elems * jnp.dtype(aval.dtype).itemsize

    request = COMPILER_TEMP_BYTES + sum(window_bytes(a, spec) for a, spec in (*inputs, *outputs))
    for buf in scratch:
        elems = 1
        for d in buf.shape:
            elems *= d
        request += elems * jnp.dtype(buf.dtype).itemsize
    assert request <= V7X_SCOPED_VMEM_MAX_BYTES, (name, request)
    return pl.pallas_call(
        body, name=name, grid=grid,
        in_specs=[spec for _, spec in inputs], out_specs=[spec for _, spec in outputs],
        out_shape=[shape for shape, _ in outputs], scratch_shapes=scratch,
        compiler_params=pltpu.CompilerParams(dimension_semantics=("arbitrary",) * len(grid),
                                             vmem_limit_bytes=V7X_SCOPED_VMEM_MAX_BYTES),
    )(*[a for a, _ in inputs])


def _rope_tables(seq):
    half = ROT_DIM // 2
    j = np.arange(LANES) % HEAD_DIM
    inv = np.where(j < ROT_DIM, ROPE_THETA ** (-(2.0 * (j % half)) / ROT_DIM), 0.0)
    sign = np.where(j < half, -1.0, 1.0)
    ang = np.arange(seq, dtype=np.float64)[:, None] * inv[None, :]
    return jnp.asarray(np.cos(ang), F32), jnp.asarray(np.sin(ang) * sign[None, :], F32)


def _layer(h, B, S, ffn1_norm, ffn1_w1, ffn1_w3, ffn1_w2, mix_norm, w_in, w_branch_attn, w_branch_pool,
           pool_w, pool_scale, w_out, ffn2_norm, ffn2_w1, ffn2_w3, ffn2_w2, final_gain, cos_t, sin_t):
    T, D = h.shape
    d_ff = ffn1_w1.shape[1]
    in_width = w_in.shape[1]
    assert in_width == 3 * N_GROUPS * GROUP_WIDTH + POOL_WIDTH + 2 * D
    bf = lambda w: w.astype(BF16)
    row = lambda g: g.reshape(1, -1).astype(F32)
    whole = lambda a: (a, _resident(a.shape))

    tf = FFN_TOKEN_TILE
    ffn_steps = T // tf
    wide = lambda width: pl.BlockSpec((tf, width), lambda i: (i, 0))
    cast_weights = (ffn2_w1, ffn2_w3, ffn2_w2, w_out, w_branch_attn)
    assert all(w.shape[0] % (BF16_SUBLANES * ffn_steps) == 0 for w in cast_weights)
    cast_specs = [pl.BlockSpec((w.shape[0] // ffn_steps, w.shape[1]), lambda i: (i, 0)) for w in cast_weights]
    h1, um, *cast_done = _pallas_call(
        functools.partial(_ffn1_kernel, n_cast=len(cast_weights)), "ffn1", (ffn_steps,),
        inputs=[(h, wide(D)), whole(row(ffn1_norm)), whole(bf(ffn1_w1)), whole(bf(ffn1_w3)), whole(bf(ffn1_w2)),
                whole(row(mix_norm))] + list(zip(cast_weights, cast_specs)),
        outputs=[(jax.ShapeDtypeStruct((T, D), F32), wide(D)), (jax.ShapeDtypeStruct((T, D), BF16), wide(D))]
        + [(jax.ShapeDtypeStruct(w.shape, BF16), spec) for w, spec in zip(cast_weights, cast_specs)],
        scratch=[pltpu.VMEM((tf, d_ff), BF16)])
    ffn2_w1_bf, ffn2_w3_bf, ffn2_w2_bf, w_out_bf, w_branch_attn_bf = cast_done

    tp = PROJ_TOKEN_TILE
    proj_tiles_per_seq = S // tp
    ptok = lambda width: pl.BlockSpec((tp, width), lambda i: (i, 0))
    table = pl.BlockSpec((tp, LANES), lambda i: (i % proj_tiles_per_seq, 0))
    qkv_outputs = [(jax.ShapeDtypeStruct((B, r, S // r, GROUP_WIDTH), BF16),
                    pl.BlockSpec((1, r, tp // r, GROUP_WIDTH),
                                 lambda i: (i // proj_tiles_per_seq, 0, i % proj_tiles_per_seq, 0)))
                   for _, r in DILATED_GROUPS] * 3
    slab_slots = sum(_slots_per_half(r) for _, r in DILATED_GROUPS) * GROUP_WIDTH // LANES
    outs = _pallas_call(
        functools.partial(_proj_kernel, tiles_per_seq=proj_tiles_per_seq), "proj", (T // tp,),
        inputs=[(um, ptok(D)), whole(bf(w_in)), (cos_t, table), (sin_t, table),
                whole(bf(pool_w)), whole(row(pool_scale)), whole(bf(w_branch_pool))],
        outputs=qkv_outputs + [(jax.ShapeDtypeStruct((T, D), BF16), ptok(D))] * 2,
        scratch=[pltpu.VMEM((slab_slots, tp, LANES), F32), pltpu.VMEM((POOL_HALO + tp, POOL_WIDTH), F32)])
    qs, ks, vs = outs[0:3], outs[3:6], outs[6:9]
    gate_attn, gated_pool = outs[9], outs[10]

    tm = TOKEN_TILE
    tiles_per_seq = S // tm
    n_tiles = T // tm
    tok = lambda width: pl.BlockSpec((tm, width), lambda i: (i, 0))

    def residue_major(r):
        return pl.BlockSpec((1, r, tm // r, GROUP_WIDTH), lambda i: (i // tiles_per_seq, 0, i % tiles_per_seq, 0))

    attn_o, attn_lse = [], []
    rows = ATTN_ROWS
    blocks_per_step = rows // ATTN_BLOCK
    steps_per_batch = S // rows
    for g, (window, r) in enumerate(DILATED_GROUPS):
        L = S // r
        flat = lambda a: a.reshape(B, S, GROUP_WIDTH)
        cur = pl.BlockSpec((1, rows, GROUP_WIDTH), lambda b, s: (b, s, 0))
        prev = pl.BlockSpec((1, ATTN_BLOCK, GROUP_WIDTH),
                            lambda b, s: (b, jnp.maximum(s * blocks_per_step - 1, 0), 0))
        o_g, lse_g = _pallas_call(
            functools.partial(_attn_kernel, blocks_per_residue=L // ATTN_BLOCK, n_back=window // r),
            f"dilated_attn_g{g}", (B, steps_per_batch),
            inputs=[(flat(qs[g]), cur), (flat(ks[g]), cur), (flat(vs[g]), cur), (flat(ks[g]), prev),
                    (flat(vs[g]), prev)],
            outputs=[(jax.ShapeDtypeStruct((B, S, GROUP_WIDTH), BF16), cur),
                     (jax.ShapeDtypeStruct((B, S, GROUP_WIDTH), F32), cur)],
            scratch=[pltpu.VMEM((2, HEADS_PER_GROUP * ATTN_BLOCK, 2 * ATTN_BLOCK), F32),
                     pltpu.VMEM((HEADS_PER_GROUP * ATTN_BLOCK, 2 * ATTN_BLOCK), BF16)])
        attn_o.append(o_g.reshape(B, r, L, GROUP_WIDTH))
        attn_lse.append(lse_g.reshape(B, r, L, GROUP_WIDTH))

    o_specs = [residue_major(r) for _, r in DILATED_GROUPS]
    (out,) = _pallas_call(
        _merge_ffn2_kernel, "merge_ffn2", (n_tiles,),
        inputs=[(h1, tok(D))] + list(zip(attn_o, o_specs)) + list(zip(attn_lse, o_specs))
        + [(gate_attn, tok(D)), (gated_pool, tok(D)), whole(w_branch_attn_bf), whole(w_out_bf),
           whole(row(ffn2_norm)), whole(ffn2_w1_bf), whole(ffn2_w3_bf), whole(ffn2_w2_bf), whole(row(final_gain))],
        outputs=[(jax.ShapeDtypeStruct((T, D), F32), tok(D))],
        scratch=[pltpu.VMEM((tm, d_ff), BF16),
                 pltpu.VMEM((2 * (N_GROUPS - 1) * GROUP_WIDTH // LANES, tm, LANES), F32)])
    return out


def kernel(x, ffn1_norm, ffn1_w1, ffn1_w3, ffn1_w2, mix_norm, w_in, w_branch_attn, w_branch_pool, pool_w,
           pool_scale, w_out, ffn2_norm, ffn2_w1, ffn2_w3, ffn2_w2, final_norm):
    B, S, D = x.shape
    depth = ffn1_norm.shape[0]
    assert all(S % t == 0 for t in (TOKEN_TILE, FFN_TOKEN_TILE, PROJ_TOKEN_TILE, ATTN_ROWS))
    assert all(window // r == ATTN_BLOCK and (S // r) % ATTN_BLOCK == 0 for window, r in DILATED_GROUPS)
    assert all(t % (BF16_SUBLANES * r) == 0 for t in (TOKEN_TILE, PROJ_TOKEN_TILE) for _, r in DILATED_GROUPS)
    cos_t, sin_t = _rope_tables(S)
    h = x.reshape(B * S, D)
    for l in range(depth):
        assert l == depth - 1, "only the last layer's output norm is implemented"
        h = _layer(h, B, S, ffn1_norm[l], ffn1_w1[l], ffn1_w3[l], ffn1_w2[l], mix_norm[l], w_in[l],
                   w_branch_attn[l], w_branch_pool[l], pool_w[l], pool_scale[l], w_out[l], ffn2_norm[l],
                   ffn2_w1[l], ffn2_w3[l], ffn2_w2[l], final_norm, cos_t, sin_t)
    return h.reshape(B, S, D)
```

```python
import functools

import jax
import jax.numpy as jnp
import numpy as np
from jax import lax
from jax.experimental import pallas as pl
from jax.experimental.pallas import tpu as pltpu

F32 = jnp.float32
BF16 = jnp.bfloat16

HEAD_DIM = 64
HEADS_PER_GROUP = 4
GROUP_WIDTH = HEADS_PER_GROUP * HEAD_DIM
DILATED_GROUPS = ((128, 1), (512, 4), (2048, 16))
N_GROUPS = len(DILATED_GROUPS)
ATTN_BLOCK = 128
ROT_DIM = HEAD_DIM // 4
ROPE_THETA = 500000.0
POOL_WINDOWS = (2, 4, 8, 16)
POOL_GROUP_WIDTH = 128
POOL_WIDTH = len(POOL_WINDOWS) * POOL_GROUP_WIDTH
POOL_HALO = max(POOL_WINDOWS)
RMS_EPS = 1e-6
SCORE_SCALE_LOG2 = HEAD_DIM ** -0.5 * 1.4426950408889634

LANES = 128
BF16_SUBLANES = 16
V7X_SCOPED_VMEM_MAX_BYTES = 60000 * 1024
COMPILER_TEMP_BYTES = 4 * 1024 * 1024

TOKEN_TILE = 512
FFN_TOKEN_TILE = 1024
PROJ_TOKEN_TILE = 1024
FF_CHUNK = 256
ATTN_ROWS = 4096
DEINTERLEAVE_STEP = 4


def _rms(x, gain):
    return x * lax.rsqrt(jnp.mean(x * x, axis=-1, keepdims=True) + RMS_EPS) * gain


def _sigmoid(x):
    return 0.5 * jnp.tanh(0.5 * x) + 0.5


def _swiglu(u, w1_ref, w3_ref, w2_ref, hid_ref):
    d_ff = w1_ref.shape[1]
    for f in range(d_ff // FF_CHUNK):
        cols = slice(f * FF_CHUNK, (f + 1) * FF_CHUNK)
        ab = jnp.dot(u, jnp.concatenate([w1_ref[:, cols], w3_ref[:, cols]], axis=1), preferred_element_type=F32)
        a, b = ab[:, :FF_CHUNK], ab[:, FF_CHUNK:]
        half = 0.5 * a
        hid_ref[:, cols] = ((half * jnp.tanh(half) + half) * b).astype(BF16)
    return jnp.dot(hid_ref[...], w2_ref[...], preferred_element_type=F32)


def _slots_per_half(r):
    return 0 if r == 1 else 1 if r <= DEINTERLEAVE_STEP else 2


def _rope(x, cos, sin):
    j = lax.broadcasted_iota(jnp.int32, x.shape, 1) % HEAD_DIM
    half = ROT_DIM // 2
    partner = jnp.where(j < half, pltpu.roll(x, LANES - half, axis=1), pltpu.roll(x, half, axis=1))
    return jnp.where(j < ROT_DIM, x * cos + partner * sin, x)


def _pool_deltas(pext_ref, first_pos):
    tm = pext_ref.shape[0] - POOL_HALO
    pos = first_pos + lax.broadcasted_iota(jnp.int32, (tm, 1), 0)
    deltas = []
    for gi, win in enumerate(POOL_WINDOWS):
        lanes = slice(gi * POOL_GROUP_WIDTH, (gi + 1) * POOL_GROUP_WIDTH)
        own = pext_ref[POOL_HALO:POOL_HALO + tm, lanes]
        total = own
        for back in range(1, win):
            total = total + pext_ref[POOL_HALO - back:POOL_HALO - back + tm, lanes]
        count = jnp.minimum(pos + 1, win).astype(F32)
        deltas.append((total / count - own).astype(BF16))
    return deltas


def _pool_project(deltas, poolw_ref, pscale_ref, wb_ref):
    pooled = []
    for gi, d in enumerate(deltas):
        lanes = slice(gi * POOL_GROUP_WIDTH, (gi + 1) * POOL_GROUP_WIDTH)
        pooled.append(jnp.dot(d, poolw_ref[gi], preferred_element_type=F32) * pscale_ref[:, lanes])
    return jnp.dot(jnp.concatenate(pooled, axis=1).astype(BF16), wb_ref[...], preferred_element_type=F32)


def _ffn1_kernel(x_ref, g1_ref, w1_ref, w3_ref, w2_ref, gm_ref, *rest, n_cast):
    cast_in, (h1_ref, um_ref), cast_out = rest[:n_cast], rest[n_cast:n_cast + 2], rest[n_cast + 2:2 * n_cast + 2]
    (hid_ref,) = rest[2 * n_cast + 2:]
    for w_ref, w_bf_ref in zip(cast_in, cast_out):
        w_bf_ref[...] = w_ref[...].astype(BF16)
    x = x_ref[...]
    u = _rms(x, g1_ref[...]).astype(BF16)
    h1 = x + 0.5 * _swiglu(u, w1_ref, w3_ref, w2_ref, hid_ref)
    h1_ref[...] = h1
    um_ref[...] = _rms(h1, gm_ref[...]).astype(BF16)


def _proj_kernel(um_ref, win_ref, cos_ref, sin_ref, poolw_ref, pscale_ref, wb_ref,
                 q0_ref, q1_ref, q2_ref, k0_ref, k1_ref, k2_ref, v0_ref, v1_ref, v2_ref,
                 ga_ref, gpool_ref, slab_ref, pext_ref, *, tiles_per_seq):
    tm = um_ref.shape[0]
    d_model = um_ref.shape[1]
    t = pl.program_id(0) % tiles_per_seq

    @pl.when(t == 0)
    def _():
        pext_ref[0:POOL_HALO, :] = jnp.zeros((POOL_HALO, POOL_WIDTH), F32)

    um = um_ref[...]
    cos = cos_ref[...]
    sin = sin_ref[...]

    def project(col, width):
        return jnp.dot(um, win_ref[:, col:col + width], preferred_element_type=F32)

    def rope(p):
        return jnp.concatenate([_rope(p[:, :LANES], cos, sin), _rope(p[:, LANES:], cos, sin)], axis=1)

    qkv_width = 3 * N_GROUPS * GROUP_WIDTH
    gate_col = qkv_width + POOL_WIDTH
    halves = GROUP_WIDTH // LANES
    first_slot = [halves * sum(_slots_per_half(r) for _, r in DILATED_GROUPS[:g]) for g in range(N_GROUPS)]

    def store_residue_major(p, ref, r, scratch_slot):
        if r == 1:
            ref[0, 0] = p.astype(BF16)
            return
        for half in range(halves):
            lanes = slice(half * LANES, (half + 1) * LANES)
            first = scratch_slot + half * _slots_per_half(r)
            slab_ref[first] = p[:, lanes]
            if _slots_per_half(r) == 1:
                for c in range(r):
                    ref[0, c, :, lanes] = slab_ref[first, pl.ds(c, tm // r, stride=r), :].astype(BF16)
                continue
            step, rest = DEINTERLEAVE_STEP, r // DEINTERLEAVE_STEP
            for c0 in range(step):
                slab_ref[first + 1, c0 * (tm // step):(c0 + 1) * (tm // step), :] = (
                    slab_ref[first, pl.ds(c0, tm // step, stride=step), :])
            for c0 in range(step):
                for c1 in range(rest):
                    ref[0, c1 * step + c0, :, lanes] = slab_ref[
                        first + 1, pl.ds(c0 * (tm // step) + c1, tm // r, stride=rest), :].astype(BF16)

    def qkv(kind, refs, group_order):
        both = project(("q", "k", "v").index(kind) * N_GROUPS * GROUP_WIDTH, N_GROUPS * GROUP_WIDTH)
        for g in group_order:
            p = both[:, g * GROUP_WIDTH:(g + 1) * GROUP_WIDTH]
            if kind == "q":
                p = rope(p) * SCORE_SCALE_LOG2
            elif kind == "k":
                p = rope(p)
            store_residue_major(p, refs[g], DILATED_GROUPS[g][1], first_slot[g])

    pext_ref[POOL_HALO:, :] = project(qkv_width, POOL_WIDTH)
    deltas = _pool_deltas(pext_ref, t * tm)
    pext_ref[0:POOL_HALO, :] = pext_ref[tm:tm + POOL_HALO, :]
    pair = 2 * GROUP_WIDTH
    for s in range(d_model // pair):
        lanes = slice(s * pair, (s + 1) * pair)
        ga_ref[:, lanes] = _sigmoid(project(gate_col + s * pair, pair)).astype(BF16)
    qkv("q", (q0_ref, q1_ref, q2_ref), range(N_GROUPS))
    y_pool = _pool_project(deltas, poolw_ref, pscale_ref, wb_ref)
    for s in range(d_model // pair):
        lanes = slice(s * pair, (s + 1) * pair)
        gate_pool = _sigmoid(project(gate_col + d_model + s * pair, pair))
        gpool_ref[:, lanes] = (gate_pool * y_pool[:, lanes]).astype(BF16)
    qkv("k", (k0_ref, k1_ref, k2_ref), reversed(range(N_GROUPS)))
    qkv("v", (v0_ref, v1_ref, v2_ref), reversed(range(N_GROUPS)))


def _attn_kernel(q_ref, k_ref, v_ref, kh_ref, vh_ref, o_ref, lse_ref, s_ref, p_ref, *,
                 blocks_per_residue, n_back):
    rows = q_ref.shape[1]
    nblk = rows // ATTN_BLOCK
    step = pl.program_id(1)
    qi = lax.broadcasted_iota(jnp.int32, (ATTN_BLOCK, 2 * ATTN_BLOCK), 0)
    kj = lax.broadcasted_iota(jnp.int32, (ATTN_BLOCK, 2 * ATTN_BLOCK), 1)
    dist = qi + ATTN_BLOCK - kj
    band = (dist >= 0) & (dist <= n_back)
    in_current = kj >= ATTN_BLOCK
    head_of_lane = lax.broadcasted_iota(jnp.int32, (1, GROUP_WIDTH), 1) // HEAD_DIM
    heads = range(HEADS_PER_GROUP)
    heads_per_vreg = LANES // HEAD_DIM

    def head_rows(h):
        return slice(h * ATTN_BLOCK, (h + 1) * ATTN_BLOCK)

    def keys_values(j, ref, halo_ref):
        if j == 0:
            return jnp.concatenate([halo_ref[0], ref[0, 0:ATTN_BLOCK, :]], axis=0)
        return ref[0, (j - 1) * ATTN_BLOCK:(j + 1) * ATTN_BLOCK, :]

    def scores(j):
        qb = q_ref[0, j * ATTN_BLOCK:(j + 1) * ATTN_BLOCK, :]
        qs = jnp.concatenate([jnp.where(head_of_lane == h, qb, jnp.zeros_like(qb)) for h in heads], axis=0)
        return lax.dot_general(qs, keys_values(j, k_ref, kh_ref), (((1,), (1,)), ((), ())),
                               preferred_element_type=F32)

    def lanes_by_head(per_head):
        first_in_vreg = (head_of_lane[:, :LANES] % heads_per_vreg) == 0
        vregs = [jnp.where(first_in_vreg, per_head[v * heads_per_vreg], per_head[v * heads_per_vreg + 1])
                 for v in range(GROUP_WIDTH // LANES)]
        return jnp.concatenate(vregs, axis=1)

    s_ref[0] = scores(0)
    for j in range(nblk):
        if j + 1 < nblk:
            s_ref[(j + 1) % 2] = scores(j + 1)
        has_prev = ((step * nblk + j) % blocks_per_residue) != 0
        allowed = band & (in_current | has_prev)
        inv_l, lse = [], []
        for h in heads:
            s = jnp.where(allowed, s_ref[j % 2, head_rows(h), :], -jnp.inf)
            m = jnp.max(s, axis=-1, keepdims=True)
            p = jnp.exp2(s - m)
            l = jnp.sum(p, axis=-1, keepdims=True)
            p_ref[head_rows(h), :] = p.astype(BF16)
            inv_l.append(1.0 / l)
            lse.append(m + jnp.log2(l))
        pv = jnp.dot(p_ref[...], keys_values(j, v_ref, vh_ref), preferred_element_type=F32)
        o_heads = [pv[head_rows(h), (h // heads_per_vreg) * LANES:(h // heads_per_vreg + 1) * LANES] * inv_l[h]
                   for h in heads]
        cur = slice(j * ATTN_BLOCK, (j + 1) * ATTN_BLOCK)
        o_ref[0, cur, :] = lanes_by_head(o_heads).astype(BF16)
        lse_ref[0, cur, :] = lanes_by_head([jnp.broadcast_to(x, (ATTN_BLOCK, LANES)) for x in lse])


def _merge_ffn2_kernel(h1_ref, o0_ref, o1_ref, o2_ref, l0_ref, l1_ref, l2_ref, ga_ref, gpool_ref,
                       wa_ref, wo_ref, g2_ref, w1_ref, w3_ref, w2_ref, gf_ref,
                       out_ref, hid_ref, nat_ref):
    tm = h1_ref.shape[0]

    def natural_order(ref, r, slot):
        if r == 1:
            return ref[0, 0].astype(F32)
        halves = []
        for half in range(GROUP_WIDTH // LANES):
            lanes = slice(half * LANES, (half + 1) * LANES)
            for c in range(r):
                nat_ref[slot + half, pl.ds(c, tm // r, stride=r), :] = ref[0, c, :, lanes].astype(F32)
            halves.append(nat_ref[slot + half])
        return jnp.concatenate(halves, axis=1)

    outs, lses = [], []
    slot = 0
    halves_per_slab = GROUP_WIDTH // LANES
    for (_, r), o_ref, l_ref in zip(DILATED_GROUPS, (o0_ref, o1_ref, o2_ref), (l0_ref, l1_ref, l2_ref)):
        outs.append(natural_order(o_ref, r, slot))
        lses.append(natural_order(l_ref, r, slot + halves_per_slab))
        if r != 1:
            slot += 2 * halves_per_slab
    top = functools.reduce(jnp.maximum, lses)
    weights = [jnp.exp2(l - top) for l in lses]
    o = sum(w * og for w, og in zip(weights, outs)) / sum(weights)
    y_attn = jnp.dot(o.astype(BF16), wa_ref[...], preferred_element_type=F32)

    merged = ga_ref[...].astype(F32) * y_attn + gpool_ref[...].astype(F32)
    h2 = h1_ref[...] + jnp.dot(merged.astype(BF16), wo_ref[...], preferred_element_type=F32)
    u2 = _rms(h2, g2_ref[...]).astype(BF16)
    h3 = h2 + 0.5 * _swiglu(u2, w1_ref, w3_ref, w2_ref, hid_ref)
    out_ref[...] = _rms(h3, gf_ref[...])


def _resident(shape):
    return pl.BlockSpec(shape, lambda *_: (0,) * len(shape), pipeline_mode=pl.Buffered(1))


def _pallas_call(body, name, grid, inputs, outputs, scratch):
    def window_bytes(aval, spec):
        buffers = 1 if isinstance(spec.pipeline_mode, pl.Buffered) and spec.pipeline_mode.buffer_count == 1 else 2
        elems = 1
        for d in spec.block_shape:
            elems *= d
        return buffers * elems * jnp.dtype(aval.dtype).itemsize

    request = COMPILER_TEMP_BYTES + sum(window_bytes(a, spec) for a, spec in (*inputs, *outputs))
    for buf in scratch:
        elems = 1
        for d in buf.shape:
            elems *= d
        request += elems * jnp.dtype(buf.dtype).itemsize
    assert request <= V7X_SCOPED_VMEM_MAX_BYTES, (name, request)
    return pl.pallas_call(
        body, name=name, grid=grid,
        in_specs=[spec for _, spec in inputs], out_specs=[spec for _, spec in outputs],
        out_shape=[shape for shape, _ in outputs], scratch_shapes=scratch,
        compiler_params=pltpu.CompilerParams(dimension_semantics=("arbitrary",) * len(grid),
                                             vmem_limit_bytes=V7X_SCOPED_VMEM_MAX_BYTES),
    )(*[a for a, _ in inputs])


def _rope_tables(seq):
    half = ROT_DIM // 2
    j = np.arange(LANES) % HEAD_DIM
    inv = np.where(j < ROT_DIM, ROPE_THETA ** (-(2.0 * (j % half)) / ROT_DIM), 0.0)
    sign = np.where(j < half, -1.0, 1.0)
    ang = np.arange(seq, dtype=np.float64)[:, None] * inv[None, :]
    return jnp.asarray(np.cos(ang), F32), jnp.asarray(np.sin(ang) * sign[None, :], F32)


def _layer(h, B, S, ffn1_norm, ffn1_w1, ffn1_w3, ffn1_w2, mix_norm, w_in, w_branch_attn, w_branch_pool,
           pool_w, pool_scale, w_out, ffn2_norm, ffn2_w1, ffn2_w3, ffn2_w2, final_gain, cos_t, sin_t):
    T, D = h.shape
    d_ff = ffn1_w1.shape[1]
    in_width = w_in.shape[1]
    assert in_width == 3 * N_GROUPS * GROUP_WIDTH + POOL_WIDTH + 2 * D
    bf = lambda w: w.astype(BF16)
    row = lambda g: g.reshape(1, -1).astype(F32)
    whole = lambda a: (a, _resident(a.shape))

    tf = FFN_TOKEN_TILE
    ffn_steps = T // tf
    wide = lambda width: pl.BlockSpec((tf, width), lambda i: (i, 0))
    cast_weights = (ffn2_w1, ffn2_w3, ffn2_w2, w_out, w_branch_attn, w_in, w_branch_pool,
                    pool_w.reshape(-1, POOL_GROUP_WIDTH))
    assert all(w.shape[0] % (BF16_SUBLANES * ffn_steps) == 0 for w in cast_weights)
    cast_specs = [pl.BlockSpec((w.shape[0] // ffn_steps, w.shape[1]), lambda i: (i, 0)) for w in cast_weights]
    h1, um, *cast_done = _pallas_call(
        functools.partial(_ffn1_kernel, n_cast=len(cast_weights)), "ffn1", (ffn_steps,),
        inputs=[(h, wide(D)), whole(row(ffn1_norm)), whole(bf(ffn1_w1)), whole(bf(ffn1_w3)), whole(bf(ffn1_w2)),
                whole(row(mix_norm))] + list(zip(cast_weights, cast_specs)),
        outputs=[(jax.ShapeDtypeStruct((T, D), F32), wide(D)), (jax.ShapeDtypeStruct((T, D), BF16), wide(D))]
        + [(jax.ShapeDtypeStruct(w.shape, BF16), spec) for w, spec in zip(cast_weights, cast_specs)],
        scratch=[pltpu.VMEM((tf, d_ff), BF16)])
    ffn2_w1_bf, ffn2_w3_bf, ffn2_w2_bf, w_out_bf, w_branch_attn_bf, w_in_bf, w_branch_pool_bf, pool_w_bf = cast_done
    pool_w_bf = pool_w_bf.reshape(pool_w.shape)

    tp = PROJ_TOKEN_TILE
    proj_tiles_per_seq = S // tp
    ptok = lambda width: pl.BlockSpec((tp, width), lambda i: (i, 0))
    table = pl.BlockSpec((tp, LANES), lambda i: (i % proj_tiles_per_seq, 0))
    qkv_outputs = [(jax.ShapeDtypeStruct((B, r, S // r, GROUP_WIDTH), BF16),
                    pl.BlockSpec((1, r, tp // r, GROUP_WIDTH),
                                 lambda i: (i // proj_tiles_per_seq, 0, i % proj_tiles_per_seq, 0)))
                   for _, r in DILATED_GROUPS] * 3
    slab_slots = sum(_slots_per_half(r) for _, r in DILATED_GROUPS) * GROUP_WIDTH // LANES
    outs = _pallas_call(
        functools.partial(_proj_kernel, tiles_per_seq=proj_tiles_per_seq), "proj", (T // tp,),
        inputs=[(um, ptok(D)), whole(w_in_bf), (cos_t, table), (sin_t, table),
                whole(pool_w_bf), whole(row(pool_scale)), whole(w_branch_pool_bf)],
        outputs=qkv_outputs + [(jax.ShapeDtypeStruct((T, D), BF16), ptok(D))] * 2,
        scratch=[pltpu.VMEM((slab_slots, tp, LANES), F32), pltpu.VMEM((POOL_HALO + tp, POOL_WIDTH), F32)])
    qs, ks, vs = outs[0:3], outs[3:6], outs[6:9]
    gate_attn, gated_pool = outs[9], outs[10]

    tm = TOKEN_TILE
    tiles_per_seq = S // tm
    n_tiles = T // tm
    tok = lambda width: pl.BlockSpec((tm, width), lambda i: (i, 0))

    def residue_major(r):
        return pl.BlockSpec((1, r, tm // r, GROUP_WIDTH), lambda i: (i // tiles_per_seq, 0, i % tiles_per_seq, 0))

    attn_o, attn_lse = [], []
    rows = ATTN_ROWS
    blocks_per_step = rows // ATTN_BLOCK
    steps_per_batch = S // rows
    for g, (window, r) in enumerate(DILATED_GROUPS):
        L = S // r
        flat = lambda a: a.reshape(B, S, GROUP_WIDTH)
        cur = pl.BlockSpec((1, rows, GROUP_WIDTH), lambda b, s: (b, s, 0))
        prev = pl.BlockSpec((1, ATTN_BLOCK, GROUP_WIDTH),
                            lambda b, s: (b, jnp.maximum(s * blocks_per_step - 1, 0), 0))
        o_g, lse_g = _pallas_call(
            functools.partial(_attn_kernel, blocks_per_residue=L // ATTN_BLOCK, n_back=window // r),
            f"dilated_attn_g{g}", (B, steps_per_batch),
            inputs=[(flat(qs[g]), cur), (flat(ks[g]), cur), (flat(vs[g]), cur), (flat(ks[g]), prev),
                    (flat(vs[g]), prev)],
            outputs=[(jax.ShapeDtypeStruct((B, S, GROUP_WIDTH), BF16), cur),
                     (jax.ShapeDtypeStruct((B, S, GROUP_WIDTH), F32), cur)],
            scratch=[pltpu.VMEM((2, HEADS_PER_GROUP * ATTN_BLOCK, 2 * ATTN_BLOCK), F32),
                     pltpu.VMEM((HEADS_PER_GROUP * ATTN_BLOCK, 2 * ATTN_BLOCK), BF16)])
        attn_o.append(o_g.reshape(B, r, L, GROUP_WIDTH))
        attn_lse.append(lse_g.reshape(B, r, L, GROUP_WIDTH))

    o_specs = [residue_major(r) for _, r in DILATED_GROUPS]
    (out,) = _pallas_call(
        _merge_ffn2_kernel, "merge_ffn2", (n_tiles,),
        inputs=[(h1, tok(D))] + list(zip(attn_o, o_specs)) + list(zip(attn_lse, o_specs))
        + [(gate_attn, tok(D)), (gated_pool, tok(D)), whole(w_branch_attn_bf), whole(w_out_bf),
           whole(row(ffn2_norm)), whole(ffn2_w1_bf), whole(ffn2_w3_bf), whole(ffn2_w2_bf), whole(row(final_gain))],
        outputs=[(jax.ShapeDtypeStruct((T, D), F32), tok(D))],
        scratch=[pltpu.VMEM((tm, d_ff), BF16),
                 pltpu.VMEM((2 * (N_GROUPS - 1) * GROUP_WIDTH // LANES, tm, LANES), F32)])
    return out


def kernel(x, ffn1_norm, ffn1_w1, ffn1_w3, ffn1_w2, mix_norm, w_in, w_branch_attn, w_branch_pool, pool_w,
           pool_scale, w_out, ffn2_norm, ffn2_w1, ffn2_w3, ffn2_w2, final_norm):
    B, S, D = x.shape
    depth = ffn1_norm.shape[0]
    assert all(S % t == 0 for t in (TOKEN_TILE, FFN_TOKEN_TILE, PROJ_TOKEN_TILE, ATTN_ROWS))
    assert all(window // r == ATTN_BLOCK and (S // r) % ATTN_BLOCK == 0 for window, r in DILATED_GROUPS)
    assert all(t % (BF16_SUBLANES * r) == 0 for t in (TOKEN_TILE, PROJ_TOKEN_TILE) for _, r in DILATED_GROUPS)
    cos_t, sin_t = _rope_tables(S)
    h = x.reshape(B * S, D)
    for l in range(depth):
        assert l == depth - 1, "only the last layer's output norm is implemented"
        h = _layer(h, B, S, ffn1_norm[l], ffn1_w1[l], ffn1_w3[l], ffn1_w2[l], mix_norm[l], w_in[l],
                   w_branch_attn[l], w_branch_pool[l], pool_w[l], pool_scale[l], w_out[l], ffn2_norm[l],
                   ffn2_w1[l], ffn2_w3[l], ffn2_w2[l], final_norm, cos_t, sin_t)
    return h.reshape(B, S, D)
```

```python
import functools

import jax
import jax.numpy as jnp
import numpy as np
from jax import lax
from jax.experimental import pallas as pl
from jax.experimental.pallas import tpu as pltpu

F32 = jnp.float32
BF16 = jnp.bfloat16

HEAD_DIM = 64
HEADS_PER_GROUP = 4
GROUP_WIDTH = HEADS_PER_GROUP * HEAD_DIM
DILATED_GROUPS = ((128, 1), (512, 4), (2048, 16))
N_GROUPS = len(DILATED_GROUPS)
ATTN_BLOCK = 128
ROT_DIM = HEAD_DIM // 4
ROPE_THETA = 500000.0
POOL_WINDOWS = (2, 4, 8, 16)
POOL_GROUP_WIDTH = 128
POOL_WIDTH = len(POOL_WINDOWS) * POOL_GROUP_WIDTH
POOL_HALO = max(POOL_WINDOWS)
RMS_EPS = 1e-6
SCORE_SCALE_LOG2 = HEAD_DIM ** -0.5 * 1.4426950408889634

LANES = 128
BF16_SUBLANES = 16
V7X_SCOPED_VMEM_MAX_BYTES = 60000 * 1024
COMPILER_TEMP_BYTES = 4 * 1024 * 1024

TOKEN_TILE = 512
FFN_TOKEN_TILE = 512
PROJ_TOKEN_TILE = 1024
FF_CHUNK = 256
ATTN_ROWS = 4096
DEINTERLEAVE_STEP = 4


def _rms(x, gain):
    return x * lax.rsqrt(jnp.mean(x * x, axis=-1, keepdims=True) + RMS_EPS) * gain


def _sigmoid(x):
    return 0.5 * jnp.tanh(0.5 * x) + 0.5


def _swiglu(u, w1_ref, w3_ref, w2_ref, hid_ref):
    d_ff = w1_ref.shape[1]
    for f in range(d_ff // FF_CHUNK):
        cols = slice(f * FF_CHUNK, (f + 1) * FF_CHUNK)
        w13 = jnp.concatenate([w1_ref[:, cols].astype(BF16), w3_ref[:, cols].astype(BF16)], axis=1)
        ab = jnp.dot(u, w13, preferred_element_type=F32)
        a, b = ab[:, :FF_CHUNK], ab[:, FF_CHUNK:]
        half = 0.5 * a
        hid_ref[:, cols] = ((half * jnp.tanh(half) + half) * b).astype(BF16)
    if w2_ref.dtype == BF16:
        return jnp.dot(hid_ref[...], w2_ref[...], preferred_element_type=F32)
    down = None
    for f in range(d_ff // FF_CHUNK):
        rows = slice(f * FF_CHUNK, (f + 1) * FF_CHUNK)
        part = jnp.dot(hid_ref[:, rows], w2_ref[rows, :].astype(BF16), preferred_element_type=F32)
        down = part if down is None else down + part
    return down


def _slots_per_half(r):
    return 0 if r == 1 else 1 if r <= DEINTERLEAVE_STEP else 2


def _rope(x, cos, sin):
    j = lax.broadcasted_iota(jnp.int32, x.shape, 1) % HEAD_DIM
    half = ROT_DIM // 2
    partner = jnp.where(j < half, pltpu.roll(x, LANES - half, axis=1), pltpu.roll(x, half, axis=1))
    return jnp.where(j < ROT_DIM, x * cos + partner * sin, x)


def _pool_deltas(pext_ref, first_pos):
    tm = pext_ref.shape[0] - POOL_HALO
    pos = first_pos + lax.broadcasted_iota(jnp.int32, (tm, 1), 0)
    deltas = []
    for gi, win in enumerate(POOL_WINDOWS):
        lanes = slice(gi * POOL_GROUP_WIDTH, (gi + 1) * POOL_GROUP_WIDTH)
        own = pext_ref[POOL_HALO:POOL_HALO + tm, lanes]
        total = own
        for back in range(1, win):
            total = total + pext_ref[POOL_HALO - back:POOL_HALO - back + tm, lanes]
        count = jnp.minimum(pos + 1, win).astype(F32)
        deltas.append((total / count - own).astype(BF16))
    return deltas


def _pool_project(deltas, poolw_ref, pscale_ref, wb_ref):
    pooled = []
    for gi, d in enumerate(deltas):
        lanes = slice(gi * POOL_GROUP_WIDTH, (gi + 1) * POOL_GROUP_WIDTH)
        pooled.append(jnp.dot(d, poolw_ref[gi], preferred_element_type=F32) * pscale_ref[:, lanes])
    return jnp.dot(jnp.concatenate(pooled, axis=1).astype(BF16), wb_ref[...], preferred_element_type=F32)


def _ffn1_kernel(x_ref, g1_ref, w1_ref, w3_ref, w2_ref, gm_ref, *rest, n_cast):
    cast_in, (h1_ref, um_ref), cast_out = rest[:n_cast], rest[n_cast:n_cast + 2], rest[n_cast + 2:2 * n_cast + 2]
    (hid_ref,) = rest[2 * n_cast + 2:]
    for w_ref, w_bf_ref in zip(cast_in, cast_out):
        w_bf_ref[...] = w_ref[...].astype(BF16)
    x = x_ref[...]
    u = _rms(x, g1_ref[...]).astype(BF16)
    h1 = x + 0.5 * _swiglu(u, w1_ref, w3_ref, w2_ref, hid_ref)
    h1_ref[...] = h1
    um_ref[...] = _rms(h1, gm_ref[...]).astype(BF16)


def _proj_kernel(um_ref, win_ref, cos_ref, sin_ref, poolw_ref, pscale_ref, wb_ref,
                 q0_ref, q1_ref, q2_ref, k0_ref, k1_ref, k2_ref, v0_ref, v1_ref, v2_ref,
                 ga_ref, gpool_ref, slab_ref, pext_ref, *, tiles_per_seq):
    tm = um_ref.shape[0]
    d_model = um_ref.shape[1]
    t = pl.program_id(0) % tiles_per_seq

    @pl.when(t == 0)
    def _():
        pext_ref[0:POOL_HALO, :] = jnp.zeros((POOL_HALO, POOL_WIDTH), F32)

    um = um_ref[...]
    cos = cos_ref[...]
    sin = sin_ref[...]

    def project(col, width):
        return jnp.dot(um, win_ref[:, col:col + width], preferred_element_type=F32)

    def rope(p):
        return jnp.concatenate([_rope(p[:, :LANES], cos, sin), _rope(p[:, LANES:], cos, sin)], axis=1)

    qkv_width = 3 * N_GROUPS * GROUP_WIDTH
    gate_col = qkv_width + POOL_WIDTH
    halves = GROUP_WIDTH // LANES
    first_slot = [halves * sum(_slots_per_half(r) for _, r in DILATED_GROUPS[:g]) for g in range(N_GROUPS)]

    def store_residue_major(p, ref, r, scratch_slot):
        if r == 1:
            ref[0, 0] = p.astype(BF16)
            return
        for half in range(halves):
            lanes = slice(half * LANES, (half + 1) * LANES)
            first = scratch_slot + half * _slots_per_half(r)
            slab_ref[first] = p[:, lanes]
            if _slots_per_half(r) == 1:
                for c in range(r):
                    ref[0, c, :, lanes] = slab_ref[first, pl.ds(c, tm // r, stride=r), :].astype(BF16)
                continue
            step, rest = DEINTERLEAVE_STEP, r // DEINTERLEAVE_STEP
            for c0 in range(step):
                slab_ref[first + 1, c0 * (tm // step):(c0 + 1) * (tm // step), :] = (
                    slab_ref[first, pl.ds(c0, tm // step, stride=step), :])
            for c0 in range(step):
                for c1 in range(rest):
                    ref[0, c1 * step + c0, :, lanes] = slab_ref[
                        first + 1, pl.ds(c0 * (tm // step) + c1, tm // r, stride=rest), :].astype(BF16)

    def qkv(kind, refs, group_order):
        both = project(("q", "k", "v").index(kind) * N_GROUPS * GROUP_WIDTH, N_GROUPS * GROUP_WIDTH)
        for g in group_order:
            p = both[:, g * GROUP_WIDTH:(g + 1) * GROUP_WIDTH]
            if kind == "q":
                p = rope(p) * SCORE_SCALE_LOG2
            elif kind == "k":
                p = rope(p)
            store_residue_major(p, refs[g], DILATED_GROUPS[g][1], first_slot[g])

    pext_ref[POOL_HALO:, :] = project(qkv_width, POOL_WIDTH)
    deltas = _pool_deltas(pext_ref, t * tm)
    pext_ref[0:POOL_HALO, :] = pext_ref[tm:tm + POOL_HALO, :]
    pair = 2 * GROUP_WIDTH
    for s in range(d_model // pair):
        lanes = slice(s * pair, (s + 1) * pair)
        ga_ref[:, lanes] = _sigmoid(project(gate_col + s * pair, pair)).astype(BF16)
    qkv("q", (q0_ref, q1_ref, q2_ref), range(N_GROUPS))
    y_pool = _pool_project(deltas, poolw_ref, pscale_ref, wb_ref)
    for s in range(d_model // pair):
        lanes = slice(s * pair, (s + 1) * pair)
        gate_pool = _sigmoid(project(gate_col + d_model + s * pair, pair))
        gpool_ref[:, lanes] = (gate_pool * y_pool[:, lanes]).astype(BF16)
    qkv("k", (k0_ref, k1_ref, k2_ref), reversed(range(N_GROUPS)))
    qkv("v", (v0_ref, v1_ref, v2_ref), reversed(range(N_GROUPS)))


def _attn_kernel(q_ref, k_ref, v_ref, kh_ref, vh_ref, o_ref, lse_ref, s_ref, p_ref, *,
                 blocks_per_residue, n_back):
    rows = q_ref.shape[1]
    nblk = rows // ATTN_BLOCK
    step = pl.program_id(1)
    qi = lax.broadcasted_iota(jnp.int32, (ATTN_BLOCK, 2 * ATTN_BLOCK), 0)
    kj = lax.broadcasted_iota(jnp.int32, (ATTN_BLOCK, 2 * ATTN_BLOCK), 1)
    dist = qi + ATTN_BLOCK - kj
    band = (dist >= 0) & (dist <= n_back)
    in_current = kj >= ATTN_BLOCK
    head_of_lane = lax.broadcasted_iota(jnp.int32, (1, GROUP_WIDTH), 1) // HEAD_DIM
    heads = range(HEADS_PER_GROUP)
    heads_per_vreg = LANES // HEAD_DIM

    def head_rows(h):
        return slice(h * ATTN_BLOCK, (h + 1) * ATTN_BLOCK)

    def keys_values(j, ref, halo_ref):
        if j == 0:
            return jnp.concatenate([halo_ref[0], ref[0, 0:ATTN_BLOCK, :]], axis=0)
        return ref[0, (j - 1) * ATTN_BLOCK:(j + 1) * ATTN_BLOCK, :]

    def scores(j):
        qb = q_ref[0, j * ATTN_BLOCK:(j + 1) * ATTN_BLOCK, :]
        qs = jnp.concatenate([jnp.where(head_of_lane == h, qb, jnp.zeros_like(qb)) for h in heads], axis=0)
        return lax.dot_general(qs, keys_values(j, k_ref, kh_ref), (((1,), (1,)), ((), ())),
                               preferred_element_type=F32)

    def lanes_by_head(per_head):
        first_in_vreg = (head_of_lane[:, :LANES] % heads_per_vreg) == 0
        vregs = [jnp.where(first_in_vreg, per_head[v * heads_per_vreg], per_head[v * heads_per_vreg + 1])
                 for v in range(GROUP_WIDTH // LANES)]
        return jnp.concatenate(vregs, axis=1)

    s_ref[0] = scores(0)
    for j in range(nblk):
        if j + 1 < nblk:
            s_ref[(j + 1) % 2] = scores(j + 1)
        has_prev = ((step * nblk + j) % blocks_per_residue) != 0
        allowed = band & (in_current | has_prev)
        inv_l, lse = [], []
        for h in heads:
            s = jnp.where(allowed, s_ref[j % 2, head_rows(h), :], -jnp.inf)
            m = jnp.max(s, axis=-1, keepdims=True)
            p = jnp.exp2(s - m)
            l = jnp.sum(p, axis=-1, keepdims=True)
            p_ref[head_rows(h), :] = p.astype(BF16)
            inv_l.append(1.0 / l)
            lse.append(m + jnp.log2(l))
        pv = jnp.dot(p_ref[...], keys_values(j, v_ref, vh_ref), preferred_element_type=F32)
        o_heads = [pv[head_rows(h), (h // heads_per_vreg) * LANES:(h // heads_per_vreg + 1) * LANES] * inv_l[h]
                   for h in heads]
        cur = slice(j * ATTN_BLOCK, (j + 1) * ATTN_BLOCK)
        o_ref[0, cur, :] = lanes_by_head(o_heads).astype(BF16)
        lse_ref[0, cur, :] = lanes_by_head([jnp.broadcast_to(x, (ATTN_BLOCK, LANES)) for x in lse])


def _merge_ffn2_kernel(h1_ref, o0_ref, o1_ref, o2_ref, l0_ref, l1_ref, l2_ref, ga_ref, gpool_ref,
                       wa_ref, wo_ref, g2_ref, w1_ref, w3_ref, w2_ref, gf_ref,
                       out_ref, hid_ref, nat_ref):
    tm = h1_ref.shape[0]

    def natural_order(ref, r, slot):
        if r == 1:
            return ref[0, 0].astype(F32)
        halves = []
        for half in range(GROUP_WIDTH // LANES):
            lanes = slice(half * LANES, (half + 1) * LANES)
            for c in range(r):
                nat_ref[slot + half, pl.ds(c, tm // r, stride=r), :] = ref[0, c, :, lanes].astype(F32)
            halves.append(nat_ref[slot + half])
        return jnp.concatenate(halves, axis=1)

    outs, lses = [], []
    slot = 0
    halves_per_slab = GROUP_WIDTH // LANES
    for (_, r), o_ref, l_ref in zip(DILATED_GROUPS, (o0_ref, o1_ref, o2_ref), (l0_ref, l1_ref, l2_ref)):
        outs.append(natural_order(o_ref, r, slot))
        lses.append(natural_order(l_ref, r, slot + halves_per_slab))
        if r != 1:
            slot += 2 * halves_per_slab
    top = functools.reduce(jnp.maximum, lses)
    weights = [jnp.exp2(l - top) for l in lses]
    o = sum(w * og for w, og in zip(weights, outs)) / sum(weights)
    y_attn = jnp.dot(o.astype(BF16), wa_ref[...], preferred_element_type=F32)

    merged = ga_ref[...].astype(F32) * y_attn + gpool_ref[...].astype(F32)
    h2 = h1_ref[...] + jnp.dot(merged.astype(BF16), wo_ref[...], preferred_element_type=F32)
    u2 = _rms(h2, g2_ref[...]).astype(BF16)
    h3 = h2 + 0.5 * _swiglu(u2, w1_ref, w3_ref, w2_ref, hid_ref)
    out_ref[...] = _rms(h3, gf_ref[...])


def _resident(shape):
    return pl.BlockSpec(shape, lambda *_: (0,) * len(shape), pipeline_mode=pl.Buffered(1))


def _pallas_call(body, name, grid, inputs, outputs, scratch):
    def window_bytes(aval, spec):
        buffers = 1 if isinstance(spec.pipeline_mode, pl.Buffered) and spec.pipeline_mode.buffer_count == 1 else 2
        elems = 1
        for d in spec.block_shape:
            elems *= d
        return buffers * elems * jnp.dtype(aval.dtype).itemsize

    request = COMPILER_TEMP_BYTES + sum(window_bytes(a, spec) for a, spec in (*inputs, *outputs))
    for buf in scratch:
        elems = 1
        for d in buf.shape:
            elems *= d
        request += elems * jnp.dtype(buf.dtype).itemsize
    assert request <= V7X_SCOPED_VMEM_MAX_BYTES, (name, request)
    return pl.pallas_call(
        body, name=name, grid=grid,
        in_specs=[spec for _, spec in inputs], out_specs=[spec for _, spec in outputs],
        out_shape=[shape for shape, _ in outputs], scratch_shapes=scratch,
        compiler_params=pltpu.CompilerParams(dimension_semantics=("arbitrary",) * len(grid),
                                             vmem_limit_bytes=V7X_SCOPED_VMEM_MAX_BYTES),
    )(*[a for a, _ in inputs])


def _rope_tables(seq):
    half = ROT_DIM // 2
    j = np.arange(LANES) % HEAD_DIM
    inv = np.where(j < ROT_DIM, ROPE_THETA ** (-(2.0 * (j % half)) / ROT_DIM), 0.0)
    sign = np.where(j < half, -1.0, 1.0)
    ang = np.arange(seq, dtype=np.float64)[:, None] * inv[None, :]
    return jnp.asarray(np.cos(ang), F32), jnp.asarray(np.sin(ang) * sign[None, :], F32)


def _layer(h, B, S, ffn1_norm, ffn1_w1, ffn1_w3, ffn1_w2, mix_norm, w_in, w_branch_attn, w_branch_pool,
           pool_w, pool_scale, w_out, ffn2_norm, ffn2_w1, ffn2_w3, ffn2_w2, final_gain, cos_t, sin_t):
    T, D = h.shape
    d_ff = ffn1_w1.shape[1]
    in_width = w_in.shape[1]
    assert in_width == 3 * N_GROUPS * GROUP_WIDTH + POOL_WIDTH + 2 * D
    bf = lambda w: w.astype(BF16)
    row = lambda g: g.reshape(1, -1).astype(F32)
    whole = lambda a: (a, _resident(a.shape))

    tf = FFN_TOKEN_TILE
    ffn_steps = T // tf
    wide = lambda width: pl.BlockSpec((tf, width), lambda i: (i, 0))
    cast_weights = (ffn2_w1, ffn2_w3, ffn2_w2, w_out, w_branch_attn, w_in, w_branch_pool,
                    pool_w.reshape(-1, POOL_GROUP_WIDTH))

    def cast_spec(w):
        rows = next(r for r in range(BF16_SUBLANES, w.shape[0] + 1, BF16_SUBLANES)
                    if w.shape[0] % r == 0 and w.shape[0] // r <= ffn_steps)
        return pl.BlockSpec((rows, w.shape[1]), lambda i: (jnp.minimum(i, w.shape[0] // rows - 1), 0))

    cast_specs = [cast_spec(w) for w in cast_weights]
    h1, um, *cast_done = _pallas_call(
        functools.partial(_ffn1_kernel, n_cast=len(cast_weights)), "ffn1", (ffn_steps,),
        inputs=[(h, wide(D)), whole(row(ffn1_norm)), whole(ffn1_w1), whole(ffn1_w3), whole(ffn1_w2),
                whole(row(mix_norm))] + list(zip(cast_weights, cast_specs)),
        outputs=[(jax.ShapeDtypeStruct((T, D), F32), wide(D)), (jax.ShapeDtypeStruct((T, D), BF16), wide(D))]
        + [(jax.ShapeDtypeStruct(w.shape, BF16), spec) for w, spec in zip(cast_weights, cast_specs)],
        scratch=[pltpu.VMEM((tf, d_ff), BF16)])
    ffn2_w1_bf, ffn2_w3_bf, ffn2_w2_bf, w_out_bf, w_branch_attn_bf, w_in_bf, w_branch_pool_bf, pool_w_bf = cast_done
    pool_w_bf = pool_w_bf.reshape(pool_w.shape)

    tp = PROJ_TOKEN_TILE
    proj_tiles_per_seq = S // tp
    ptok = lambda width: pl.BlockSpec((tp, width), lambda i: (i, 0))
    table = pl.BlockSpec((tp, LANES), lambda i: (i % proj_tiles_per_seq, 0))
    qkv_outputs = [(jax.ShapeDtypeStruct((B, r, S // r, GROUP_WIDTH), BF16),
                    pl.BlockSpec((1, r, tp // r, GROUP_WIDTH),
                                 lambda i: (i // proj_tiles_per_seq, 0, i % proj_tiles_per_seq, 0)))
                   for _, r in DILATED_GROUPS] * 3
    slab_slots = sum(_slots_per_half(r) for _, r in DILATED_GROUPS) * GROUP_WIDTH // LANES
    outs = _pallas_call(
        functools.partial(_proj_kernel, tiles_per_seq=proj_tiles_per_seq), "proj", (T // tp,),
        inputs=[(um, ptok(D)), whole(w_in_bf), (cos_t, table), (sin_t, table),
                whole(pool_w_bf), whole(row(pool_scale)), whole(w_branch_pool_bf)],
        outputs=qkv_outputs + [(jax.ShapeDtypeStruct((T, D), BF16), ptok(D))] * 2,
        scratch=[pltpu.VMEM((slab_slots, tp, LANES), F32), pltpu.VMEM((POOL_HALO + tp, POOL_WIDTH), F32)])
    qs, ks, vs = outs[0:3], outs[3:6], outs[6:9]
    gate_attn, gated_pool = outs[9], outs[10]

    tm = TOKEN_TILE
    tiles_per_seq = S // tm
    n_tiles = T // tm
    tok = lambda width: pl.BlockSpec((tm, width), lambda i: (i, 0))

    def residue_major(r):
        return pl.BlockSpec((1, r, tm // r, GROUP_WIDTH), lambda i: (i // tiles_per_seq, 0, i % tiles_per_seq, 0))

    attn_o, attn_lse = [], []
    rows = ATTN_ROWS
    blocks_per_step = rows // ATTN_BLOCK
    steps_per_batch = S // rows
    for g, (window, r) in enumerate(DILATED_GROUPS):
        L = S // r
        flat = lambda a: a.reshape(B, S, GROUP_WIDTH)
        cur = pl.BlockSpec((1, rows, GROUP_WIDTH), lambda b, s: (b, s, 0))
        prev = pl.BlockSpec((1, ATTN_BLOCK, GROUP_WIDTH),
                            lambda b, s: (b, jnp.maximum(s * blocks_per_step - 1, 0), 0))
        o_g, lse_g = _pallas_call(
            functools.partial(_attn_kernel, blocks_per_residue=L // ATTN_BLOCK, n_back=window // r),
            f"dilated_attn_g{g}", (B, steps_per_batch),
            inputs=[(flat(qs[g]), cur), (flat(ks[g]), cur), (flat(vs[g]), cur), (flat(ks[g]), prev),
                    (flat(vs[g]), prev)],
            outputs=[(jax.ShapeDtypeStruct((B, S, GROUP_WIDTH), BF16), cur),
                     (jax.ShapeDtypeStruct((B, S, GROUP_WIDTH), F32), cur)],
            scratch=[pltpu.VMEM((2, HEADS_PER_GROUP * ATTN_BLOCK, 2 * ATTN_BLOCK), F32),
                     pltpu.VMEM((HEADS_PER_GROUP * ATTN_BLOCK, 2 * ATTN_BLOCK), BF16)])
        attn_o.append(o_g.reshape(B, r, L, GROUP_WIDTH))
        attn_lse.append(lse_g.reshape(B, r, L, GROUP_WIDTH))

    o_specs = [residue_major(r) for _, r in DILATED_GROUPS]
    (out,) = _pallas_call(
        _merge_ffn2_kernel, "merge_ffn2", (n_tiles,),
        inputs=[(h1, tok(D))] + list(zip(attn_o, o_specs)) + list(zip(attn_lse, o_specs))
        + [(gate_attn, tok(D)), (gated_pool, tok(D)), whole(w_branch_attn_bf), whole(w_out_bf),
           whole(row(ffn2_norm)), whole(ffn2_w1_bf), whole(ffn2_w3_bf), whole(ffn2_w2_bf), whole(row(final_gain))],
        outputs=[(jax.ShapeDtypeStruct((T, D), F32), tok(D))],
        scratch=[pltpu.VMEM((tm, d_ff), BF16),
                 pltpu.VMEM((2 * (N_GROUPS - 1) * GROUP_WIDTH // LANES, tm, LANES), F32)])
    return out


def kernel(x, ffn1_norm, ffn1_w1, ffn1_w3, ffn1_w2, mix_norm, w_in, w_branch_attn, w_branch_pool, pool_w,
           pool_scale, w_out, ffn2_norm, ffn2_w1, ffn2_w3, ffn2_w2, final_norm):
    B, S, D = x.shape
    depth = ffn1_norm.shape[0]
    assert all(S % t == 0 for t in (TOKEN_TILE, FFN_TOKEN_TILE, PROJ_TOKEN_TILE, ATTN_ROWS))
    assert all(window // r == ATTN_BLOCK and (S // r) % ATTN_BLOCK == 0 for window, r in DILATED_GROUPS)
    assert all(t % (BF16_SUBLANES * r) == 0 for t in (TOKEN_TILE, PROJ_TOKEN_TILE) for _, r in DILATED_GROUPS)
    cos_t, sin_t = _rope_tables(S)
    h = x.reshape(B * S, D)
    for l in range(depth):
        assert l == depth - 1, "only the last layer's output norm is implemented"
        h = _layer(h, B, S, ffn1_norm[l], ffn1_w1[l], ffn1_w3[l], ffn1_w2[l], mix_norm[l], w_in[l],
                   w_branch_attn[l], w_branch_pool[l], pool_w[l], pool_scale[l], w_out[l], ffn2_norm[l],
                   ffn2_w1[l], ffn2_w3[l], ffn2_w2[l], final_norm, cos_t, sin_t)
    return h.reshape(B, S, D)
```

```python
import functools

import jax
import jax.numpy as jnp
import numpy as np
from jax import lax
from jax.experimental import pallas as pl
from jax.experimental.pallas import tpu as pltpu

F32 = jnp.float32
BF16 = jnp.bfloat16

HEAD_DIM = 64
HEADS_PER_GROUP = 4
GROUP_WIDTH = HEADS_PER_GROUP * HEAD_DIM
DILATED_GROUPS = ((128, 1), (512, 4), (2048, 16))
N_GROUPS = len(DILATED_GROUPS)
ATTN_BLOCK = 128
ROT_DIM = HEAD_DIM // 4
ROPE_THETA = 500000.0
POOL_WINDOWS = (2, 4, 8, 16)
POOL_GROUP_WIDTH = 128
POOL_WIDTH = len(POOL_WINDOWS) * POOL_GROUP_WIDTH
POOL_HALO = max(POOL_WINDOWS)
RMS_EPS = 1e-6
SCORE_SCALE_LOG2 = HEAD_DIM ** -0.5 * 1.4426950408889634

LANES = 128
BF16_SUBLANES = 16
V7X_SCOPED_VMEM_MAX_BYTES = 60000 * 1024
COMPILER_TEMP_BYTES = 4 * 1024 * 1024

TOKEN_TILE = 512
FFN_TOKEN_TILE = 512
PROJ_TOKEN_TILE = 1024
FF_CHUNK = 256
ATTN_ROWS = 4096
DEINTERLEAVE_STEP = 4


def _rms(x, gain):
    return x * lax.rsqrt(jnp.mean(x * x, axis=-1, keepdims=True) + RMS_EPS) * gain


def _sigmoid(x):
    return 0.5 * jnp.tanh(0.5 * x) + 0.5


def _swiglu(u, w1_ref, w3_ref, w2_ref, hid_ref, before_chunk=lambda n: None):
    d_ff = w1_ref.shape[1]
    n_chunks = d_ff // FF_CHUNK
    for f in range(n_chunks):
        cols = slice(f * FF_CHUNK, (f + 1) * FF_CHUNK)
        before_chunk(f)
        w13 = jnp.concatenate([w1_ref[:, cols].astype(BF16), w3_ref[:, cols].astype(BF16)], axis=1)
        ab = jnp.dot(u, w13, preferred_element_type=F32)
        a, b = ab[:, :FF_CHUNK], ab[:, FF_CHUNK:]
        half = 0.5 * a
        hid_ref[:, cols] = ((half * jnp.tanh(half) + half) * b).astype(BF16)
    if w2_ref.dtype == BF16:
        for f in range(n_chunks):
            before_chunk(n_chunks + f)
        return jnp.dot(hid_ref[...], w2_ref[...], preferred_element_type=F32)
    down = None
    for f in range(n_chunks):
        rows = slice(f * FF_CHUNK, (f + 1) * FF_CHUNK)
        before_chunk(n_chunks + f)
        part = jnp.dot(hid_ref[:, rows], w2_ref[rows, :].astype(BF16), preferred_element_type=F32)
        down = part if down is None else down + part
    return down


def _slots_per_half(r):
    return 0 if r == 1 else 1 if r <= DEINTERLEAVE_STEP else 2


def _rope(x, cos, sin):
    j = lax.broadcasted_iota(jnp.int32, x.shape, 1) % HEAD_DIM
    half = ROT_DIM // 2
    partner = jnp.where(j < half, pltpu.roll(x, LANES - half, axis=1), pltpu.roll(x, half, axis=1))
    return jnp.where(j < ROT_DIM, x * cos + partner * sin, x)


def _pool_deltas(pext_ref, first_pos):
    tm = pext_ref.shape[0] - POOL_HALO
    pos = first_pos + lax.broadcasted_iota(jnp.int32, (tm, 1), 0)
    deltas = []
    for gi, win in enumerate(POOL_WINDOWS):
        lanes = slice(gi * POOL_GROUP_WIDTH, (gi + 1) * POOL_GROUP_WIDTH)
        own = pext_ref[POOL_HALO:POOL_HALO + tm, lanes]
        total = own
        for back in range(1, win):
            total = total + pext_ref[POOL_HALO - back:POOL_HALO - back + tm, lanes]
        count = jnp.minimum(pos + 1, win).astype(F32)
        deltas.append((total / count - own).astype(BF16))
    return deltas


def _pool_project(deltas, poolw_ref, pscale_ref, wb_ref):
    pooled = []
    for gi, d in enumerate(deltas):
        lanes = slice(gi * POOL_GROUP_WIDTH, (gi + 1) * POOL_GROUP_WIDTH)
        pooled.append(jnp.dot(d, poolw_ref[gi], preferred_element_type=F32) * pscale_ref[:, lanes])
    return jnp.dot(jnp.concatenate(pooled, axis=1).astype(BF16), wb_ref[...], preferred_element_type=F32)


def _swiglu_weight_copies(hbm_refs, vmem_refs, sem):
    (w1_hbm, w3_hbm, w2_hbm), (w1_ref, w3_ref, w2_ref) = hbm_refs, vmem_refs
    n_chunks = w1_ref.shape[1] // FF_CHUNK
    chunk = lambda f: slice(f * FF_CHUNK, (f + 1) * FF_CHUNK)
    groups = [[pltpu.make_async_copy(w1_hbm.at[:, chunk(f)], w1_ref.at[:, chunk(f)], sem.at[0, f]),
               pltpu.make_async_copy(w3_hbm.at[:, chunk(f)], w3_ref.at[:, chunk(f)], sem.at[1, f])]
              for f in range(n_chunks)]
    return groups + [[pltpu.make_async_copy(w2_hbm.at[chunk(f), :], w2_ref.at[chunk(f), :], sem.at[2, f])]
                     for f in range(n_chunks)]


def _first_step_and_rest(groups, step):
    first = pl.program_id(0) == 0

    order = list(groups)

    def wait(key):
        k = order.index(key)
        if k % 2 == 0:
            for pair_key in order[k:k + 2]:
                for c in groups[pair_key]:
                    c.wait()

    @pl.when(first)
    def _():
        for key in groups:
            for c in groups[key]:
                c.start()
        step(wait)

    @pl.when(jnp.logical_not(first))
    def _():
        step(lambda key: None)


def _ffn1_kernel(x_ref, g1_ref, w1_hbm, w3_hbm, w2_hbm, gm_ref, *rest, n_cast):
    cast_in, (h1_ref, um_ref), cast_out = rest[:n_cast], rest[n_cast:n_cast + 2], rest[n_cast + 2:2 * n_cast + 2]
    hid_ref, w1_ref, w3_ref, w2_ref, sem = rest[2 * n_cast + 2:]
    groups = dict(enumerate(_swiglu_weight_copies((w1_hbm, w3_hbm, w2_hbm), (w1_ref, w3_ref, w2_ref), sem)))

    def step(before_chunk):
        for w_ref, w_bf_ref in zip(cast_in, cast_out):
            w_bf_ref[...] = w_ref[...].astype(BF16)
        x = x_ref[...]
        u = _rms(x, g1_ref[...]).astype(BF16)
        h1 = x + 0.5 * _swiglu(u, w1_ref, w3_ref, w2_ref, hid_ref, before_chunk)
        h1_ref[...] = h1
        um_ref[...] = _rms(h1, gm_ref[...]).astype(BF16)

    _first_step_and_rest(groups, step)


def _proj_kernel(um_ref, win_ref, cos_ref, sin_ref, poolw_ref, pscale_ref, wb_ref,
                 q0_ref, q1_ref, q2_ref, k0_ref, k1_ref, k2_ref, v0_ref, v1_ref, v2_ref,
                 ga_ref, gpool_ref, slab_ref, pext_ref, *, tiles_per_seq):
    tm = um_ref.shape[0]
    d_model = um_ref.shape[1]
    t = pl.program_id(0) % tiles_per_seq

    @pl.when(t == 0)
    def _():
        pext_ref[0:POOL_HALO, :] = jnp.zeros((POOL_HALO, POOL_WIDTH), F32)

    um = um_ref[...]
    cos = cos_ref[...]
    sin = sin_ref[...]

    def project(col, width):
        return jnp.dot(um, win_ref[:, col:col + width], preferred_element_type=F32)

    def rope(p):
        return jnp.concatenate([_rope(p[:, :LANES], cos, sin), _rope(p[:, LANES:], cos, sin)], axis=1)

    qkv_width = 3 * N_GROUPS * GROUP_WIDTH
    gate_col = qkv_width + POOL_WIDTH
    halves = GROUP_WIDTH // LANES
    first_slot = [halves * sum(_slots_per_half(r) for _, r in DILATED_GROUPS[:g]) for g in range(N_GROUPS)]

    def store_residue_major(p, ref, r, scratch_slot):
        if r == 1:
            ref[0, 0] = p.astype(BF16)
            return
        for half in range(halves):
            lanes = slice(half * LANES, (half + 1) * LANES)
            first = scratch_slot + half * _slots_per_half(r)
            slab_ref[first] = p[:, lanes]
            if _slots_per_half(r) == 1:
                for c in range(r):
                    ref[0, c, :, lanes] = slab_ref[first, pl.ds(c, tm // r, stride=r), :].astype(BF16)
                continue
            step, rest = DEINTERLEAVE_STEP, r // DEINTERLEAVE_STEP
            for c0 in range(step):
                slab_ref[first + 1, c0 * (tm // step):(c0 + 1) * (tm // step), :] = (
                    slab_ref[first, pl.ds(c0, tm // step, stride=step), :])
            for c0 in range(step):
                for c1 in range(rest):
                    ref[0, c1 * step + c0, :, lanes] = slab_ref[
                        first + 1, pl.ds(c0 * (tm // step) + c1, tm // r, stride=rest), :].astype(BF16)

    def qkv(kind, refs, group_order):
        both = project(("q", "k", "v").index(kind) * N_GROUPS * GROUP_WIDTH, N_GROUPS * GROUP_WIDTH)
        for g in group_order:
            p = both[:, g * GROUP_WIDTH:(g + 1) * GROUP_WIDTH]
            if kind == "q":
                p = rope(p) * SCORE_SCALE_LOG2
            elif kind == "k":
                p = rope(p)
            store_residue_major(p, refs[g], DILATED_GROUPS[g][1], first_slot[g])

    pext_ref[POOL_HALO:, :] = project(qkv_width, POOL_WIDTH)
    deltas = _pool_deltas(pext_ref, t * tm)
    pext_ref[0:POOL_HALO, :] = pext_ref[tm:tm + POOL_HALO, :]
    pair = 2 * GROUP_WIDTH
    for s in range(d_model // pair):
        lanes = slice(s * pair, (s + 1) * pair)
        ga_ref[:, lanes] = _sigmoid(project(gate_col + s * pair, pair)).astype(BF16)
    qkv("q", (q0_ref, q1_ref, q2_ref), range(N_GROUPS))
    y_pool = _pool_project(deltas, poolw_ref, pscale_ref, wb_ref)
    for s in range(d_model // pair):
        lanes = slice(s * pair, (s + 1) * pair)
        gate_pool = _sigmoid(project(gate_col + d_model + s * pair, pair))
        gpool_ref[:, lanes] = (gate_pool * y_pool[:, lanes]).astype(BF16)
    qkv("k", (k0_ref, k1_ref, k2_ref), reversed(range(N_GROUPS)))
    qkv("v", (v0_ref, v1_ref, v2_ref), reversed(range(N_GROUPS)))


def _attn_kernel(q_ref, k_ref, v_ref, kh_ref, vh_ref, o_ref, lse_ref, s_ref, p_ref, *,
                 blocks_per_residue, n_back):
    rows = q_ref.shape[1]
    nblk = rows // ATTN_BLOCK
    step = pl.program_id(1)
    qi = lax.broadcasted_iota(jnp.int32, (ATTN_BLOCK, 2 * ATTN_BLOCK), 0)
    kj = lax.broadcasted_iota(jnp.int32, (ATTN_BLOCK, 2 * ATTN_BLOCK), 1)
    dist = qi + ATTN_BLOCK - kj
    band = (dist >= 0) & (dist <= n_back)
    in_current = kj >= ATTN_BLOCK
    head_of_lane = lax.broadcasted_iota(jnp.int32, (1, GROUP_WIDTH), 1) // HEAD_DIM
    heads = range(HEADS_PER_GROUP)
    heads_per_vreg = LANES // HEAD_DIM

    def head_rows(h):
        return slice(h * ATTN_BLOCK, (h + 1) * ATTN_BLOCK)

    def keys_values(j, ref, halo_ref):
        if j == 0:
            return jnp.concatenate([halo_ref[0], ref[0, 0:ATTN_BLOCK, :]], axis=0)
        return ref[0, (j - 1) * ATTN_BLOCK:(j + 1) * ATTN_BLOCK, :]

    def scores(j):
        qb = q_ref[0, j * ATTN_BLOCK:(j + 1) * ATTN_BLOCK, :]
        qs = jnp.concatenate([jnp.where(head_of_lane == h, qb, jnp.zeros_like(qb)) for h in heads], axis=0)
        return lax.dot_general(qs, keys_values(j, k_ref, kh_ref), (((1,), (1,)), ((), ())),
                               preferred_element_type=F32)

    def lanes_by_head(per_head):
        first_in_vreg = (head_of_lane[:, :LANES] % heads_per_vreg) == 0
        vregs = [jnp.where(first_in_vreg, per_head[v * heads_per_vreg], per_head[v * heads_per_vreg + 1])
                 for v in range(GROUP_WIDTH // LANES)]
        return jnp.concatenate(vregs, axis=1)

    s_ref[0] = scores(0)
    for j in range(nblk):
        if j + 1 < nblk:
            s_ref[(j + 1) % 2] = scores(j + 1)
        has_prev = ((step * nblk + j) % blocks_per_residue) != 0
        allowed = band & (in_current | has_prev)
        inv_l, lse = [], []
        for h in heads:
            s = jnp.where(allowed, s_ref[j % 2, head_rows(h), :], -jnp.inf)
            m = jnp.max(s, axis=-1, keepdims=True)
            p = jnp.exp2(s - m)
            l = jnp.sum(p, axis=-1, keepdims=True)
            p_ref[head_rows(h), :] = p.astype(BF16)
            inv_l.append(1.0 / l)
            lse.append(m + jnp.log2(l))
        pv = jnp.dot(p_ref[...], keys_values(j, v_ref, vh_ref), preferred_element_type=F32)
        o_heads = [pv[head_rows(h), (h // heads_per_vreg) * LANES:(h // heads_per_vreg + 1) * LANES] * inv_l[h]
                   for h in heads]
        cur = slice(j * ATTN_BLOCK, (j + 1) * ATTN_BLOCK)
        o_ref[0, cur, :] = lanes_by_head(o_heads).astype(BF16)
        lse_ref[0, cur, :] = lanes_by_head([jnp.broadcast_to(x, (ATTN_BLOCK, LANES)) for x in lse])


def _merge_ffn2_kernel(h1_ref, o0_ref, o1_ref, o2_ref, l0_ref, l1_ref, l2_ref, ga_ref, gpool_ref,
                       wa_ref, wo_ref, g2_ref, w1_ref, w3_ref, w2_ref, gf_ref,
                       out_ref, hid_ref, nat_ref):
    tm = h1_ref.shape[0]

    def natural_order(ref, r, slot):
        if r == 1:
            return ref[0, 0].astype(F32)
        halves = []
        for half in range(GROUP_WIDTH // LANES):
            lanes = slice(half * LANES, (half + 1) * LANES)
            dst = slot + half * _slots_per_half(r)
            if _slots_per_half(r) == 1:
                for c in range(r):
                    nat_ref[dst, pl.ds(c, tm // r, stride=r), :] = ref[0, c, :, lanes].astype(F32)
            else:
                step, rest = DEINTERLEAVE_STEP, r // DEINTERLEAVE_STEP
                quarter = tm // step
                for c0 in range(step):
                    for c1 in range(rest):
                        nat_ref[dst + 1, pl.ds(c0 * quarter + c1, tm // r, stride=rest), :] = (
                            ref[0, c1 * step + c0, :, lanes].astype(F32))
                for c0 in range(step):
                    nat_ref[dst, pl.ds(c0, quarter, stride=step), :] = (
                        nat_ref[dst + 1, c0 * quarter:(c0 + 1) * quarter, :])
            halves.append(nat_ref[dst])
        return jnp.concatenate(halves, axis=1)

    outs, lses = [], []
    slot = 0
    for (_, r), o_ref, l_ref in zip(DILATED_GROUPS, (o0_ref, o1_ref, o2_ref), (l0_ref, l1_ref, l2_ref)):
        slots_per_slab = _slots_per_half(r) * GROUP_WIDTH // LANES
        outs.append(natural_order(o_ref, r, slot))
        lses.append(natural_order(l_ref, r, slot + slots_per_slab))
        slot += 2 * slots_per_slab
    top = functools.reduce(jnp.maximum, lses)
    weights = [jnp.exp2(l - top) for l in lses]
    o = sum(w * og for w, og in zip(weights, outs)) / sum(weights)
    y_attn = jnp.dot(o.astype(BF16), wa_ref[...], preferred_element_type=F32)

    merged = ga_ref[...].astype(F32) * y_attn + gpool_ref[...].astype(F32)
    h2 = h1_ref[...] + jnp.dot(merged.astype(BF16), wo_ref[...], preferred_element_type=F32)
    u2 = _rms(h2, g2_ref[...]).astype(BF16)
    h3 = h2 + 0.5 * _swiglu(u2, w1_ref, w3_ref, w2_ref, hid_ref)
    out_ref[...] = _rms(h3, gf_ref[...])


def _resident(shape):
    return pl.BlockSpec(shape, lambda *_: (0,) * len(shape), pipeline_mode=pl.Buffered(1))


def _pallas_call(body, name, grid, inputs, outputs, scratch, semaphores=()):
    def window_bytes(aval, spec):
        if spec.block_shape is None:
            return 0
        buffers = 1 if isinstance(spec.pipeline_mode, pl.Buffered) and spec.pipeline_mode.buffer_count == 1 else 2
        elems = 1
        for d in spec.block_shape:
            elems *= d
        return buffers * elems * jnp.dtype(aval.dtype).itemsize

    request = COMPILER_TEMP_BYTES + sum(window_bytes(a, spec) for a, spec in (*inputs, *outputs))
    for buf in scratch:
        elems = 1
        for d in buf.shape:
            elems *= d
        request += elems * jnp.dtype(buf.dtype).itemsize
    assert request <= V7X_SCOPED_VMEM_MAX_BYTES, (name, request)
    return pl.pallas_call(
        body, name=name, grid=grid,
        in_specs=[spec for _, spec in inputs], out_specs=[spec for _, spec in outputs],
        out_shape=[shape for shape, _ in outputs], scratch_shapes=[*scratch, *semaphores],
        compiler_params=pltpu.CompilerParams(dimension_semantics=("arbitrary",) * len(grid),
                                             vmem_limit_bytes=V7X_SCOPED_VMEM_MAX_BYTES),
    )(*[a for a, _ in inputs])


def _rope_tables(seq):
    half = ROT_DIM // 2
    j = np.arange(LANES) % HEAD_DIM
    inv = np.where(j < ROT_DIM, ROPE_THETA ** (-(2.0 * (j % half)) / ROT_DIM), 0.0)
    sign = np.where(j < half, -1.0, 1.0)
    ang = np.arange(seq, dtype=np.float64)[:, None] * inv[None, :]
    return jnp.asarray(np.cos(ang), F32), jnp.asarray(np.sin(ang) * sign[None, :], F32)


def _layer(h, B, S, ffn1_norm, ffn1_w1, ffn1_w3, ffn1_w2, mix_norm, w_in, w_branch_attn, w_branch_pool,
           pool_w, pool_scale, w_out, ffn2_norm, ffn2_w1, ffn2_w3, ffn2_w2, final_gain, cos_t, sin_t):
    T, D = h.shape
    d_ff = ffn1_w1.shape[1]
    in_width = w_in.shape[1]
    assert in_width == 3 * N_GROUPS * GROUP_WIDTH + POOL_WIDTH + 2 * D
    bf = lambda w: w.astype(BF16)
    row = lambda g: g.reshape(1, -1).astype(F32)
    whole = lambda a: (a, _resident(a.shape))
    in_hbm = lambda a: (a, pl.BlockSpec(memory_space=pl.ANY))

    tf = FFN_TOKEN_TILE
    ffn_steps = T // tf
    wide = lambda width: pl.BlockSpec((tf, width), lambda i: (i, 0))
    cast_weights = (ffn2_w1, ffn2_w3, ffn2_w2, w_out, w_branch_attn, w_in, w_branch_pool,
                    pool_w.reshape(-1, POOL_GROUP_WIDTH))

    def cast_spec(w):
        rows = next(r for r in range(BF16_SUBLANES, w.shape[0] + 1, BF16_SUBLANES)
                    if w.shape[0] % r == 0 and w.shape[0] // r <= ffn_steps)
        return pl.BlockSpec((rows, w.shape[1]), lambda i: (jnp.minimum(i, w.shape[0] // rows - 1), 0))

    cast_specs = [cast_spec(w) for w in cast_weights]
    h1, um, *cast_done = _pallas_call(
        functools.partial(_ffn1_kernel, n_cast=len(cast_weights)), "ffn1", (ffn_steps,),
        inputs=[(h, wide(D)), whole(row(ffn1_norm)), in_hbm(ffn1_w1), in_hbm(ffn1_w3), in_hbm(ffn1_w2),
                whole(row(mix_norm))] + list(zip(cast_weights, cast_specs)),
        outputs=[(jax.ShapeDtypeStruct((T, D), F32), wide(D)), (jax.ShapeDtypeStruct((T, D), BF16), wide(D))]
        + [(jax.ShapeDtypeStruct(w.shape, BF16), spec) for w, spec in zip(cast_weights, cast_specs)],
        scratch=[pltpu.VMEM((tf, d_ff), BF16)] + [pltpu.VMEM(w.shape, F32) for w in (ffn1_w1, ffn1_w3, ffn1_w2)],
        semaphores=[pltpu.SemaphoreType.DMA((3, d_ff // FF_CHUNK))])
    ffn2_w1_bf, ffn2_w3_bf, ffn2_w2_bf, w_out_bf, w_branch_attn_bf, w_in_bf, w_branch_pool_bf, pool_w_bf = cast_done
    pool_w_bf = pool_w_bf.reshape(pool_w.shape)

    tp = PROJ_TOKEN_TILE
    proj_tiles_per_seq = S // tp
    ptok = lambda width: pl.BlockSpec((tp, width), lambda i: (i, 0))
    table = pl.BlockSpec((tp, LANES), lambda i: (i % proj_tiles_per_seq, 0))
    qkv_outputs = [(jax.ShapeDtypeStruct((B, r, S // r, GROUP_WIDTH), BF16),
                    pl.BlockSpec((1, r, tp // r, GROUP_WIDTH),
                                 lambda i: (i // proj_tiles_per_seq, 0, i % proj_tiles_per_seq, 0)))
                   for _, r in DILATED_GROUPS] * 3
    slab_slots = sum(_slots_per_half(r) for _, r in DILATED_GROUPS) * GROUP_WIDTH // LANES
    outs = _pallas_call(
        functools.partial(_proj_kernel, tiles_per_seq=proj_tiles_per_seq), "proj", (T // tp,),
        inputs=[(um, ptok(D)), whole(w_in_bf), (cos_t, table), (sin_t, table),
                whole(pool_w_bf), whole(row(pool_scale)), whole(w_branch_pool_bf)],
        outputs=qkv_outputs + [(jax.ShapeDtypeStruct((T, D), BF16), ptok(D))] * 2,
        scratch=[pltpu.VMEM((slab_slots, tp, LANES), F32), pltpu.VMEM((POOL_HALO + tp, POOL_WIDTH), F32)])
    qs, ks, vs = outs[0:3], outs[3:6], outs[6:9]
    gate_attn, gated_pool = outs[9], outs[10]

    tm = TOKEN_TILE
    tiles_per_seq = S // tm
    n_tiles = T // tm
    tok = lambda width: pl.BlockSpec((tm, width), lambda i: (i, 0))

    def residue_major(r):
        return pl.BlockSpec((1, r, tm // r, GROUP_WIDTH), lambda i: (i // tiles_per_seq, 0, i % tiles_per_seq, 0))

    attn_o, attn_lse = [], []
    rows = ATTN_ROWS
    blocks_per_step = rows // ATTN_BLOCK
    steps_per_batch = S // rows
    for g, (window, r) in enumerate(DILATED_GROUPS):
        L = S // r
        flat = lambda a: a.reshape(B, S, GROUP_WIDTH)
        cur = pl.BlockSpec((1, rows, GROUP_WIDTH), lambda b, s: (b, s, 0))
        prev = pl.BlockSpec((1, ATTN_BLOCK, GROUP_WIDTH),
                            lambda b, s: (b, jnp.maximum(s * blocks_per_step - 1, 0), 0))
        o_g, lse_g = _pallas_call(
            functools.partial(_attn_kernel, blocks_per_residue=L // ATTN_BLOCK, n_back=window // r),
            f"dilated_attn_g{g}", (B, steps_per_batch),
            inputs=[(flat(qs[g]), cur), (flat(ks[g]), cur), (flat(vs[g]), cur), (flat(ks[g]), prev),
                    (flat(vs[g]), prev)],
            outputs=[(jax.ShapeDtypeStruct((B, S, GROUP_WIDTH), BF16), cur),
                     (jax.ShapeDtypeStruct((B, S, GROUP_WIDTH), F32), cur)],
            scratch=[pltpu.VMEM((2, HEADS_PER_GROUP * ATTN_BLOCK, 2 * ATTN_BLOCK), F32),
                     pltpu.VMEM((HEADS_PER_GROUP * ATTN_BLOCK, 2 * ATTN_BLOCK), BF16)])
        attn_o.append(o_g.reshape(B, r, L, GROUP_WIDTH))
        attn_lse.append(lse_g.reshape(B, r, L, GROUP_WIDTH))

    o_specs = [residue_major(r) for _, r in DILATED_GROUPS]
    (out,) = _pallas_call(
        _merge_ffn2_kernel, "merge_ffn2", (n_tiles,),
        inputs=[(h1, tok(D))] + list(zip(attn_o, o_specs)) + list(zip(attn_lse, o_specs))
        + [(gate_attn, tok(D)), (gated_pool, tok(D)), whole(w_branch_attn_bf), whole(w_out_bf),
           whole(row(ffn2_norm)), whole(ffn2_w1_bf), whole(ffn2_w3_bf), whole(ffn2_w2_bf), whole(row(final_gain))],
        outputs=[(jax.ShapeDtypeStruct((T, D), F32), tok(D))],
        scratch=[pltpu.VMEM((tm, d_ff), BF16),
                 pltpu.VMEM((2 * slab_slots, tm, LANES), F32)])
    return out


def kernel(x, ffn1_norm, ffn1_w1, ffn1_w3, ffn1_w2, mix_norm, w_in, w_branch_attn, w_branch_pool, pool_w,
           pool_scale, w_out, ffn2_norm, ffn2_w1, ffn2_w3, ffn2_w2, final_norm):
    B, S, D = x.shape
    depth = ffn1_norm.shape[0]
    assert all(S % t == 0 for t in (TOKEN_TILE, FFN_TOKEN_TILE, PROJ_TOKEN_TILE, ATTN_ROWS))
    assert all(window // r == ATTN_BLOCK and (S // r) % ATTN_BLOCK == 0 for window, r in DILATED_GROUPS)
    assert all(t % (BF16_SUBLANES * r) == 0 for t in (TOKEN_TILE, PROJ_TOKEN_TILE) for _, r in DILATED_GROUPS)
    cos_t, sin_t = _rope_tables(S)
    h = x.reshape(B * S, D)
    for l in range(depth):
        assert l == depth - 1, "only the last layer's output norm is implemented"
        h = _layer(h, B, S, ffn1_norm[l], ffn1_w1[l], ffn1_w3[l], ffn1_w2[l], mix_norm[l], w_in[l],
                   w_branch_attn[l], w_branch_pool[l], pool_w[l], pool_scale[l], w_out[l], ffn2_norm[l],
                   ffn2_w1[l], ffn2_w3[l], ffn2_w2[l], final_norm, cos_t, sin_t)
    return h.reshape(B, S, D)
```
